```python
import math
import jax, jax.numpy as jnp
from jax import lax
import numpy as np

D_MODEL = 1024
BATCH = 8
SEQ = 2048
DEPTH = 2
DEC_BATCH = 32
DEC_SEQ = 4
PAST_LEN = 16384
PAGE_SIZE = 128

FOX_H = 8
FOX_DH = 64
FOX_W = FOX_H * FOX_DH
ML_H = 4
ML_DH = 128
ML_W = ML_H * ML_DH
ML_CONV = 4
ML_CHUNK = 64
MIX_W = FOX_W + ML_W
Q_BLOCK = 128
N_MEM = 256
X_H = 4
X_DH = 128
X_W = X_H * X_DH
D_FF = 2816
FFN_CONV = 3
EPS = 1e-6
IN_COLS = 3 * FOX_W + FOX_H + 4 * ML_W + 2 * ML_H

kernel_name = "hybrid_fox_mlstm_decoder_step"


def _in_split_points():
    sizes = (FOX_W, FOX_W, FOX_W, FOX_H, ML_W, ML_W, ML_W, ML_H, ML_H, ML_W)
    pts, acc = [], 0
    for s in sizes[:-1]:
        acc += s
        pts.append(acc)
    return pts


def rmsnorm(x, g):
    xf = x.astype(jnp.float32)
    y = xf * lax.rsqrt(jnp.mean(xf * xf, axis=-1, keepdims=True) + EPS)
    return (y * g.astype(jnp.float32)).astype(x.dtype)


def causal_dwconv(x, buf, w, b):
    width = w.shape[0]
    L = x.shape[1]
    xp = jnp.concatenate([buf.astype(x.dtype), x], axis=1)
    y = b.astype(x.dtype)
    for j in range(width):
        y = y + w[j].astype(x.dtype) * xp[:, j:j + L]
    return y, xp[:, L:]


def fox_block(qb, q_pos, Fq, k, v, Fk):
    T = k.shape[1]
    s = jnp.einsum('bqhd,bkhd->bhqk', qb, k).astype(jnp.float32) * (FOX_DH ** -0.5)
    bias = jnp.transpose(Fq, (0, 2, 1))[..., None] - jnp.transpose(Fk, (0, 2, 1))[:, :, None, :]
    mask = jnp.arange(T)[None, :] <= q_pos[:, None]
    s = jnp.where(mask, s + bias, -jnp.inf)
    p = jax.nn.softmax(s, axis=-1).astype(v.dtype)
    return jnp.einsum('bhqk,bkhd->bqhd', p, v)


def fox_prompt(q, k, v, logf):
    B, S = q.shape[:2]
    F = jnp.cumsum(logf, axis=1)
    nb = S // Q_BLOCK

    def blk(i):
        st = i * Q_BLOCK
        qb = lax.dynamic_slice_in_dim(q, st, Q_BLOCK, axis=1)
        Fq = lax.dynamic_slice_in_dim(F, st, Q_BLOCK, axis=1)
        return fox_block(qb, st + jnp.arange(Q_BLOCK), Fq, k, v, F)

    o = lax.map(blk, jnp.arange(nb))
    return jnp.moveaxis(o, 0, 1).reshape(B, S, FOX_H, FOX_DH)


def make_fox_sample(pk, pv, plf):
    def fn(q, k, v, logf):
        Lq = q.shape[1]
        P = pk.shape[1]
        F = jnp.cumsum(jnp.concatenate([plf.astype(jnp.float32), logf], axis=1), axis=1)
        Fq = F[:, P:]
        bias = jnp.transpose(Fq, (0, 2, 1))[..., None] - jnp.transpose(F, (0, 2, 1))[:, :, None, :]
        s_past = jnp.einsum('bqhd,bkhd->bhqk', q, pk.astype(q.dtype))
        s_new = jnp.einsum('bqhd,bkhd->bhqk', q, k)
        s = jnp.concatenate([s_past, s_new], axis=-1).astype(jnp.float32) * (FOX_DH ** -0.5) + bias
        mask = jnp.concatenate([jnp.ones((Lq, P), bool), jnp.tril(jnp.ones((Lq, Lq), bool))], axis=-1)
        s = jnp.where(mask, s, -jnp.inf)
        p = jax.nn.softmax(s, axis=-1).astype(v.dtype)
        return (jnp.einsum('bhqk,bkhd->bqhd', p[..., :P], pv.astype(v.dtype))
                + jnp.einsum('bhqk,bkhd->bqhd', p[..., P:], v))
    return fn


def mlstm_chunkwise(q, k, v, ig, lf, state, chunk):
    B, L, H, D = q.shape
    f32 = jnp.float32
    C0, n0, m0 = (a.astype(f32) for a in state)
    nc = L // chunk

    def to_chunks(a):
        return jnp.moveaxis(a.astype(f32).reshape((B, nc, chunk) + a.shape[2:]), 1, 0)

    xs = tuple(to_chunks(a) for a in (q, k, v, ig, lf))
    causal = jnp.tril(jnp.ones((chunk, chunk), bool))

    def step(carry, inp):
        C, n, m = carry
        qc, kc, vc, ic, fc = inp
        bT = jnp.transpose(jnp.cumsum(fc, axis=1), (0, 2, 1))
        iT = jnp.transpose(ic, (0, 2, 1))
        Dlog = bT[..., :, None] - bT[..., None, :] + iT[..., None, :]
        Dlog = jnp.where(causal, Dlog, -jnp.inf)
        g = bT + m[..., None]
        mt = jnp.maximum(g, jnp.max(Dlog, axis=-1))
        w_intra = jnp.exp(Dlog - mt[..., None])
        w_inter = jnp.exp(g - mt)
        qk = jnp.einsum('bthd,bshd->bhts', qc, kc) * w_intra
        wi = jnp.transpose(w_inter, (0, 2, 1))[..., None]
        num = jnp.einsum('bhts,bshd->bthd', qk, vc) + jnp.einsum('bthd,bhde->bthe', qc, C) * wi
        den = jnp.sum(qk, axis=-1) + w_inter * jnp.einsum('bthd,bhd->bht', qc, n)
        denom = jnp.maximum(jnp.abs(den), jnp.exp(-mt))
        h = num / jnp.transpose(denom, (0, 2, 1))[..., None]
        bL = bT[..., -1]
        a = bL[..., None] - bT + iT
        m_new = jnp.maximum(bL + m, jnp.max(a, axis=-1))
        ws = jnp.exp(a - m_new[..., None])
        decay = jnp.exp(bL + m - m_new)
        C_new = decay[..., None, None] * C + jnp.einsum('bhs,bshd,bshe->bhde', ws, kc, vc)
        n_new = decay[..., None] * n + jnp.einsum('bhs,bshd->bhd', ws, kc)
        return (C_new, n_new, m_new), h

    (C, n, m), hs = lax.scan(step, (C0, n0, m0), xs)
    h = jnp.moveaxis(hs, 0, 1).reshape(B, L, H, D)
    return h, (C, n, m)


def _layer(x, l, W, fox_fn, ml_state, ml_buf, ffn_buf, mem_k, mem_v):
    B, L, _ = x.shape
    f32 = jnp.float32
    h = rmsnorm(x, W['g_pre_mix'][l])
    z = h @ W['w_in'][l]
    fq, fk, fv, ff, mq, mk, mv, mi, mf, mo = jnp.split(z, _in_split_points(), axis=-1)
    fox_logf = jax.nn.log_sigmoid((ff + W['b_fox_f'][l]).astype(f32))
    fq = fq.reshape(B, L, FOX_H, FOX_DH)
    fk = fk.reshape(B, L, FOX_H, FOX_DH)
    fv = fv.reshape(B, L, FOX_H, FOX_DH)
    fox_o = fox_fn(fq, fk, fv, fox_logf).reshape(B, L, FOX_W)

    qk, ml_buf_new = causal_dwconv(jnp.concatenate([mq, mk], axis=-1), ml_buf,
                                   W['w_ml_conv'][l], W['b_ml_conv'][l])
    mq, mk = jnp.split(jax.nn.silu(qk), 2, axis=-1)
    ig = (mi + W['b_ml_i'][l]).astype(f32)
    lf = jax.nn.log_sigmoid((mf + W['b_ml_f'][l]).astype(f32))
    chunk = ML_CHUNK if L % ML_CHUNK == 0 else L
    h_ml, ml_new = mlstm_chunkwise(mq.reshape(B, L, ML_H, ML_DH),
                                   mk.reshape(B, L, ML_H, ML_DH) * (ML_DH ** -0.5),
                                   mv.reshape(B, L, ML_H, ML_DH), ig, lf, ml_state, chunk)
    h_ml = rmsnorm(h_ml, W['g_ml_head'][l].reshape(ML_H, ML_DH)) * \
        jax.nn.sigmoid(mo.astype(f32)).reshape(B, L, ML_H, ML_DH)
    mix = jnp.concatenate([fox_o, h_ml.reshape(B, L, ML_W).astype(x.dtype)], axis=-1) @ W['w_out'][l]
    x = x + rmsnorm(mix, W['g_post_mix'][l])
    h = rmsnorm(x, W['g_pre_x'][l])
    xq = (h @ W['w_xq'][l]).reshape(B, L, X_H, X_DH)
    s = jnp.einsum('blhd,bmhd->bhlm', xq, mem_k.astype(xq.dtype)).astype(f32) * (X_DH ** -0.5)
    p = jax.nn.softmax(s, axis=-1).astype(x.dtype)
    xo = jnp.einsum('bhlm,bmhd->blhd', p, mem_v.astype(x.dtype)).reshape(B, L, X_W) @ W['w_xo'][l]
    x = x + rmsnorm(xo, W['g_post_x'][l])
    h = rmsnorm(x, W['g_pre_ffn'][l])
    gt, ffn_buf_new = causal_dwconv(h @ W['w_gate'][l], ffn_buf, W['w_ffn_conv'][l], W['b_ffn_conv'][l])
    f = (jax.nn.gelu(gt) * (h @ W['w_up'][l])) @ W['w_down'][l]
    x = x + rmsnorm(f, W['g_post_ffn'][l])
    return x, (fk, fv, fox_logf, ml_new[0], ml_new[1], ml_new[2], ml_buf_new, ffn_buf_new)


def setup_inputs(seed: int = 0) -> dict:
    key = jax.random.key(seed)
    ks = iter(jax.random.split(key, 48))
    f32 = jnp.float32

    def nrm(shape, scale=1.0):
        return jax.random.normal(next(ks), shape, f32) * scale

    n_pages = PAST_LEN // PAGE_SIZE
    used = DEC_BATCH * n_pages
    pool = used + max(1, used // 4)
    perm = jax.random.permutation(next(ks), pool)
    page_table = perm[:used].reshape(DEC_BATCH, n_pages).astype(jnp.int32)

    d = {}
    d['x_prompt'] = nrm((BATCH, SEQ, D_MODEL))
    d['x_sample'] = nrm((DEC_BATCH, DEC_SEQ, D_MODEL))
    d['mem_prompt'] = nrm((BATCH, N_MEM, D_MODEL))
    d['cache_fox_k'] = nrm((DEPTH, pool, PAGE_SIZE, FOX_H, FOX_DH))
    d['cache_fox_v'] = nrm((DEPTH, pool, PAGE_SIZE, FOX_H, FOX_DH))
    d['cache_fox_logf'] = jax.nn.log_sigmoid(3.0 + nrm((DEPTH, pool, PAGE_SIZE, FOX_H), 0.5))
    d['state_ml_C'] = nrm((DEPTH, DEC_BATCH, ML_H, ML_DH, ML_DH), 0.1)
    d['state_ml_n'] = nrm((DEPTH, DEC_BATCH, ML_H, ML_DH), 0.1)
    d['state_ml_m'] = nrm((DEPTH, DEC_BATCH, ML_H))
    d['state_ml_conv'] = nrm((DEPTH, DEC_BATCH, ML_CONV - 1, 2 * ML_W))
    d['state_ffn_conv'] = nrm((DEPTH, DEC_BATCH, FFN_CONV - 1, D_FF))
    d['cache_mem_k'] = nrm((DEPTH, DEC_BATCH, N_MEM, X_H, X_DH))
    d['cache_mem_v'] = nrm((DEPTH, DEC_BATCH, N_MEM, X_H, X_DH))
    d['page_table'] = page_table
    gain = lambda: 1.0 + nrm((DEPTH, D_MODEL), 0.05)
    d['g_pre_mix'] = gain()
    d['w_in'] = nrm((DEPTH, D_MODEL, IN_COLS), D_MODEL ** -0.5)
    d['b_fox_f'] = 3.0 + nrm((DEPTH, FOX_H), 0.5)
    d['w_ml_conv'] = nrm((DEPTH, ML_CONV, 2 * ML_W), ML_CONV ** -0.5)
    d['b_ml_conv'] = nrm((DEPTH, 2 * ML_W), 0.01)
    d['b_ml_i'] = nrm((DEPTH, ML_H), 0.1)
    d['b_ml_f'] = 3.0 + nrm((DEPTH, ML_H), 0.5)
    d['g_ml_head'] = 1.0 + nrm((DEPTH, ML_W), 0.05)
    d['w_out'] = nrm((DEPTH, MIX_W, D_MODEL), MIX_W ** -0.5)
    d['g_post_mix'] = gain()
    d['g_pre_x'] = gain()
    d['g_mem'] = gain()
    d['w_xq'] = nrm((DEPTH, D_MODEL, X_W), D_MODEL ** -0.5)
    d['w_xk'] = nrm((DEPTH, D_MODEL, X_W), D_MODEL ** -0.5)
    d['w_xv'] = nrm((DEPTH, D_MODEL, X_W), D_MODEL ** -0.5)
    d['w_xo'] = nrm((DEPTH, X_W, D_MODEL), X_W ** -0.5)
    d['g_post_x'] = gain()
    d['g_pre_ffn'] = gain()
    d['w_gate'] = nrm((DEPTH, D_MODEL, D_FF), D_MODEL ** -0.5)
    d['w_up'] = nrm((DEPTH, D_MODEL, D_FF), D_MODEL ** -0.5)
    d['w_ffn_conv'] = nrm((DEPTH, FFN_CONV, D_FF), FFN_CONV ** -0.5)
    d['b_ffn_conv'] = nrm((DEPTH, D_FF), 0.01)
    d['w_down'] = nrm((DEPTH, D_FF, D_MODEL), D_FF ** -0.5)
    d['g_post_ffn'] = gain()
    return d


def reference(x_prompt, x_sample, mem_prompt, cache_fox_k, cache_fox_v, cache_fox_logf,
              state_ml_C, state_ml_n, state_ml_m, state_ml_conv, state_ffn_conv,
              cache_mem_k, cache_mem_v, page_table,
              g_pre_mix, w_in, b_fox_f, w_ml_conv, b_ml_conv, b_ml_i, b_ml_f, g_ml_head, w_out,
              g_post_mix, g_pre_x, g_mem, w_xq, w_xk, w_xv, w_xo, g_post_x,
              g_pre_ffn, w_gate, w_up, w_ffn_conv, b_ffn_conv, w_down, g_post_ffn):
    W = dict(g_pre_mix=g_pre_mix, w_in=w_in, b_fox_f=b_fox_f, w_ml_conv=w_ml_conv, b_ml_conv=b_ml_conv,
             b_ml_i=b_ml_i, b_ml_f=b_ml_f, g_ml_head=g_ml_head, w_out=w_out, g_post_mix=g_post_mix,
             g_pre_x=g_pre_x, w_xq=w_xq, w_xo=w_xo, g_post_x=g_post_x, g_pre_ffn=g_pre_ffn,
             w_gate=w_gate, w_up=w_up, w_ffn_conv=w_ffn_conv, b_ffn_conv=b_ffn_conv, w_down=w_down,
             g_post_ffn=g_post_ffn)
    f32 = jnp.float32
    B = x_prompt.shape[0]
    Bd = x_sample.shape[0]
    past = page_table.shape[1] * PAGE_SIZE
    xp, xs = x_prompt, x_sample
    st_p, st_s, mk_p, mv_p = [], [], [], []
    for l in range(DEPTH):
        mn = rmsnorm(mem_prompt, g_mem[l])
        mk = (mn @ w_xk[l]).reshape(B, N_MEM, X_H, X_DH)
        mv = (mn @ w_xv[l]).reshape(B, N_MEM, X_H, X_DH)
        ml0 = (jnp.zeros((B, ML_H, ML_DH, ML_DH), f32), jnp.zeros((B, ML_H, ML_DH), f32),
               jnp.zeros((B, ML_H), f32))
        xp, sp = _layer(xp, l, W, fox_prompt, ml0,
                        jnp.zeros((B, ML_CONV - 1, 2 * ML_W), xp.dtype),
                        jnp.zeros((B, FFN_CONV - 1, D_FF), xp.dtype), mk, mv)
        st_p.append(sp)
        mk_p.append(mk)
        mv_p.append(mv)
        pk = cache_fox_k[l, page_table].reshape(Bd, past, FOX_H, FOX_DH)
        pv = cache_fox_v[l, page_table].reshape(Bd, past, FOX_H, FOX_DH)
        plf = cache_fox_logf[l, page_table].reshape(Bd, past, FOX_H)
        xs, ss = _layer(xs, l, W, make_fox_sample(pk, pv, plf),
                        (state_ml_C[l], state_ml_n[l], state_ml_m[l]),
                        state_ml_conv[l], state_ffn_conv[l], cache_mem_k[l], cache_mem_v[l])
        st_s.append(ss)

    def stk(sts, j):
        return jnp.stack([s[j] for s in sts], axis=0)

    fox_k_p, fox_v_p, fox_lf_p = stk(st_p, 0), stk(st_p, 1), stk(st_p, 2)
    fox_k_s, fox_v_s, fox_lf_s = stk(st_s, 0), stk(st_s, 1), stk(st_s, 2)
    ml_C_p, ml_n_p, ml_m_p, ml_conv_p = stk(st_p, 3), stk(st_p, 4), stk(st_p, 5), stk(st_p, 6)
    ml_C_s, ml_n_s, ml_m_s, ml_conv_s = stk(st_s, 3), stk(st_s, 4), stk(st_s, 5), stk(st_s, 6)
    ffn_conv_p, ffn_conv_s = stk(st_p, 7), stk(st_s, 7)
    mem_k_p = jnp.stack(mk_p, axis=0)
    mem_v_p = jnp.stack(mv_p, axis=0)
    return (xp, xs, fox_k_p, fox_v_p, fox_lf_p, fox_k_s, fox_v_s, fox_lf_s,
            ml_C_p, ml_n_p, ml_m_p, ml_conv_p, ml_C_s, ml_n_s, ml_m_s, ml_conv_s,
            ffn_conv_p, ffn_conv_s, mem_k_p, mem_v_p)
```

```python
import functools

import jax
import jax.numpy as jnp
from jax import lax
from jax.experimental import pallas as pl
from jax.experimental.pallas import tpu as pltpu

FOX_H, FOX_DH = 8, 64
FOX_W = FOX_H * FOX_DH
ML_H, ML_DH = 4, 128
ML_W = ML_H * ML_DH
ML_CONV = 4
ML_CHUNK = 64
X_H, X_DH = 4, 128
X_W = X_H * X_DH
FFN_CONV = 3
EPS = 1e-6
PAGE = 128
NEG = -1e30
GROUP = 8
LANES = 128
VMEM_LIMIT = 56 * 1024 * 1024

F32 = jnp.float32
BF16 = jnp.bfloat16


def _cparams(sem):
    return pltpu.CompilerParams(dimension_semantics=sem, vmem_limit_bytes=VMEM_LIMIT)


def _split3(x):
    hi = x.astype(BF16)
    r1 = x - hi.astype(F32)
    mid = r1.astype(BF16)
    lo = (r1 - mid.astype(F32)).astype(BF16)
    return hi, mid, lo


def _dot(a, b):
    return jnp.dot(a, b, preferred_element_type=F32)


def _dot_nt(a, b):
    return lax.dot_general(a, b, (((1,), (1,)), ((), ())), preferred_element_type=F32)


def _dot_tn(a, b):
    return lax.dot_general(a, b, (((0,), (0,)), ((), ())), preferred_element_type=F32)


def _dot3_lhs(x, w):
    hi, mid, lo = _split3(x)
    return _dot(hi, w) + _dot(mid, w) + _dot(lo, w)


def _dot3_rhs(w, x):
    hi, mid, lo = _split3(x)
    return _dot(w, hi) + _dot(w, mid) + _dot(w, lo)


def _rms(x, g):
    return x * lax.rsqrt(jnp.mean(x * x, axis=-1, keepdims=True) + EPS) * g


def _log_sigmoid(x):
    return jnp.minimum(x, 0.0) - jnp.log1p(jnp.exp(-jnp.abs(x)))


def _sigmoid(x):
    return 1.0 / (1.0 + jnp.exp(-x))


def _gelu_tanh(x):
    c = 0.7978845608028654
    return x * (0.5 * (1.0 + jnp.tanh(c * (x + 0.044715 * (x * x * x)))))


def _norm_proj_kernel(*refs, with_gates):
    if with_gates:
        x_ref, g_ref, w_ref, ws_ref, bs_ref, z_ref, zs_ref, xn_ref = refs
    else:
        x_ref, g_ref, w_ref, z_ref, xn_ref = refs
    j = pl.program_id(1)

    @pl.when(j == 0)
    def _():
        xn = _rms(x_ref[...], g_ref[...]).astype(BF16)
        xn_ref[...] = xn
        if with_gates:
            zs = _dot(xn, ws_ref[...]) + bs_ref[...]
            lane = lax.broadcasted_iota(jnp.int32, zs.shape, 1)
            forget = (lane < FOX_H) | ((lane >= FOX_H + ML_H) & (lane < FOX_H + 2 * ML_H))
            zs_ref[...] = jnp.where(forget, _log_sigmoid(zs), zs)

    z_ref[0] = _dot(xn_ref[...], w_ref[...])


def _norm_proj(x, g, w, tn, ws=None, bs=None, tm=512):
    M, K = x.shape
    N = w.shape[1]
    tm = min(tm, M)
    nj = N // tn
    with_gates = ws is not None
    in_specs = [
        pl.BlockSpec((tm, K), lambda i, j: (i, 0)),
        pl.BlockSpec((1, K), lambda i, j: (0, 0)),
        pl.BlockSpec((K, tn), lambda i, j: (0, j)),
    ]
    args = [x, g.reshape(1, K), w]
    out_shape = [jax.ShapeDtypeStruct((nj, M, tn), F32)]
    out_specs = [pl.BlockSpec((1, tm, tn), lambda i, j: (j, i, 0))]
    if with_gates:
        in_specs += [pl.BlockSpec((K, LANES), lambda i, j: (0, 0)),
                     pl.BlockSpec((1, LANES), lambda i, j: (0, 0))]
        args += [ws, bs]
        out_shape.append(jax.ShapeDtypeStruct((M, LANES), F32))
        out_specs.append(pl.BlockSpec((tm, LANES), lambda i, j: (i, 0)))
    res = pl.pallas_call(
        functools.partial(_norm_proj_kernel, with_gates=with_gates),
        grid=(M // tm, nj),
        in_specs=in_specs,
        out_specs=out_specs,
        out_shape=out_shape,
        scratch_shapes=[pltpu.VMEM((tm, K), BF16)],
        compiler_params=_cparams(("arbitrary", "arbitrary")),
        name="norm_proj_gates" if with_gates else "norm_proj",
    )(*args)
    return res if with_gates else res[0]


def _tri_incl(n, upper):
    r = lax.broadcasted_iota(jnp.int32, (n, n), 0)
    c = lax.broadcasted_iota(jnp.int32, (n, n), 1)
    return ((r <= c) if upper else (c <= r)).astype(BF16)


def _fox_cum_kernel(zs_ref, ft_ref, carry_ref):
    @pl.when(pl.program_id(1) == 0)
    def _():
        carry_ref[...] = jnp.zeros_like(carry_ref)

    lf = zs_ref[...].T[0:FOX_H, :]
    cum = _dot3_lhs(lf, _tri_incl(LANES, True)) + carry_ref[:, 0:1]
    ft_ref[0, 0] = cum
    carry_ref[...] = jnp.broadcast_to(cum[:, LANES - 1:LANES], carry_ref.shape)


def _fox_cum(zs, B, S):
    nb = S // LANES
    return pl.pallas_call(
        _fox_cum_kernel,
        grid=(B, nb),
        in_specs=[pl.BlockSpec((LANES, LANES), lambda b, t: (b * nb + t, 0))],
        out_specs=pl.BlockSpec((1, 1, FOX_H, LANES), lambda b, t: (b, t, 0, 0)),
        out_shape=jax.ShapeDtypeStruct((B, nb, FOX_H, LANES), F32),
        scratch_shapes=[pltpu.VMEM((FOX_H, LANES), F32)],
        compiler_params=_cparams(("arbitrary", "arbitrary")),
        name="fox_cum",
    )(zs)


def _fox_prompt_kernel(q_ref, k_ref, v_ref, ft_ref, o_ref, *, tq):
    i = pl.program_id(1)
    sub = tq // LANES
    low = lax.broadcasted_iota(jnp.int32, (tq, LANES), 1) < FOX_DH
    r2 = lax.broadcasted_iota(jnp.int32, (2 * tq, tq), 0)
    c2 = lax.broadcasted_iota(jnp.int32, (2 * tq, tq), 1)
    causal = c2 <= jnp.where(r2 >= tq, r2 - tq, r2)
    top = r2 < tq

    for p in range(FOX_H // 2):
        cols = slice(p * LANES, (p + 1) * LANES)
        qp = q_ref[0, :, cols] * (FOX_DH ** -0.5)
        zero = jnp.zeros_like(qp)
        q2 = jnp.concatenate([jnp.where(low, qp, zero), jnp.where(low, zero, qp)], axis=0).astype(BF16)

        def block(j, carry, masked):
            m, l, acc = carry
            off = pl.multiple_of(j * tq, tq)
            kb = k_ref[0, pl.ds(off, tq), cols].astype(BF16)
            vb = v_ref[0, pl.ds(off, tq), cols].astype(BF16)
            s = _dot_nt(q2, kb)
            fe = jnp.concatenate([ft_ref[0, j * sub + u, 2 * p:2 * p + 1, :] for u in range(sub)], axis=1)
            fo = jnp.concatenate([ft_ref[0, j * sub + u, 2 * p + 1:2 * p + 2, :] for u in range(sub)], axis=1)
            s = s - jnp.where(top, fe, fo)
            if masked:
                s = jnp.where(causal, s, NEG)
            m_new = jnp.maximum(m, jnp.max(s, axis=1, keepdims=True))
            alpha = jnp.exp(m - m_new)
            pr = jnp.exp(s - m_new)
            l = alpha * l + jnp.sum(pr, axis=1, keepdims=True)
            acc = alpha * acc + _dot(pr.astype(BF16), vb)
            return m_new, l, acc

        init = (jnp.full((2 * tq, 1), NEG, F32), jnp.zeros((2 * tq, 1), F32), jnp.zeros((2 * tq, LANES), F32))
        carry = lax.fori_loop(0, i, lambda j, c: block(j, c, False), init)
        m, l, acc = block(i, carry, True)
        o = acc / l
        o_ref[:, cols] = jnp.where(low, o[:tq], o[tq:]).astype(o_ref.dtype)


def _fox_prompt(z3, ft, B, S, iq, ik, iv, tq=256):
    tq = min(tq, S)
    nq = S // tq
    M = B * S
    return pl.pallas_call(
        functools.partial(_fox_prompt_kernel, tq=tq),
        grid=(B, nq),
        in_specs=[
            pl.BlockSpec((1, tq, FOX_W), lambda b, i: (iq, b * nq + i, 0)),
            pl.BlockSpec((1, S, FOX_W), lambda b, i: (ik, b, 0)),
            pl.BlockSpec((1, S, FOX_W), lambda b, i: (iv, b, 0)),
            pl.BlockSpec((1, S // LANES, FOX_H, LANES), lambda b, i: (b, 0, 0, 0)),
        ],
        out_specs=pl.BlockSpec((tq, FOX_W), lambda b, i: (b * nq + i, 0)),
        out_shape=jax.ShapeDtypeStruct((M, FOX_W), BF16),
        compiler_params=_cparams(("arbitrary", "arbitrary")),
        name="fox_prompt",
    )(z3, z3, z3, ft)


def _fox_sample_kernel(pt_ref, *refs, npg):
    del pt_ref
    q_ref, kn_ref, vn_ref, zs_ref, cum_ref = refs[0:5]
    lf_refs = refs[5:5 + npg]
    k_refs = refs[5 + npg:5 + 2 * npg]
    v_refs = refs[5 + 2 * npg:5 + 3 * npg]
    o_ref = refs[5 + 3 * npg]
    qm_ref, m_ref, l_ref, acc_ref, carry_ref = refs[6 + 3 * npg:]
    step = pl.program_id(1)
    nreal = GROUP // 2
    rows = nreal * FOX_H
    flat = PAGE * FOX_H

    hrow = lax.broadcasted_iota(jnp.int32, (FOX_H, FOX_W), 0)
    hcol = lax.broadcasted_iota(jnp.int32, (FOX_H, FOX_W), 1) // FOX_DH
    head_sel = jnp.concatenate([hrow == hcol] * nreal, axis=0)

    @pl.when(step == 0)
    def _():
        q = q_ref[...] * (FOX_DH ** -0.5)
        qbd = jnp.concatenate(
            [jnp.broadcast_to(q[nreal + t:nreal + t + 1, :], (FOX_H, FOX_W)) for t in range(nreal)], axis=0)
        qbd = jnp.where(head_sel, qbd, 0.0).astype(BF16)
        fr = lax.broadcasted_iota(jnp.int32, (FOX_W, FOX_DH), 0) % FOX_DH
        fc = lax.broadcasted_iota(jnp.int32, (FOX_W, FOX_DH), 1)
        qm_ref[...] = _dot(qbd, (fr == fc).astype(BF16)).astype(BF16)
        m_ref[...] = jnp.full(m_ref.shape, NEG, F32)
        l_ref[...] = jnp.zeros(l_ref.shape, F32)
        acc_ref[...] = jnp.zeros(acc_ref.shape, F32)
        carry_ref[...] = jnp.zeros(carry_ref.shape, F32)

    lf = jnp.concatenate([r[...] for r in lf_refs], axis=0)
    cum = _dot3_lhs(lf, cum_ref[...])

    sel = (lax.broadcasted_iota(jnp.int32, (FOX_H, flat), 0)
           == lax.broadcasted_iota(jnp.int32, (FOX_H, flat), 1) % FOX_H)
    sel = jnp.concatenate([sel] * nreal, axis=0)
    last = lax.broadcasted_iota(jnp.int32, (1, LANES), 1) >= LANES - FOX_H
    qm = qm_ref[...]
    carry = carry_ref[...]
    m, l, acc = m_ref[...], l_ref[...], acc_ref[...]
    for pg in range(npg):
        f = cum[pg:pg + 1, :] + jnp.concatenate([carry] * (flat // LANES), axis=1)
        xk = k_refs[pg][...].astype(BF16)
        xv = v_refs[pg][...].astype(BF16)
        s = jnp.where(sel, _dot_nt(qm, xk) - f, NEG)
        m_new = jnp.maximum(m, jnp.max(s, axis=1, keepdims=True))
        alpha = jnp.exp(m - m_new)
        pr = jnp.exp(s - m_new)
        l = alpha * l + jnp.sum(pr, axis=1, keepdims=True)
        acc = alpha * acc + _dot(pr.astype(BF16), xv)
        m = m_new
        t = jnp.where(last, f[:, flat - LANES:], 0.0)
        for sh in (FOX_H, 2 * FOX_H, 4 * FOX_H, 8 * FOX_H):
            t = t + pltpu.roll(t, sh, axis=1)
        carry = t
    m_ref[...], l_ref[...], acc_ref[...] = m, l, acc
    carry_ref[...] = carry

    @pl.when(step == pl.num_programs(1) - 1)
    def _():
        q = q_ref[...] * (FOX_DH ** -0.5)
        qbd = jnp.concatenate(
            [jnp.broadcast_to(q[nreal + t:nreal + t + 1, :], (FOX_H, FOX_W)) for t in range(nreal)], axis=0)
        qbd = jnp.where(head_sel, qbd, 0.0).astype(BF16)
        zpad = jnp.zeros((LANES - GROUP, FOX_W), F32)
        kn = jnp.concatenate([kn_ref[0], zpad], axis=0).astype(BF16)
        vn = jnp.concatenate([vn_ref[0], zpad], axis=0).astype(BF16)
        s = _dot_nt(qbd, kn)
        zs = jnp.concatenate([zs_ref[...], jnp.zeros((LANES - GROUP, LANES), F32)], axis=0)
        lf_new = zs.T[0:FOX_H, :]
        kr = lax.broadcasted_iota(jnp.int32, (LANES, LANES), 0)
        kc = lax.broadcasted_iota(jnp.int32, (LANES, LANES), 1)
        inc = ((kr <= kc) & (kr >= nreal) & (kr < GROUP)).astype(BF16)
        f_new = _dot3_lhs(lf_new, inc)
        d8 = (lax.broadcasted_iota(jnp.int32, (FOX_H, LANES), 0)
              == lax.broadcasted_iota(jnp.int32, (FOX_H, LANES), 1))
        base = jnp.sum(jnp.where(d8, jnp.broadcast_to(carry, (FOX_H, LANES)), 0.0), axis=1, keepdims=True)
        f_new = jnp.concatenate([f_new + base] * nreal, axis=0)
        qi = lax.broadcasted_iota(jnp.int32, (rows, LANES), 0) // FOX_H
        kj = lax.broadcasted_iota(jnp.int32, (rows, LANES), 1) - nreal
        s = jnp.where((kj >= 0) & (kj <= qi), s - f_new, NEG)
        m_new = jnp.maximum(m, jnp.max(s, axis=1, keepdims=True))
        alpha = jnp.exp(m - m_new)
        pr = jnp.exp(s - m_new)
        lt = alpha * l + jnp.sum(pr, axis=1, keepdims=True)
        pv = jnp.where(head_sel, _dot(pr.astype(BF16), vn), 0.0)
        pv = pv[:, 0:128] + pv[:, 128:256] + pv[:, 256:384] + pv[:, 384:512]
        pv = pv + pltpu.roll(pv, FOX_DH, axis=1)
        out = (alpha * acc + pv[:, 0:FOX_DH]) / lt
        er = lax.broadcasted_iota(jnp.int32, (FOX_DH, FOX_W), 0)
        ec = lax.broadcasted_iota(jnp.int32, (FOX_DH, FOX_W), 1) % FOX_DH
        wide = jnp.where(head_sel, _dot(out.astype(BF16), (er == ec).astype(BF16)), 0.0)
        wide = jnp.concatenate([wide, jnp.zeros((LANES - rows, FOX_W), F32)], axis=0).astype(BF16)
        gr = lax.broadcasted_iota(jnp.int32, (GROUP, LANES), 0) - nreal
        gc = lax.broadcasted_iota(jnp.int32, (GROUP, LANES), 1) // FOX_H
        o_ref[...] = _dot((gr == gc).astype(BF16), wide)


def _fox_sample(z3, zs, cache_k, cache_v, cache_lf, page_table, layer, iq, ik, iv, npg=8):
    Bd, n_pages = page_table.shape
    npg = min(npg, n_pages)
    steps = n_pages // npg
    flat = PAGE * FOX_H
    rows = (GROUP // 2) * FOX_H
    r = lax.broadcasted_iota(jnp.int32, (flat, flat), 0)
    c = lax.broadcasted_iota(jnp.int32, (flat, flat), 1)
    cum_w = ((r % FOX_H == c % FOX_H) & (r <= c)).astype(BF16)

    def page_spec(shape, g):
        return pl.BlockSpec(shape, lambda b, s, pt: (layer, pt[b, s * npg + g]) + (0,) * (len(shape) - 2))

    in_specs = [
        pl.BlockSpec((None, GROUP, FOX_W), lambda b, s, pt: (iq, b, 0)),
        pl.BlockSpec((1, GROUP, FOX_W), lambda b, s, pt: (ik, b, 0)),
        pl.BlockSpec((1, GROUP, FOX_W), lambda b, s, pt: (iv, b, 0)),
        pl.BlockSpec((GROUP, LANES), lambda b, s, pt: (b, 0)),
        pl.BlockSpec((flat, flat), lambda b, s, pt: (0, 0)),
    ]
    in_specs += [page_spec((None, None, 1, flat), g) for g in range(npg)]
    in_specs += [page_spec((None, None, flat, FOX_DH), g) for g in range(npg)]
    in_specs += [page_spec((None, None, flat, FOX_DH), g) for g in range(npg)]
    grid_spec = pltpu.PrefetchScalarGridSpec(
        num_scalar_prefetch=1,
        grid=(Bd, steps),
        in_specs=in_specs,
        out_specs=pl.BlockSpec((GROUP, FOX_W), lambda b, s, pt: (b, 0)),
        scratch_shapes=[
            pltpu.VMEM((rows, FOX_DH), BF16),
            pltpu.VMEM((rows, 1), F32),
            pltpu.VMEM((rows, 1), F32),
            pltpu.VMEM((rows, FOX_DH), F32),
            pltpu.VMEM((1, LANES), F32),
        ],
    )
    return pl.pallas_call(
        functools.partial(_fox_sample_kernel, npg=npg),
        grid_spec=grid_spec,
        out_shape=jax.ShapeDtypeStruct((Bd * GROUP, FOX_W), F32),
        compiler_params=_cparams(("arbitrary", "arbitrary")),
        name="fox_sample",
    )(page_table, z3, z3, z3, zs, cum_w, *([cache_lf] * npg), *([cache_k] * npg), *([cache_v] * npg))


def _mlstm_kernel(*refs, rin, chunk, sample):
    if sample:
        (qk_ref, v_ref, og_ref, zs_ref, inj_ref, wc_ref, bc_ref, gh_ref, c0_ref, nm0_ref,
         h_ref, c_out_ref, nm_out_ref, xbuf, c_s, nm_s) = refs
    else:
        (qk_ref, v_ref, og_ref, zs_ref, wc_ref, bc_ref, gh_ref, c0_ref, nm0_ref,
         h_ref, c_out_ref, nm_out_ref, xbuf, c_s, nm_s) = refs
    t = pl.program_id(1)
    rc = LANES
    nsub = rc // chunk

    @pl.when(t == 0)
    def _():
        xbuf[0:8, :] = jnp.zeros((8, 2 * ML_W), F32)
        c_s[...] = c0_ref[0]
        nm_s[...] = nm0_ref[0]

    x = jnp.concatenate([qk_ref[0], qk_ref[1]], axis=1)
    if sample:
        slot = lax.broadcasted_iota(jnp.int32, (rin, 1), 0)
        x = jnp.where((slot >= 1) & (slot < GROUP // 2), inj_ref[...], x)
    xbuf[8:8 + rin, :] = x
    y = bc_ref[...] + wc_ref[3:4, :] * x
    for jj in range(ML_CONV - 1):
        y = y + wc_ref[jj:jj + 1, :] * xbuf[pl.ds(8 - (ML_CONV - 1) + jj, rin), :]
    xbuf[0:8, :] = xbuf[rin:rin + 8, :]
    y = y * _sigmoid(y)
    q_all = y[:, 0:ML_W]
    k_all = y[:, ML_W:] * (ML_DH ** -0.5)
    v_all = v_ref[0]
    zs = zs_ref[...]
    if rin < rc:
        pad = jnp.zeros((rc - rin, ML_W), F32)
        q_all = jnp.concatenate([q_all, pad], axis=0)
        k_all = jnp.concatenate([k_all, pad], axis=0)
        v_all = jnp.concatenate([v_all, pad], axis=0)
        zs = jnp.concatenate([zs, jnp.zeros((rc - rin, LANES), F32)], axis=0)
    if sample:
        row = lax.broadcasted_iota(jnp.int32, (rc, LANES), 0)
        lane = lax.broadcasted_iota(jnp.int32, (rc, LANES), 1)
        padrow = (row < GROUP // 2) | (row >= GROUP)
        is_i = (lane >= FOX_H) & (lane < FOX_H + ML_H)
        zs = jnp.where(padrow, jnp.where(is_i, NEG, 0.0), zs)

    r = lax.broadcasted_iota(jnp.int32, (rc, rc), 0)
    c = lax.broadcasted_iota(jnp.int32, (rc, rc), 1)
    same = (r // chunk) == (c // chunk)
    lower = (same & (c <= r)).astype(BF16)
    upper = (same & (r <= c)).astype(BF16)
    colcum = _dot3_rhs(lower, zs)
    rowraw = zs.T[FOX_H:FOX_H + 2 * ML_H, :]
    rowcum = _dot3_lhs(rowraw, upper)
    tr = lax.broadcasted_iota(jnp.int32, (chunk, chunk), 0)
    tc = lax.broadcasted_iota(jnp.int32, (chunk, chunk), 1)
    causal = tc <= tr

    outs = []
    for u in range(nsub):
        rs = slice(u * chunk, (u + 1) * chunk)
        heads = []
        for h in range(ML_H):
            cs = slice(h * ML_DH, (h + 1) * ML_DH)
            qf = q_all[rs, cs]
            kf = k_all[rs, cs]
            qh = qf.astype(BF16)
            kh = kf.astype(BF16)
            vh = v_all[rs, cs].astype(BF16)
            icol = zs[rs, FOX_H + h:FOX_H + h + 1]
            bcol = colcum[rs, FOX_H + ML_H + h:FOX_H + ML_H + h + 1]
            irow = rowraw[h:h + 1, rs]
            brow = rowcum[ML_H + h:ML_H + h + 1, rs]
            cmat = c_s[h]
            nrow = nm_s[h, 0:1, :]
            m = nm_s[h, 1:2, 0:1]

            dlog = jnp.where(causal, bcol - brow + irow, NEG)
            g = bcol + m
            mt = jnp.maximum(g, jnp.max(dlog, axis=1, keepdims=True))
            w_intra = jnp.exp(dlog - mt)
            w_inter = jnp.exp(g - mt)
            sc = _dot_nt(qh, kh) * w_intra
            num = _dot(sc.astype(BF16), vh) + _dot(qh, cmat.astype(BF16)) * w_inter
            den = jnp.sum(sc, axis=1, keepdims=True) + w_inter * jnp.sum(qf * nrow, axis=1, keepdims=True)
            hh = num / jnp.maximum(jnp.abs(den), jnp.exp(-mt))

            b_last = bcol[chunk - 1:chunk, :]
            a = b_last - bcol + icol
            m_new = jnp.maximum(b_last + m, jnp.max(a, axis=0, keepdims=True))
            ws = jnp.exp(a - m_new)
            decay = jnp.exp(b_last + m - m_new)
            kw = kf * ws
            c_s[h] = decay * cmat + _dot_tn(kw.astype(BF16), vh)
            nm_s[h, 0:1, :] = decay * nrow + jnp.sum(kw, axis=0, keepdims=True)
            nm_s[h, 1:2, :] = jnp.broadcast_to(m_new, (1, ML_DH))

            hn = _rms(hh, gh_ref[:, cs])
            heads.append(hn)
        outs.append(jnp.concatenate(heads, axis=1))
    hfull = outs[0] if nsub == 1 else jnp.concatenate(outs, axis=0)
    hfull = hfull[0:rin] * _sigmoid(og_ref[0])
    h_ref[...] = hfull.astype(h_ref.dtype)

    @pl.when(t == pl.num_programs(1) - 1)
    def _():
        c_out_ref[0] = c_s[...]
        nm_out_ref[0] = nm_s[...]


def _mlstm(z3, zs, wconv, bconv, ghead, c0, nm0, B, L, iqk, iv, io, inj=None):
    sample = inj is not None
    rin = GROUP if sample else LANES
    chunk = LANES if sample else ML_CHUNK
    nt = L // rin
    M = B * L
    in_specs = [
        pl.BlockSpec((2, rin, ML_W), lambda b, t: (iqk, b * nt + t, 0)),
        pl.BlockSpec((1, rin, ML_W), lambda b, t: (iv, b * nt + t, 0)),
        pl.BlockSpec((1, rin, ML_W), lambda b, t: (io, b * nt + t, 0)),
        pl.BlockSpec((rin, LANES), lambda b, t: (b * nt + t, 0)),
    ]
    args = [z3, z3, z3, zs]
    if sample:
        in_specs.append(pl.BlockSpec((rin, 2 * ML_W), lambda b, t: (b * nt + t, 0)))
        args.append(inj)
    in_specs += [
        pl.BlockSpec((ML_CONV, 2 * ML_W), lambda b, t: (0, 0)),
        pl.BlockSpec((1, 2 * ML_W), lambda b, t: (0, 0)),
        pl.BlockSpec((1, ML_W), lambda b, t: (0, 0)),
        pl.BlockSpec((1, ML_H, ML_DH, ML_DH), lambda b, t: (b, 0, 0, 0)),
        pl.BlockSpec((1, ML_H, 8, ML_DH), lambda b, t: (b, 0, 0, 0)),
    ]
    args += [wconv, bconv.reshape(1, -1), ghead.reshape(1, -1), c0, nm0]
    return pl.pallas_call(
        functools.partial(_mlstm_kernel, rin=rin, chunk=chunk, sample=sample),
        grid=(B, nt),
        in_specs=in_specs,
        out_specs=[
            pl.BlockSpec((rin, ML_W), lambda b, t: (b * nt + t, 0)),
            pl.BlockSpec((1, ML_H, ML_DH, ML_DH), lambda b, t: (b, 0, 0, 0)),
            pl.BlockSpec((1, ML_H, 8, ML_DH), lambda b, t: (b, 0, 0, 0)),
        ],
        out_shape=[
            jax.ShapeDtypeStruct((M, ML_W), F32 if sample else BF16),
            jax.ShapeDtypeStruct((B, ML_H, ML_DH, ML_DH), F32),
            jax.ShapeDtypeStruct((B, ML_H, 8, ML_DH), F32),
        ],
        scratch_shapes=[
            pltpu.VMEM((rin + 8, 2 * ML_W), F32),
            pltpu.VMEM((ML_H, ML_DH, ML_DH), F32),
            pltpu.VMEM((ML_H, 8, ML_DH), F32),
        ],
        compiler_params=_cparams(("arbitrary", "arbitrary")),
        name="mlstm_sample" if sample else "mlstm_prompt",
    )(*args)


def _mix_xattn_kernel(fo_ref, hm_ref, x_ref, wo_ref, gpm_ref, gpx_ref, wq_ref, mk_ref, mv_ref, wxo_ref, gox_ref, o_ref):
    mix = (_dot(fo_ref[...].astype(BF16), wo_ref[0:FOX_W, :])
           + _dot(hm_ref[...].astype(BF16), wo_ref[FOX_W:, :]))
    x1 = x_ref[...] + _rms(mix, gpm_ref[...])
    hq = _rms(x1, gpx_ref[...]).astype(BF16)
    xq = _dot(hq, wq_ref[...]).astype(BF16)
    heads = []
    for h in range(X_H):
        cs = slice(h * X_DH, (h + 1) * X_DH)
        s = _dot_nt(xq[:, cs], mk_ref[0, :, cs].astype(BF16)) * (X_DH ** -0.5)
        e = jnp.exp(s - jnp.max(s, axis=1, keepdims=True))
        o = _dot(e.astype(BF16), mv_ref[0, :, cs].astype(BF16)) / jnp.sum(e, axis=1, keepdims=True)
        heads.append(o.astype(BF16))
    xo = _dot(jnp.concatenate(heads, axis=1), wxo_ref[...])
    o_ref[...] = x1 + _rms(xo, gox_ref[...])


def _mix_xattn(fo, hm, x, wo, gpm, gpx, wq, mk, mv, wxo, gox, rows_per_batch, tm):
    M, D = x.shape
    tpb = rows_per_batch // tm
    n_mem = mk.shape[1]
    vec = lambda: pl.BlockSpec((1, D), lambda i: (0, 0))
    return pl.pallas_call(
        _mix_xattn_kernel,
        grid=(M // tm,),
        in_specs=[
            pl.BlockSpec((tm, FOX_W), lambda i: (i, 0)),
            pl.BlockSpec((tm, ML_W), lambda i: (i, 0)),
            pl.BlockSpec((tm, D), lambda i: (i, 0)),
            pl.BlockSpec((FOX_W + ML_W, D), lambda i: (0, 0)),
            vec(), vec(),
            pl.BlockSpec((D, X_W), lambda i: (0, 0)),
            pl.BlockSpec((1, n_mem, X_W), lambda i: (i // tpb, 0, 0)),
            pl.BlockSpec((1, n_mem, X_W), lambda i: (i // tpb, 0, 0)),
            pl.BlockSpec((X_W, D), lambda i: (0, 0)),
            vec(),
        ],
        out_specs=pl.BlockSpec((tm, D), lambda i: (i, 0)),
        out_shape=jax.ShapeDtypeStruct((M, D), F32),
        compiler_params=_cparams(("arbitrary",)),
        name="mix_xattn",
    )(fo, hm, x, wo, gpm.reshape(1, D), gpx.reshape(1, D), wq, mk, mv, wxo, gox.reshape(1, D))


def _convffn_kernel(*refs, tm, tiles_per_batch, sample):
    if sample:
        (x_ref, gpre_ref, wg_ref, wu_ref, wc_ref, bc_ref, wd_ref, gpost_ref, inj_ref,
         o_ref, gate_ref, xn_ref, acc_ref, gbuf, tails) = refs
    else:
        (x_ref, gpre_ref, wg_ref, wu_ref, wc_ref, bc_ref, wd_ref, gpost_ref,
         o_ref, gate_ref, xn_ref, acc_ref, gbuf, tails) = refs
    i = pl.program_id(0)
    j = pl.program_id(1)

    @pl.when(j == 0)
    def _():
        xn_ref[...] = _rms(x_ref[...], gpre_ref[...]).astype(BF16)
        acc_ref[...] = jnp.zeros_like(acc_ref)

    xn = xn_ref[...]
    g = _dot(xn, wg_ref[...])
    if sample:
        slot = lax.broadcasted_iota(jnp.int32, (tm, 1), 0) % GROUP
        g = jnp.where((slot >= GROUP // 2 - (FFN_CONV - 1)) & (slot < GROUP // 2), inj_ref[...], g)
        gate_ref[...] = g
        gbuf[0:8, :] = jnp.zeros((8, g.shape[1]), F32)
    else:
        first = (i % tiles_per_batch) == 0
        gbuf[0:8, :] = jnp.where(first, 0.0, tails[j])
        gate_ref[0] = g[tm - 8:tm, :]
        tails[j] = g[tm - 8:tm, :]
    gbuf[8:8 + tm, :] = g
    t = bc_ref[...] + wc_ref[2:3, :] * g
    for jj in range(FFN_CONV - 1):
        t = t + wc_ref[jj:jj + 1, :] * gbuf[pl.ds(8 - (FFN_CONV - 1) + jj, tm), :]
    a = _gelu_tanh(t) * _dot(xn, wu_ref[...])
    acc_ref[...] += _dot(a.astype(BF16), wd_ref[...])

    @pl.when(j == pl.num_programs(1) - 1)
    def _():
        o_ref[...] = x_ref[...] + _rms(acc_ref[...], gpost_ref[...])


def _convffn(x, gpre, wg, wu, wc, bc, wd, gpost, rows_per_batch, tm, tn, inj=None):
    M, D = x.shape
    F = wg.shape[1]
    sample = inj is not None
    nj = F // tn
    tiles_per_batch = max(rows_per_batch // tm, 1)
    B = M // rows_per_batch
    vec = lambda: pl.BlockSpec((1, D), lambda i, j: (0, 0))
    in_specs = [
        pl.BlockSpec((tm, D), lambda i, j: (i, 0)),
        vec(),
        pl.BlockSpec((D, tn), lambda i, j: (0, j)),
        pl.BlockSpec((D, tn), lambda i, j: (0, j)),
        pl.BlockSpec((FFN_CONV, tn), lambda i, j: (0, j)),
        pl.BlockSpec((1, tn), lambda i, j: (0, j)),
        pl.BlockSpec((tn, D), lambda i, j: (j, 0)),
        vec(),
    ]
    args = [x, gpre.reshape(1, D), wg, wu, wc, bc.reshape(1, F), wd, gpost.reshape(1, D)]
    if sample:
        in_specs.append(pl.BlockSpec((tm, tn), lambda i, j: (i, j)))
        args.append(inj)
        gate_shape = jax.ShapeDtypeStruct((M, F), F32)
        gate_spec = pl.BlockSpec((tm, tn), lambda i, j: (i, j))
    else:
        gate_shape = jax.ShapeDtypeStruct((B, 8, F), F32)
        gate_spec = pl.BlockSpec((1, 8, tn), lambda i, j: (i // tiles_per_batch, 0, j))
    return pl.pallas_call(
        functools.partial(_convffn_kernel, tm=tm, tiles_per_batch=tiles_per_batch, sample=sample),
        grid=(M // tm, nj),
        in_specs=in_specs,
        out_specs=[pl.BlockSpec((tm, D), lambda i, j: (i, 0)), gate_spec],
        out_shape=[jax.ShapeDtypeStruct((M, D), F32), gate_shape],
        scratch_shapes=[
            pltpu.VMEM((tm, D), BF16),
            pltpu.VMEM((tm, D), F32),
            pltpu.VMEM((tm + 8, tn), F32),
            pltpu.VMEM((nj, 8, tn), F32),
        ],
        compiler_params=_cparams(("arbitrary", "arbitrary")),
        name="convffn_sample" if sample else "convffn_prompt",
    )(*args)


_IQK, _IMV, _IMO, _IFQ, _IFK, _IFV = 0, 2, 3, 4, 5, 6


def _prep_layer_weights(l, w_in, b_fox_f, b_ml_i, b_ml_f, w_out, w_xq, w_xk, w_xv, w_xo, w_gate, w_up, w_down):
    w = w_in[l]
    o_ff = 3 * FOX_W
    o_mq = o_ff + FOX_H
    o_mi = o_mq + 3 * ML_W
    o_mo = o_mi + 2 * ML_H
    wbig = jnp.concatenate([w[:, o_mq:o_mi], w[:, o_mo:], w[:, :o_ff]], axis=1).astype(BF16)
    K = w.shape[0]
    wsmall = jnp.concatenate(
        [w[:, o_ff:o_mq], w[:, o_mi:o_mo], jnp.zeros((K, LANES - FOX_H - 2 * ML_H), F32)], axis=1).astype(BF16)
    bsmall = jnp.concatenate(
        [b_fox_f[l], b_ml_i[l], b_ml_f[l], jnp.zeros((LANES - FOX_H - 2 * ML_H,), F32)]).reshape(1, LANES)
    return dict(
        wbig=wbig, wsmall=wsmall, bsmall=bsmall,
        w_out=w_out[l].astype(BF16), w_xq=w_xq[l].astype(BF16),
        w_xkv=jnp.concatenate([w_xk[l], w_xv[l]], axis=1).astype(BF16),
        w_xo=w_xo[l].astype(BF16), w_gate=w_gate[l].astype(BF16), w_up=w_up[l].astype(BF16),
        w_down=w_down[l].astype(BF16))


def kernel(x_prompt, x_sample, mem_prompt, cache_fox_k, cache_fox_v, cache_fox_logf, state_ml_C, state_ml_n, state_ml_m, state_ml_conv, state_ffn_conv, cache_mem_k, cache_mem_v, page_table, g_pre_mix, w_in, b_fox_f, w_ml_conv, b_ml_conv, b_ml_i, b_ml_f, g_ml_head, w_out, g_post_mix, g_pre_x, g_mem, w_xq, w_xk, w_xv, w_xo, g_post_x, g_pre_ffn, w_gate, w_up, w_ffn_conv, b_ffn_conv, w_down, g_post_ffn):
    B, S, D = x_prompt.shape
    Bd, Ld, _ = x_sample.shape
    depth = w_in.shape[0]
    n_mem = mem_prompt.shape[1]
    d_ff = w_gate.shape[2]
    pool = cache_fox_k.shape[1]
    pad = GROUP - Ld
    assert Ld == GROUP // 2 and S % 256 == 0

    xp = x_prompt.reshape(B * S, D)
    xs = jnp.pad(x_sample, ((0, 0), (pad, 0), (0, 0))).reshape(Bd * GROUP, D)
    memx = mem_prompt.reshape(B * n_mem, D)
    ck = cache_fox_k.reshape(depth, pool, PAGE * FOX_H, FOX_DH)
    cv = cache_fox_v.reshape(depth, pool, PAGE * FOX_H, FOX_DH)
    clf = cache_fox_logf.reshape(depth, pool, 1, PAGE * FOX_H)
    ffn_tn = d_ff // 2 if (d_ff // 2) % LANES == 0 else d_ff

    outs = {k: [] for k in ("fk_p", "fv_p", "lf_p", "fk_s", "fv_s", "lf_s", "C_p", "n_p", "m_p", "cv_p",
                            "C_s", "n_s", "m_s", "cv_s", "fc_p", "fc_s", "mk_p", "mv_p")}
    for l in range(depth):
        W = _prep_layer_weights(l, w_in, b_fox_f, b_ml_i, b_ml_f, w_out, w_xq, w_xk, w_xv, w_xo, w_gate, w_up, w_down)

        mkv = _norm_proj(memx, g_mem[l], W["w_xkv"], X_W)
        mk = mkv[0].reshape(B, n_mem, X_W)
        mv = mkv[1].reshape(B, n_mem, X_W)
        z3, zs = _norm_proj(xp, g_pre_mix[l], W["wbig"], FOX_W, W["wsmall"], W["bsmall"])
        ft = _fox_cum(zs, B, S)
        fo = _fox_prompt(z3, ft, B, S, _IFQ, _IFK, _IFV)
        hm, c_new, nm_new = _mlstm(
            z3, zs, w_ml_conv[l], b_ml_conv[l], g_ml_head[l],
            jnp.zeros((B, ML_H, ML_DH, ML_DH), F32), jnp.zeros((B, ML_H, 8, ML_DH), F32),
            B, S, _IQK, _IMV, _IMO)
        xp = _mix_xattn(fo, hm, xp, W["w_out"], g_post_mix[l], g_pre_x[l], W["w_xq"], mk, mv, W["w_xo"],
                        g_post_x[l], S, min(512, S))
        xp, gate_tail = _convffn(xp, g_pre_ffn[l], W["w_gate"], W["w_up"], w_ffn_conv[l], b_ffn_conv[l],
                                 W["w_down"], g_post_ffn[l], S, min(512, S), ffn_tn)
        outs["fk_p"].append(z3[_IFK].reshape(B, S, FOX_H, FOX_DH))
        outs["fv_p"].append(z3[_IFV].reshape(B, S, FOX_H, FOX_DH))
        outs["lf_p"].append(zs[:, :FOX_H].reshape(B, S, FOX_H))
        outs["C_p"].append(c_new)
        outs["n_p"].append(nm_new[:, :, 0, :])
        outs["m_p"].append(nm_new[:, :, 1, 0])
        qk_pre = jnp.concatenate([z3[0].reshape(B, S, ML_W), z3[1].reshape(B, S, ML_W)], axis=-1)
        outs["cv_p"].append(qk_pre[:, S - (ML_CONV - 1):, :])
        outs["fc_p"].append(gate_tail[:, 8 - (FFN_CONV - 1):, :])
        outs["mk_p"].append(mk.reshape(B, n_mem, X_H, X_DH))
        outs["mv_p"].append(mv.reshape(B, n_mem, X_H, X_DH))

        z3, zs = _norm_proj(xs, g_pre_mix[l], W["wbig"], FOX_W, W["wsmall"], W["bsmall"])
        fo = _fox_sample(z3, zs, ck, cv, clf, page_table, l, _IFQ, _IFK, _IFV)
        inj_ml = jnp.pad(state_ml_conv[l], ((0, 0), (1, GROUP - ML_CONV), (0, 0))).reshape(Bd * GROUP, 2 * ML_W)
        nm0 = jnp.concatenate(
            [state_ml_n[l][:, :, None, :],
             jnp.broadcast_to(state_ml_m[l][:, :, None, None], (Bd, ML_H, 1, ML_DH)),
             jnp.zeros((Bd, ML_H, 6, ML_DH), F32)], axis=2)
        hm, c_new, nm_new = _mlstm(
            z3, zs, w_ml_conv[l], b_ml_conv[l], g_ml_head[l], state_ml_C[l], nm0,
            Bd, GROUP, _IQK, _IMV, _IMO, inj=inj_ml)
        smk = cache_mem_k[l].reshape(Bd, n_mem, X_W)
        smv = cache_mem_v[l].reshape(Bd, n_mem, X_W)
        xs = _mix_xattn(fo, hm, xs, W["w_out"], g_post_mix[l], g_pre_x[l], W["w_xq"], smk, smv, W["w_xo"],
                        g_post_x[l], GROUP, GROUP)
        inj_ffn = jnp.pad(state_ffn_conv[l], ((0, 0), (GROUP // 2 - (FFN_CONV - 1), GROUP // 2), (0, 0)))
        inj_ffn = inj_ffn.reshape(Bd * GROUP, d_ff)
        xs, gate_full = _convffn(xs, g_pre_ffn[l], W["w_gate"], W["w_up"], w_ffn_conv[l], b_ffn_conv[l],
                                 W["w_down"], g_post_ffn[l], Bd * GROUP, Bd * GROUP, ffn_tn, inj=inj_ffn)
        real = lambda a: a.reshape((Bd, GROUP) + a.shape[1:])[:, pad:]
        outs["fk_s"].append(real(z3[_IFK]).reshape(Bd, Ld, FOX_H, FOX_DH))
        outs["fv_s"].append(real(z3[_IFV]).reshape(Bd, Ld, FOX_H, FOX_DH))
        outs["lf_s"].append(real(zs)[:, :, :FOX_H])
        outs["C_s"].append(c_new)
        outs["n_s"].append(nm_new[:, :, 0, :])
        outs["m_s"].append(nm_new[:, :, 1, 0])
        qk_pre = jnp.concatenate([z3[0], z3[1]], axis=-1).reshape(Bd, GROUP, 2 * ML_W)
        outs["cv_s"].append(qk_pre[:, GROUP - (ML_CONV - 1):, :])
        outs["fc_s"].append(gate_full.reshape(Bd, GROUP, d_ff)[:, GROUP - (FFN_CONV - 1):, :])

    st = lambda k: jnp.stack(outs[k], axis=0)
    y_p = xp.reshape(B, S, D)
    y_s = xs.reshape(Bd, GROUP, D)[:, pad:]
    return (y_p, y_s, st("fk_p"), st("fv_p"), st("lf_p"), st("fk_s"), st("fv_s"), st("lf_s"),
            st("C_p"), st("n_p"), st("m_p"), st("cv_p"), st("C_s"), st("n_s"), st("m_s"), st("cv_s"),
            st("fc_p"), st("fc_s"), st("mk_p"), st("mv_p"))
```

```python
import functools

import jax
import jax.numpy as jnp
from jax import lax
from jax.experimental import pallas as pl
from jax.experimental.pallas import tpu as pltpu

FOX_H, FOX_DH = 8, 64
FOX_W = FOX_H * FOX_DH
ML_H, ML_DH = 4, 128
ML_W = ML_H * ML_DH
ML_CONV = 4
ML_CHUNK = 64
X_H, X_DH = 4, 128
X_W = X_H * X_DH
FFN_CONV = 3
EPS = 1e-6
PAGE = 128
NEG = -1e30
GROUP = 8
LANES = 128
VMEM_LIMIT = 56 * 1024 * 1024

F32 = jnp.float32
BF16 = jnp.bfloat16


def _cparams(sem):
    return pltpu.CompilerParams(dimension_semantics=sem, vmem_limit_bytes=VMEM_LIMIT)


def _split3(x):
    hi = x.astype(BF16)
    r1 = x - hi.astype(F32)
    mid = r1.astype(BF16)
    lo = (r1 - mid.astype(F32)).astype(BF16)
    return hi, mid, lo


def _dot(a, b):
    return jnp.dot(a, b, preferred_element_type=F32)


def _dot_nt(a, b):
    return lax.dot_general(a, b, (((1,), (1,)), ((), ())), preferred_element_type=F32)


def _dot_tn(a, b):
    return lax.dot_general(a, b, (((0,), (0,)), ((), ())), preferred_element_type=F32)


def _dot3_lhs(x, w):
    hi, mid, lo = _split3(x)
    return _dot(hi, w) + _dot(mid, w) + _dot(lo, w)


def _dot3_rhs(w, x):
    hi, mid, lo = _split3(x)
    return _dot(w, hi) + _dot(w, mid) + _dot(w, lo)


def _rms(x, g):
    return x * lax.rsqrt(jnp.mean(x * x, axis=-1, keepdims=True) + EPS) * g


def _log_sigmoid(x):
    return jnp.minimum(x, 0.0) - jnp.log1p(jnp.exp(-jnp.abs(x)))


def _sigmoid(x):
    return 1.0 / (1.0 + jnp.exp(-x))


def _gelu_tanh(x):
    c = 0.7978845608028654
    return x * (0.5 * (1.0 + jnp.tanh(c * (x + 0.044715 * (x * x * x)))))


def _norm_proj_kernel(*refs, tn, nz, with_gates, nt):
    refs = list(refs)
    x_ref, g_ref, w_ref = refs[0:3]
    pos = 3
    if with_gates:
        ws_ref, bs_ref = refs[pos:pos + 2]
        pos += 2
    if nt:
        wt_ref = refs[pos]
        pos += 1
    z_ref = refs[pos]
    pos += 1
    if with_gates:
        zs_ref = refs[pos]
        pos += 1
    if nt:
        t_ref = refs[pos]

    xn = _rms(x_ref[...], g_ref[...]).astype(BF16)
    if with_gates:
        zs = _dot(xn, ws_ref[...]) + bs_ref[...]
        lane = lax.broadcasted_iota(jnp.int32, zs.shape, 1)
        forget = (lane < FOX_H) | ((lane >= FOX_H + ML_H) & (lane < FOX_H + 2 * ML_H))
        zs_ref[...] = jnp.where(forget, _log_sigmoid(zs), zs)
    for j in range(nz):
        z_ref[j] = _dot(xn, w_ref[:, j * tn:(j + 1) * tn])
    for j in range(nt):
        t_ref[j, 0] = _dot_nt(wt_ref[j], xn)


def _norm_proj(x, g, w, tn, ws=None, bs=None, wt=None, rows_per_batch=None, tm=512):
    M, K = x.shape
    tm = min(tm, M)
    nz = w.shape[1] // tn
    with_gates = ws is not None
    nt = 0 if wt is None else wt.shape[0]
    in_specs = [
        pl.BlockSpec((tm, K), lambda i: (i, 0)),
        pl.BlockSpec((1, K), lambda i: (0, 0)),
        pl.BlockSpec((K, nz * tn), lambda i: (0, 0)),
    ]
    args = [x, g.reshape(1, K), w]
    out_shape = [jax.ShapeDtypeStruct((nz, M, tn), F32)]
    out_specs = [pl.BlockSpec((nz, tm, tn), lambda i: (0, i, 0))]
    if with_gates:
        in_specs += [pl.BlockSpec((K, LANES), lambda i: (0, 0)),
                     pl.BlockSpec((1, LANES), lambda i: (0, 0))]
        args += [ws, bs]
    if nt:
        in_specs.append(pl.BlockSpec((nt, tn, K), lambda i: (0, 0, 0)))
        args.append(wt)
    if with_gates:
        out_shape.append(jax.ShapeDtypeStruct((M, LANES), F32))
        out_specs.append(pl.BlockSpec((tm, LANES), lambda i: (i, 0)))
    if nt:
        tpb = rows_per_batch // tm
        out_shape.append(jax.ShapeDtypeStruct((nt, M // rows_per_batch, tn, rows_per_batch), F32))
        out_specs.append(pl.BlockSpec((nt, 1, tn, tm), lambda i: (0, i // tpb, 0, i % tpb)))
    res = pl.pallas_call(
        functools.partial(_norm_proj_kernel, tn=tn, nz=nz, with_gates=with_gates, nt=nt),
        grid=(M // tm,),
        in_specs=in_specs,
        out_specs=out_specs,
        out_shape=out_shape,
        compiler_params=_cparams(("arbitrary",)),
        name="norm_proj_gates" if with_gates else "norm_proj",
    )(*args)
    return res if len(res) > 1 else res[0]


def _tri_incl(n, upper):
    r = lax.broadcasted_iota(jnp.int32, (n, n), 0)
    c = lax.broadcasted_iota(jnp.int32, (n, n), 1)
    return ((r <= c) if upper else (c <= r)).astype(BF16)


def _fox_cum_kernel(zs_ref, ft_ref, carry_ref):
    @pl.when(pl.program_id(1) == 0)
    def _():
        carry_ref[...] = jnp.zeros_like(carry_ref)

    lf = zs_ref[...].T[0:FOX_H, :]
    cum = _dot3_lhs(lf, _tri_incl(LANES, True)) + carry_ref[:, 0:1]
    ft_ref[0, 0] = cum
    carry_ref[...] = jnp.broadcast_to(cum[:, LANES - 1:LANES], carry_ref.shape)


def _fox_cum(zs, B, S):
    nb = S // LANES
    return pl.pallas_call(
        _fox_cum_kernel,
        grid=(B, nb),
        in_specs=[pl.BlockSpec((LANES, LANES), lambda b, t: (b * nb + t, 0))],
        out_specs=pl.BlockSpec((1, 1, FOX_H, LANES), lambda b, t: (b, t, 0, 0)),
        out_shape=jax.ShapeDtypeStruct((B, nb, FOX_H, LANES), F32),
        scratch_shapes=[pltpu.VMEM((FOX_H, LANES), F32)],
        compiler_params=_cparams(("arbitrary", "arbitrary")),
        name="fox_cum",
    )(zs)


def _fox_prompt_kernel(ti_ref, tj_ref, q_ref, kt_ref, vt_ref, ft_ref, o_ref, m_s, l_s, acc_s, *, tq):
    t = pl.program_id(1)
    i = ti_ref[t]
    j = tj_ref[t]
    sub = tq // LANES
    npair = FOX_H // 2
    low = lax.broadcasted_iota(jnp.int32, (tq, LANES), 1) < FOX_DH

    @pl.when(j == 0)
    def _():
        m_s[...] = jnp.full(m_s.shape, NEG, F32)
        l_s[...] = jnp.zeros(l_s.shape, F32)
        acc_s[...] = jnp.zeros(acc_s.shape, F32)

    def update(masked):
        r2 = lax.broadcasted_iota(jnp.int32, (2 * tq, tq), 0)
        c2 = lax.broadcasted_iota(jnp.int32, (2 * tq, tq), 1)
        top = r2 < tq
        causal = c2 <= jnp.where(top, r2, r2 - tq)
        for p in range(npair):
            rows = slice(p * LANES, (p + 1) * LANES)
            qp = q_ref[0, :, rows] * (FOX_DH ** -0.5)
            zero = jnp.zeros_like(qp)
            q2 = jnp.concatenate([jnp.where(low, qp, zero), jnp.where(low, zero, qp)], axis=0).astype(BF16)
            kb = kt_ref[rows, :].astype(BF16)
            vb = vt_ref[rows, :].astype(BF16)
            s = _dot(q2, kb)
            fe = jnp.concatenate([ft_ref[0, u, 2 * p:2 * p + 1, :] for u in range(sub)], axis=1)
            fo = jnp.concatenate([ft_ref[0, u, 2 * p + 1:2 * p + 2, :] for u in range(sub)], axis=1)
            s = s - jnp.where(top, fe, fo)
            if masked:
                s = jnp.where(causal, s, NEG)
            m = m_s[p]
            m_new = jnp.maximum(m, jnp.max(s, axis=1, keepdims=True))
            alpha = jnp.exp(m - m_new)
            pr = jnp.exp(s - m_new)
            l_s[p] = alpha * l_s[p] + jnp.sum(pr, axis=1, keepdims=True)
            acc_s[p] = alpha * acc_s[p] + _dot_nt(pr.astype(BF16), vb)
            m_s[p] = m_new

    @pl.when(j < i)
    def _():
        update(False)

    @pl.when(j == i)
    def _():
        update(True)
        for p in range(npair):
            o = acc_s[p] / l_s[p]
            o_ref[:, p * LANES:(p + 1) * LANES] = jnp.where(low, o[:tq], o[tq:]).astype(o_ref.dtype)


def _fox_prompt(z3, kvt, ft, B, S, iq, tq=512):
    tq = min(tq, S)
    nq = S // tq
    M = B * S
    sub = tq // LANES
    pairs = [(i, j) for i in range(nq) for j in range(i + 1)]
    ti = jnp.asarray([p[0] for p in pairs], jnp.int32)
    tj = jnp.asarray([p[1] for p in pairs], jnp.int32)
    grid_spec = pltpu.PrefetchScalarGridSpec(
        num_scalar_prefetch=2,
        grid=(B, len(pairs)),
        in_specs=[
            pl.BlockSpec((1, tq, FOX_W), lambda b, t, ti, tj: (iq, b * nq + ti[t], 0)),
            pl.BlockSpec((None, None, FOX_W, tq), lambda b, t, ti, tj: (0, b, 0, tj[t])),
            pl.BlockSpec((None, None, FOX_W, tq), lambda b, t, ti, tj: (1, b, 0, tj[t])),
            pl.BlockSpec((1, sub, FOX_H, LANES), lambda b, t, ti, tj: (b, tj[t], 0, 0)),
        ],
        out_specs=pl.BlockSpec((tq, FOX_W), lambda b, t, ti, tj: (b * nq + ti[t], 0)),
        scratch_shapes=[
            pltpu.VMEM((FOX_H // 2, 2 * tq, 1), F32),
            pltpu.VMEM((FOX_H // 2, 2 * tq, 1), F32),
            pltpu.VMEM((FOX_H // 2, 2 * tq, LANES), F32),
        ],
    )
    return pl.pallas_call(
        functools.partial(_fox_prompt_kernel, tq=tq),
        grid_spec=grid_spec,
        out_shape=jax.ShapeDtypeStruct((M, FOX_W), BF16),
        compiler_params=_cparams(("arbitrary", "arbitrary")),
        name="fox_prompt",
    )(ti, tj, z3, kvt, kvt, ft)


def _fox_sample_kernel(pt_ref, *refs, npg):
    del pt_ref
    q_ref, kn_ref, vn_ref, zs_ref = refs[0:4]
    lf_refs = refs[4:4 + npg]
    k_refs = refs[4 + npg:4 + 2 * npg]
    v_refs = refs[4 + 2 * npg:4 + 3 * npg]
    o_ref = refs[4 + 3 * npg]
    qbd_ref, m_ref, l_ref, acc_ref, carry_ref = refs[5 + 3 * npg:]
    step = pl.program_id(1)
    nreal = GROUP // 2
    rows = nreal * FOX_H

    hrow = lax.broadcasted_iota(jnp.int32, (FOX_H, FOX_W), 0)
    hcol = lax.broadcasted_iota(jnp.int32, (FOX_H, FOX_W), 1) // FOX_DH
    head_sel = jnp.concatenate([hrow == hcol] * nreal, axis=0)

    @pl.when(step == 0)
    def _():
        q = q_ref[...] * (FOX_DH ** -0.5)
        qbd = jnp.concatenate(
            [jnp.broadcast_to(q[nreal + t:nreal + t + 1, :], (FOX_H, FOX_W)) for t in range(nreal)], axis=0)
        qbd_ref[...] = jnp.where(head_sel, qbd, 0.0).astype(BF16)
        m_ref[...] = jnp.full(m_ref.shape, NEG, F32)
        l_ref[...] = jnp.zeros(l_ref.shape, F32)
        acc_ref[...] = jnp.zeros(acc_ref.shape, F32)
        carry_ref[...] = jnp.zeros(carry_ref.shape, F32)

    lf = jnp.concatenate([r[...] for r in lf_refs], axis=0)
    cum = _dot3_lhs(lf, _tri_incl(LANES, True))
    carry = carry_ref[:, 0:1]
    fs = []
    for pg in range(npg):
        f = cum[pg * FOX_H:(pg + 1) * FOX_H, :] + carry
        carry = f[:, LANES - 1:LANES]
        fs.append(f)
    fcat = jnp.concatenate(fs, axis=1)
    kcat = jnp.concatenate([r[...].astype(BF16) for r in k_refs], axis=1)
    vcat = jnp.concatenate([r[...].astype(BF16) for r in v_refs], axis=1)
    qbd = qbd_ref[...]
    m, l, acc = m_ref[...], l_ref[...], acc_ref[...]
    s = _dot(qbd, kcat) - jnp.concatenate([fcat] * nreal, axis=0)
    m_new = jnp.maximum(m, jnp.max(s, axis=1, keepdims=True))
    alpha = jnp.exp(m - m_new)
    pr = jnp.exp(s - m_new)
    l = alpha * l + jnp.sum(pr, axis=1, keepdims=True)
    acc = alpha * acc + _dot_nt(pr.astype(BF16), vcat)
    m = m_new
    m_ref[...], l_ref[...], acc_ref[...] = m, l, acc
    carry_ref[...] = jnp.broadcast_to(carry, carry_ref.shape)

    @pl.when(step == pl.num_programs(1) - 1)
    def _():
        zpad = jnp.zeros((LANES - GROUP, FOX_W), F32)
        kn = jnp.concatenate([kn_ref[0], zpad], axis=0).astype(BF16)
        vn = jnp.concatenate([vn_ref[0], zpad], axis=0).astype(BF16)
        s = _dot_nt(qbd, kn)
        zs = jnp.concatenate([zs_ref[...], jnp.zeros((LANES - GROUP, LANES), F32)], axis=0)
        lf_new = zs.T[0:FOX_H, :]
        kr = lax.broadcasted_iota(jnp.int32, (LANES, LANES), 0)
        kc = lax.broadcasted_iota(jnp.int32, (LANES, LANES), 1)
        inc = ((kr <= kc) & (kr >= nreal) & (kr < GROUP)).astype(BF16)
        f_new = _dot3_lhs(lf_new, inc) + carry
        f_new = jnp.concatenate([f_new] * nreal, axis=0)
        qi = lax.broadcasted_iota(jnp.int32, (rows, LANES), 0) // FOX_H
        kj = lax.broadcasted_iota(jnp.int32, (rows, LANES), 1) - nreal
        s = jnp.where((kj >= 0) & (kj <= qi), s - f_new, NEG)
        m_new = jnp.maximum(m, jnp.max(s, axis=1, keepdims=True))
        alpha = jnp.exp(m - m_new)
        pr = jnp.exp(s - m_new)
        lt = alpha * l + jnp.sum(pr, axis=1, keepdims=True)
        out = (alpha * acc + _dot(pr.astype(BF16), vn)) / lt
        wide = jnp.where(head_sel, out, 0.0)
        wide = jnp.concatenate([wide, jnp.zeros((LANES - rows, FOX_W), F32)], axis=0).astype(BF16)
        gr = lax.broadcasted_iota(jnp.int32, (GROUP, LANES), 0) - nreal
        gc = lax.broadcasted_iota(jnp.int32, (GROUP, LANES), 1) // FOX_H
        o_ref[...] = _dot((gr == gc).astype(BF16), wide)


def _fox_sample(z3, zs, cache_k, cache_v, cache_lf, page_table, layer, iq, ik, iv, npg=8):
    Bd, n_pages = page_table.shape
    npg = min(npg, n_pages)
    steps = n_pages // npg
    rows = (GROUP // 2) * FOX_H

    def page_spec(shape, g):
        return pl.BlockSpec(shape, lambda b, s, pt: (layer, pt[b, s * npg + g], 0, 0))

    in_specs = [
        pl.BlockSpec((None, GROUP, FOX_W), lambda b, s, pt: (iq, b, 0)),
        pl.BlockSpec((1, GROUP, FOX_W), lambda b, s, pt: (ik, b, 0)),
        pl.BlockSpec((1, GROUP, FOX_W), lambda b, s, pt: (iv, b, 0)),
        pl.BlockSpec((GROUP, LANES), lambda b, s, pt: (b, 0)),
    ]
    in_specs += [page_spec((None, None, FOX_H, PAGE), g) for g in range(npg)]
    in_specs += [page_spec((None, None, FOX_W, PAGE), g) for g in range(npg)]
    in_specs += [page_spec((None, None, FOX_W, PAGE), g) for g in range(npg)]
    grid_spec = pltpu.PrefetchScalarGridSpec(
        num_scalar_prefetch=1,
        grid=(Bd, steps),
        in_specs=in_specs,
        out_specs=pl.BlockSpec((GROUP, FOX_W), lambda b, s, pt: (b, 0)),
        scratch_shapes=[
            pltpu.VMEM((rows, FOX_W), BF16),
            pltpu.VMEM((rows, 1), F32),
            pltpu.VMEM((rows, 1), F32),
            pltpu.VMEM((rows, FOX_W), F32),
            pltpu.VMEM((FOX_H, LANES), F32),
        ],
    )
    return pl.pallas_call(
        functools.partial(_fox_sample_kernel, npg=npg),
        grid_spec=grid_spec,
        out_shape=jax.ShapeDtypeStruct((Bd * GROUP, FOX_W), F32),
        compiler_params=_cparams(("arbitrary", "arbitrary")),
        name="fox_sample",
    )(page_table, z3, z3, z3, zs, *([cache_lf] * npg), *([cache_k] * npg), *([cache_v] * npg))


def _mlstm_kernel(*refs, rin, chunk, sample):
    if sample:
        (qk_ref, v_ref, og_ref, zs_ref, inj_ref, wc_ref, bc_ref, gh_ref, c0_ref, nm0_ref,
         h_ref, c_out_ref, nm_out_ref, xbuf, c_s, nm_s) = refs
    else:
        (qk_ref, v_ref, og_ref, zs_ref, wc_ref, bc_ref, gh_ref, c0_ref, nm0_ref,
         h_ref, c_out_ref, nm_out_ref, xbuf, c_s, nm_s) = refs
    t = pl.program_id(1)
    rc = LANES
    nsub = rc // chunk

    @pl.when(t == 0)
    def _():
        xbuf[0:8, :] = jnp.zeros((8, 2 * ML_W), F32)
        c_s[...] = c0_ref[0]
        nm_s[...] = nm0_ref[0]

    x = jnp.concatenate([qk_ref[0], qk_ref[1]], axis=1)
    if sample:
        slot = lax.broadcasted_iota(jnp.int32, (rin, 1), 0)
        x = jnp.where((slot >= 1) & (slot < GROUP // 2), inj_ref[...], x)
    xbuf[8:8 + rin, :] = x
    y = bc_ref[...] + wc_ref[3:4, :] * x
    for jj in range(ML_CONV - 1):
        y = y + wc_ref[jj:jj + 1, :] * xbuf[pl.ds(8 - (ML_CONV - 1) + jj, rin), :]
    xbuf[0:8, :] = xbuf[rin:rin + 8, :]
    y = y * _sigmoid(y)
    q_all = y[:, 0:ML_W]
    k_all = y[:, ML_W:] * (ML_DH ** -0.5)
    v_all = v_ref[0]
    zs = zs_ref[...]
    if rin < rc:
        pad = jnp.zeros((rc - rin, ML_W), F32)
        q_all = jnp.concatenate([q_all, pad], axis=0)
        k_all = jnp.concatenate([k_all, pad], axis=0)
        v_all = jnp.concatenate([v_all, pad], axis=0)
        zs = jnp.concatenate([zs, jnp.zeros((rc - rin, LANES), F32)], axis=0)
    if sample:
        row = lax.broadcasted_iota(jnp.int32, (rc, LANES), 0)
        lane = lax.broadcasted_iota(jnp.int32, (rc, LANES), 1)
        padrow = (row < GROUP // 2) | (row >= GROUP)
        is_i = (lane >= FOX_H) & (lane < FOX_H + ML_H)
        zs = jnp.where(padrow, jnp.where(is_i, NEG, 0.0), zs)

    r = lax.broadcasted_iota(jnp.int32, (rc, rc), 0)
    c = lax.broadcasted_iota(jnp.int32, (rc, rc), 1)
    same = (r // chunk) == (c // chunk)
    lower = (same & (c <= r)).astype(BF16)
    upper = (same & (r <= c)).astype(BF16)
    colcum = _dot3_rhs(lower, zs)
    rowraw = zs.T[FOX_H:FOX_H + 2 * ML_H, :]
    rowcum = _dot3_lhs(rowraw, upper)
    tr = lax.broadcasted_iota(jnp.int32, (chunk, chunk), 0)
    tc = lax.broadcasted_iota(jnp.int32, (chunk, chunk), 1)
    causal = tc <= tr

    outs = []
    for u in range(nsub):
        rs = slice(u * chunk, (u + 1) * chunk)
        heads = []
        for h in range(ML_H):
            cs = slice(h * ML_DH, (h + 1) * ML_DH)
            qf = q_all[rs, cs]
            kf = k_all[rs, cs]
            qh = qf.astype(BF16)
            kh = kf.astype(BF16)
            vh = v_all[rs, cs].astype(BF16)
            icol = zs[rs, FOX_H + h:FOX_H + h + 1]
            bcol = colcum[rs, FOX_H + ML_H + h:FOX_H + ML_H + h + 1]
            irow = rowraw[h:h + 1, rs]
            brow = rowcum[ML_H + h:ML_H + h + 1, rs]
            cmat = c_s[h]
            nrow = nm_s[h, 0:1, :]
            m = nm_s[h, 1:2, 0:1]

            dlog = jnp.where(causal, bcol - brow + irow, NEG)
            g = bcol + m
            mt = jnp.maximum(g, jnp.max(dlog, axis=1, keepdims=True))
            w_intra = jnp.exp(dlog - mt)
            w_inter = jnp.exp(g - mt)
            sc = _dot_nt(qh, kh) * w_intra
            num = _dot(sc.astype(BF16), vh) + _dot(qh, cmat.astype(BF16)) * w_inter
            den = jnp.sum(sc, axis=1, keepdims=True) + w_inter * jnp.sum(qf * nrow, axis=1, keepdims=True)
            hh = num / jnp.maximum(jnp.abs(den), jnp.exp(-mt))

            b_last = bcol[chunk - 1:chunk, :]
            a = b_last - bcol + icol
            m_new = jnp.maximum(b_last + m, jnp.max(a, axis=0, keepdims=True))
            ws = jnp.exp(a - m_new)
            decay = jnp.exp(b_last + m - m_new)
            kw = kf * ws
            c_s[h] = decay * cmat + _dot_tn(kw.astype(BF16), vh)
            nm_s[h, 0:1, :] = decay * nrow + jnp.sum(kw, axis=0, keepdims=True)
            nm_s[h, 1:2, :] = jnp.broadcast_to(m_new, (1, ML_DH))

            hn = _rms(hh, gh_ref[:, cs])
            heads.append(hn)
        outs.append(jnp.concatenate(heads, axis=1))
    hfull = outs[0] if nsub == 1 else jnp.concatenate(outs, axis=0)
    hfull = hfull[0:rin] * _sigmoid(og_ref[0])
    h_ref[...] = hfull.astype(h_ref.dtype)

    @pl.when(t == pl.num_programs(1) - 1)
    def _():
        c_out_ref[0] = c_s[...]
        nm_out_ref[0] = nm_s[...]


def _mlstm(z3, zs, wconv, bconv, ghead, c0, nm0, B, L, iqk, iv, io, inj=None):
    sample = inj is not None
    rin = GROUP if sample else LANES
    chunk = LANES if sample else ML_CHUNK
    nt = L // rin
    M = B * L
    in_specs = [
        pl.BlockSpec((2, rin, ML_W), lambda b, t: (iqk, b * nt + t, 0)),
        pl.BlockSpec((1, rin, ML_W), lambda b, t: (iv, b * nt + t, 0)),
        pl.BlockSpec((1, rin, ML_W), lambda b, t: (io, b * nt + t, 0)),
        pl.BlockSpec((rin, LANES), lambda b, t: (b * nt + t, 0)),
    ]
    args = [z3, z3, z3, zs]
    if sample:
        in_specs.append(pl.BlockSpec((rin, 2 * ML_W), lambda b, t: (b * nt + t, 0)))
        args.append(inj)
    in_specs += [
        pl.BlockSpec((ML_CONV, 2 * ML_W), lambda b, t: (0, 0)),
        pl.BlockSpec((1, 2 * ML_W), lambda b, t: (0, 0)),
        pl.BlockSpec((1, ML_W), lambda b, t: (0, 0)),
        pl.BlockSpec((1, ML_H, ML_DH, ML_DH), lambda b, t: (b, 0, 0, 0)),
        pl.BlockSpec((1, ML_H, 8, ML_DH), lambda b, t: (b, 0, 0, 0)),
    ]
    args += [wconv, bconv.reshape(1, -1), ghead.reshape(1, -1), c0, nm0]
    return pl.pallas_call(
        functools.partial(_mlstm_kernel, rin=rin, chunk=chunk, sample=sample),
        grid=(B, nt),
        in_specs=in_specs,
        out_specs=[
            pl.BlockSpec((rin, ML_W), lambda b, t: (b * nt + t, 0)),
            pl.BlockSpec((1, ML_H, ML_DH, ML_DH), lambda b, t: (b, 0, 0, 0)),
            pl.BlockSpec((1, ML_H, 8, ML_DH), lambda b, t: (b, 0, 0, 0)),
        ],
        out_shape=[
            jax.ShapeDtypeStruct((M, ML_W), F32 if sample else BF16),
            jax.ShapeDtypeStruct((B, ML_H, ML_DH, ML_DH), F32),
            jax.ShapeDtypeStruct((B, ML_H, 8, ML_DH), F32),
        ],
        scratch_shapes=[
            pltpu.VMEM((rin + 8, 2 * ML_W), F32),
            pltpu.VMEM((ML_H, ML_DH, ML_DH), F32),
            pltpu.VMEM((ML_H, 8, ML_DH), F32),
        ],
        compiler_params=_cparams(("arbitrary", "arbitrary")),
        name="mlstm_sample" if sample else "mlstm_prompt",
    )(*args)


def _mix_xattn_kernel(fo_ref, hm_ref, x_ref, wo_ref, gpm_ref, gpx_ref, wq_ref, mk_ref, mv_ref, wxo_ref, gox_ref, o_ref):
    mix = (_dot(fo_ref[...].astype(BF16), wo_ref[0:FOX_W, :])
           + _dot(hm_ref[...].astype(BF16), wo_ref[FOX_W:, :]))
    x1 = x_ref[...] + _rms(mix, gpm_ref[...])
    hq = _rms(x1, gpx_ref[...]).astype(BF16)
    xq = _dot(hq, wq_ref[...]).astype(BF16)
    heads = []
    for h in range(X_H):
        cs = slice(h * X_DH, (h + 1) * X_DH)
        s = _dot_nt(xq[:, cs], mk_ref[0, :, cs].astype(BF16)) * (X_DH ** -0.5)
        e = jnp.exp(s - jnp.max(s, axis=1, keepdims=True))
        o = _dot(e.astype(BF16), mv_ref[0, :, cs].astype(BF16)) / jnp.sum(e, axis=1, keepdims=True)
        heads.append(o.astype(BF16))
    xo = _dot(jnp.concatenate(heads, axis=1), wxo_ref[...])
    o_ref[...] = x1 + _rms(xo, gox_ref[...])


def _mix_xattn(fo, hm, x, wo, gpm, gpx, wq, mk, mv, wxo, gox, rows_per_batch, tm):
    M, D = x.shape
    tpb = rows_per_batch // tm
    n_mem = mk.shape[1]
    vec = lambda: pl.BlockSpec((1, D), lambda i: (0, 0))
    return pl.pallas_call(
        _mix_xattn_kernel,
        grid=(M // tm,),
        in_specs=[
            pl.BlockSpec((tm, FOX_W), lambda i: (i, 0)),
            pl.BlockSpec((tm, ML_W), lambda i: (i, 0)),
            pl.BlockSpec((tm, D), lambda i: (i, 0)),
            pl.BlockSpec((FOX_W + ML_W, D), lambda i: (0, 0)),
            vec(), vec(),
            pl.BlockSpec((D, X_W), lambda i: (0, 0)),
            pl.BlockSpec((1, n_mem, X_W), lambda i: (i // tpb, 0, 0)),
            pl.BlockSpec((1, n_mem, X_W), lambda i: (i // tpb, 0, 0)),
            pl.BlockSpec((X_W, D), lambda i: (0, 0)),
            vec(),
        ],
        out_specs=pl.BlockSpec((tm, D), lambda i: (i, 0)),
        out_shape=jax.ShapeDtypeStruct((M, D), F32),
        compiler_params=_cparams(("arbitrary",)),
        name="mix_xattn",
    )(fo, hm, x, wo, gpm.reshape(1, D), gpx.reshape(1, D), wq, mk, mv, wxo, gox.reshape(1, D))


def _convffn_kernel(*refs, tm, tiles_per_batch, sample):
    if sample:
        (x_ref, gpre_ref, wg_ref, wu_ref, wc_ref, bc_ref, wd_ref, gpost_ref, inj_ref,
         o_ref, gate_ref, xn_ref, acc_ref, gbuf, tails) = refs
    else:
        (x_ref, gpre_ref, wg_ref, wu_ref, wc_ref, bc_ref, wd_ref, gpost_ref,
         o_ref, gate_ref, xn_ref, acc_ref, gbuf, tails) = refs
    i = pl.program_id(0)
    j = pl.program_id(1)

    @pl.when(j == 0)
    def _():
        xn_ref[...] = _rms(x_ref[...], gpre_ref[...]).astype(BF16)
        acc_ref[...] = jnp.zeros_like(acc_ref)

    xn = xn_ref[...]
    g = _dot(xn, wg_ref[...])
    if sample:
        slot = lax.broadcasted_iota(jnp.int32, (tm, 1), 0) % GROUP
        g = jnp.where((slot >= GROUP // 2 - (FFN_CONV - 1)) & (slot < GROUP // 2), inj_ref[...], g)
        gate_ref[...] = g
        gbuf[0:8, :] = jnp.zeros((8, g.shape[1]), F32)
    else:
        first = (i % tiles_per_batch) == 0
        gbuf[0:8, :] = jnp.where(first, 0.0, tails[j])
        gate_ref[0] = g[tm - 8:tm, :]
        tails[j] = g[tm - 8:tm, :]
    gbuf[8:8 + tm, :] = g
    t = bc_ref[...] + wc_ref[2:3, :] * g
    for jj in range(FFN_CONV - 1):
        t = t + wc_ref[jj:jj + 1, :] * gbuf[pl.ds(8 - (FFN_CONV - 1) + jj, tm), :]
    a = _gelu_tanh(t) * _dot(xn, wu_ref[...])
    acc_ref[...] += _dot(a.astype(BF16), wd_ref[...])

    @pl.when(j == pl.num_programs(1) - 1)
    def _():
        o_ref[...] = x_ref[...] + _rms(acc_ref[...], gpost_ref[...])


def _convffn(x, gpre, wg, wu, wc, bc, wd, gpost, rows_per_batch, tm, tn, inj=None):
    M, D = x.shape
    F = wg.shape[1]
    sample = inj is not None
    nj = F // tn
    tiles_per_batch = max(rows_per_batch // tm, 1)
    vec = lambda: pl.BlockSpec((1, D), lambda i, j: (0, 0))
    in_specs = [
        pl.BlockSpec((tm, D), lambda i, j: (i, 0)),
        vec(),
        pl.BlockSpec((D, tn), lambda i, j: (0, j)),
        pl.BlockSpec((D, tn), lambda i, j: (0, j)),
        pl.BlockSpec((FFN_CONV, tn), lambda i, j: (0, j)),
        pl.BlockSpec((1, tn), lambda i, j: (0, j)),
        pl.BlockSpec((tn, D), lambda i, j: (j, 0)),
        vec(),
    ]
    args = [x, gpre.reshape(1, D), wg, wu, wc, bc.reshape(1, F), wd, gpost.reshape(1, D)]
    if sample:
        in_specs.append(pl.BlockSpec((tm, tn), lambda i, j: (i, j)))
        args.append(inj)
        gate_shape = jax.ShapeDtypeStruct((M, F), F32)
        gate_spec = pl.BlockSpec((tm, tn), lambda i, j: (i, j))
    else:
        gate_shape = jax.ShapeDtypeStruct((M // tm, 8, F), F32)
        gate_spec = pl.BlockSpec((1, 8, tn), lambda i, j: (i, 0, j))
    return pl.pallas_call(
        functools.partial(_convffn_kernel, tm=tm, tiles_per_batch=tiles_per_batch, sample=sample),
        grid=(M // tm, nj),
        in_specs=in_specs,
        out_specs=[pl.BlockSpec((tm, D), lambda i, j: (i, 0)), gate_spec],
        out_shape=[jax.ShapeDtypeStruct((M, D), F32), gate_shape],
        scratch_shapes=[
            pltpu.VMEM((tm, D), BF16),
            pltpu.VMEM((tm, D), F32),
            pltpu.VMEM((tm + 8, tn), F32),
            pltpu.VMEM((nj, 8, tn), F32),
        ],
        compiler_params=_cparams(("arbitrary", "arbitrary")),
        name="convffn_sample" if sample else "convffn_prompt",
    )(*args)


_IQK, _IMV, _IMO, _IFQ, _IFK, _IFV = 0, 2, 3, 4, 5, 6


def _prep_layer_weights(l, w_in, b_fox_f, b_ml_i, b_ml_f, w_out, w_xq, w_xk, w_xv, w_xo, w_gate, w_up, w_down):
    w = w_in[l]
    o_ff = 3 * FOX_W
    o_mq = o_ff + FOX_H
    o_mi = o_mq + 3 * ML_W
    o_mo = o_mi + 2 * ML_H
    wbig = jnp.concatenate([w[:, o_mq:o_mi], w[:, o_mo:], w[:, :o_ff]], axis=1).astype(BF16)
    K = w.shape[0]
    wsmall = jnp.concatenate(
        [w[:, o_ff:o_mq], w[:, o_mi:o_mo], jnp.zeros((K, LANES - FOX_H - 2 * ML_H), F32)], axis=1).astype(BF16)
    bsmall = jnp.concatenate(
        [b_fox_f[l], b_ml_i[l], b_ml_f[l], jnp.zeros((LANES - FOX_H - 2 * ML_H,), F32)]).reshape(1, LANES)
    wkv_t = jnp.transpose(w[:, FOX_W:o_ff].reshape(K, 2, FOX_W), (1, 2, 0)).astype(BF16)
    return dict(
        wbig=wbig, wsmall=wsmall, bsmall=bsmall, wkv_t=wkv_t,
        w_out=w_out[l].astype(BF16), w_xq=w_xq[l].astype(BF16),
        w_xkv=jnp.concatenate([w_xk[l], w_xv[l]], axis=1).astype(BF16),
        w_xo=w_xo[l].astype(BF16), w_gate=w_gate[l].astype(BF16), w_up=w_up[l].astype(BF16),
        w_down=w_down[l].astype(BF16))


def kernel(x_prompt, x_sample, mem_prompt, cache_fox_k, cache_fox_v, cache_fox_logf, state_ml_C, state_ml_n, state_ml_m, state_ml_conv, state_ffn_conv, cache_mem_k, cache_mem_v, page_table, g_pre_mix, w_in, b_fox_f, w_ml_conv, b_ml_conv, b_ml_i, b_ml_f, g_ml_head, w_out, g_post_mix, g_pre_x, g_mem, w_xq, w_xk, w_xv, w_xo, g_post_x, g_pre_ffn, w_gate, w_up, w_ffn_conv, b_ffn_conv, w_down, g_post_ffn):
    B, S, D = x_prompt.shape
    Bd, Ld, _ = x_sample.shape
    depth = w_in.shape[0]
    n_mem = mem_prompt.shape[1]
    d_ff = w_gate.shape[2]
    pool = cache_fox_k.shape[1]
    pad = GROUP - Ld
    assert Ld == GROUP // 2 and S % 256 == 0

    xp = x_prompt.reshape(B * S, D)
    xs = jnp.pad(x_sample, ((0, 0), (pad, 0), (0, 0))).reshape(Bd * GROUP, D)
    memx = mem_prompt.reshape(B * n_mem, D)
    ck = jnp.transpose(cache_fox_k, (0, 1, 3, 4, 2)).reshape(depth, pool, FOX_W, PAGE)
    cv = jnp.transpose(cache_fox_v, (0, 1, 3, 4, 2)).reshape(depth, pool, FOX_W, PAGE)
    clf = jnp.transpose(cache_fox_logf, (0, 1, 3, 2))
    ffn_tn = d_ff // 2 if (d_ff // 2) % LANES == 0 else d_ff

    outs = {k: [] for k in ("fk_p", "fv_p", "lf_p", "fk_s", "fv_s", "lf_s", "C_p", "n_p", "m_p", "cv_p",
                            "C_s", "n_s", "m_s", "cv_s", "fc_p", "fc_s", "mk_p", "mv_p")}
    for l in range(depth):
        W = _prep_layer_weights(l, w_in, b_fox_f, b_ml_i, b_ml_f, w_out, w_xq, w_xk, w_xv, w_xo, w_gate, w_up, w_down)

        mkv = _norm_proj(memx, g_mem[l], W["w_xkv"], X_W)
        mk = mkv[0].reshape(B, n_mem, X_W)
        mv = mkv[1].reshape(B, n_mem, X_W)
        z3, zs, kvt = _norm_proj(xp, g_pre_mix[l], W["wbig"][:, :_IFK * FOX_W], FOX_W, W["wsmall"], W["bsmall"],
                                 wt=W["wkv_t"], rows_per_batch=S)
        ft = _fox_cum(zs, B, S)
        fo = _fox_prompt(z3, kvt, ft, B, S, _IFQ)
        hm, c_new, nm_new = _mlstm(
            z3, zs, w_ml_conv[l], b_ml_conv[l], g_ml_head[l],
            jnp.zeros((B, ML_H, ML_DH, ML_DH), F32), jnp.zeros((B, ML_H, 8, ML_DH), F32),
            B, S, _IQK, _IMV, _IMO)
        xp = _mix_xattn(fo, hm, xp, W["w_out"], g_post_mix[l], g_pre_x[l], W["w_xq"], mk, mv, W["w_xo"],
                        g_post_x[l], S, min(512, S))
        xp, gate_tail = _convffn(xp, g_pre_ffn[l], W["w_gate"], W["w_up"], w_ffn_conv[l], b_ffn_conv[l],
                                 W["w_down"], g_post_ffn[l], S, min(512, S), ffn_tn)
        to_tokens = lambda a: jnp.transpose(a.reshape(B, FOX_H, FOX_DH, S), (0, 3, 1, 2))
        outs["fk_p"].append(to_tokens(kvt[0]))
        outs["fv_p"].append(to_tokens(kvt[1]))
        outs["lf_p"].append(zs[:, :FOX_H].reshape(B, S, FOX_H))
        outs["C_p"].append(c_new)
        outs["n_p"].append(nm_new[:, :, 0, :])
        outs["m_p"].append(nm_new[:, :, 1, 0])
        tail = lambda a: a.reshape(B, S, ML_W)[:, S - (ML_CONV - 1):, :]
        outs["cv_p"].append(jnp.concatenate([tail(z3[0]), tail(z3[1])], axis=-1))
        outs["fc_p"].append(gate_tail.reshape(B, -1, 8, d_ff)[:, -1, 8 - (FFN_CONV - 1):, :])
        outs["mk_p"].append(mk.reshape(B, n_mem, X_H, X_DH))
        outs["mv_p"].append(mv.reshape(B, n_mem, X_H, X_DH))

        z3, zs = _norm_proj(xs, g_pre_mix[l], W["wbig"], FOX_W, W["wsmall"], W["bsmall"])
        fo = _fox_sample(z3, zs, ck, cv, clf, page_table, l, _IFQ, _IFK, _IFV)
        inj_ml = jnp.pad(state_ml_conv[l], ((0, 0), (1, GROUP - ML_CONV), (0, 0))).reshape(Bd * GROUP, 2 * ML_W)
        nm0 = jnp.concatenate(
            [state_ml_n[l][:, :, None, :],
             jnp.broadcast_to(state_ml_m[l][:, :, None, None], (Bd, ML_H, 1, ML_DH)),
             jnp.zeros((Bd, ML_H, 6, ML_DH), F32)], axis=2)
        hm, c_new, nm_new = _mlstm(
            z3, zs, w_ml_conv[l], b_ml_conv[l], g_ml_head[l], state_ml_C[l], nm0,
            Bd, GROUP, _IQK, _IMV, _IMO, inj=inj_ml)
        smk = cache_mem_k[l].reshape(Bd, n_mem, X_W)
        smv = cache_mem_v[l].reshape(Bd, n_mem, X_W)
        xs = _mix_xattn(fo, hm, xs, W["w_out"], g_post_mix[l], g_pre_x[l], W["w_xq"], smk, smv, W["w_xo"],
                        g_post_x[l], GROUP, GROUP)
        inj_ffn = jnp.pad(state_ffn_conv[l], ((0, 0), (GROUP // 2 - (FFN_CONV - 1), GROUP // 2), (0, 0)))
        inj_ffn = inj_ffn.reshape(Bd * GROUP, d_ff)
        xs, gate_full = _convffn(xs, g_pre_ffn[l], W["w_gate"], W["w_up"], w_ffn_conv[l], b_ffn_conv[l],
                                 W["w_down"], g_post_ffn[l], Bd * GROUP, Bd * GROUP, ffn_tn, inj=inj_ffn)
        real = lambda a: a.reshape((Bd, GROUP) + a.shape[1:])[:, pad:]
        outs["fk_s"].append(real(z3[_IFK]).reshape(Bd, Ld, FOX_H, FOX_DH))
        outs["fv_s"].append(real(z3[_IFV]).reshape(Bd, Ld, FOX_H, FOX_DH))
        outs["lf_s"].append(real(zs)[:, :, :FOX_H])
        outs["C_s"].append(c_new)
        outs["n_s"].append(nm_new[:, :, 0, :])
        outs["m_s"].append(nm_new[:, :, 1, 0])
        qk_pre = jnp.concatenate([z3[0], z3[1]], axis=-1).reshape(Bd, GROUP, 2 * ML_W)
        outs["cv_s"].append(qk_pre[:, GROUP - (ML_CONV - 1):, :])
        outs["fc_s"].append(gate_full.reshape(Bd, GROUP, d_ff)[:, GROUP - (FFN_CONV - 1):, :])

    st = lambda k: jnp.stack(outs[k], axis=0)
    y_p = xp.reshape(B, S, D)
    y_s = xs.reshape(Bd, GROUP, D)[:, pad:]
    return (y_p, y_s, st("fk_p"), st("fv_p"), st("lf_p"), st("fk_s"), st("fv_s"), st("lf_s"),
            st("C_p"), st("n_p"), st("m_p"), st("cv_p"), st("C_s"), st("n_s"), st("m_s"), st("cv_s"),
            st("fc_p"), st("fc_s"), st("mk_p"), st("mv_p"))
```

```python
import functools

import jax
import jax.numpy as jnp
from jax import lax
from jax.experimental import pallas as pl
from jax.experimental.pallas import tpu as pltpu

FOX_H, FOX_DH = 8, 64
FOX_W = FOX_H * FOX_DH
ML_H, ML_DH = 4, 128
ML_W = ML_H * ML_DH
ML_CONV = 4
ML_CHUNK = 64
X_H, X_DH = 4, 128
X_W = X_H * X_DH
FFN_CONV = 3
EPS = 1e-6
PAGE = 128
NEG = -1e30
LOG2E = 1.4426950408889634
GROUP = 8
LANES = 128
VMEM_LIMIT = 56 * 1024 * 1024

F32 = jnp.float32
BF16 = jnp.bfloat16


def _cparams(sem):
    return pltpu.CompilerParams(dimension_semantics=sem, vmem_limit_bytes=VMEM_LIMIT)


def _split3(x):
    hi = x.astype(BF16)
    r1 = x - hi.astype(F32)
    mid = r1.astype(BF16)
    lo = (r1 - mid.astype(F32)).astype(BF16)
    return hi, mid, lo


def _dot(a, b):
    return jnp.dot(a, b, preferred_element_type=F32)


def _dot_nt(a, b):
    return lax.dot_general(a, b, (((1,), (1,)), ((), ())), preferred_element_type=F32)


def _dot_tn(a, b):
    return lax.dot_general(a, b, (((0,), (0,)), ((), ())), preferred_element_type=F32)


def _dot3_lhs(x, w):
    hi, mid, lo = _split3(x)
    return _dot(hi, w) + _dot(mid, w) + _dot(lo, w)


def _dot3_rhs(w, x):
    hi, mid, lo = _split3(x)
    return _dot(w, hi) + _dot(w, mid) + _dot(w, lo)


def _rms(x, g):
    return x * lax.rsqrt(jnp.mean(x * x, axis=-1, keepdims=True) + EPS) * g


def _log_sigmoid(x):
    return jnp.minimum(x, 0.0) - jnp.log1p(jnp.exp(-jnp.abs(x)))


def _sigmoid(x):
    return 1.0 / (1.0 + jnp.exp(-x))


def _gelu_tanh(x):
    c = 0.7978845608028654
    return x * (0.5 * (1.0 + jnp.tanh(c * (x + 0.044715 * (x * x * x)))))


def _norm_proj_kernel(*refs, tn, nz, with_gates, nt, n_alias):
    refs = list(refs)
    x_ref, g_ref, w_ref = refs[0:3]
    pos = 3
    if with_gates:
        ws_ref, bs_ref = refs[pos:pos + 2]
        pos += 2
    if nt:
        wt_ref = refs[pos]
        pos += 1
    pos += n_alias
    z_ref = refs[pos]
    pos += 1
    if with_gates:
        zs_ref = refs[pos]
        pos += 1
    t_refs = refs[pos:pos + nt]

    xn = _rms(x_ref[...], g_ref[...]).astype(BF16)
    if with_gates:
        zs = _dot(xn, ws_ref[...]) + bs_ref[...]
        lane = lax.broadcasted_iota(jnp.int32, zs.shape, 1)
        forget = (lane < FOX_H) | ((lane >= FOX_H + ML_H) & (lane < FOX_H + 2 * ML_H))
        zs_ref[...] = jnp.where(forget, _log_sigmoid(zs), zs)
    for j in range(nz):
        z_ref[j] = _dot(xn, w_ref[:, j * tn:(j + 1) * tn])
    for j in range(nt):
        t_refs[j][0] = _dot_nt(wt_ref[j], xn)


def _norm_proj(x, g, w, tn, ws=None, bs=None, wt=None, rows_per_batch=None, t_layer=0, t_depth=1, t_prev=None,
               tm=512):
    M, K = x.shape
    tm = min(tm, M)
    nz = w.shape[1] // tn
    with_gates = ws is not None
    nt = 0 if wt is None else wt.shape[0]
    n_alias = 0 if t_prev is None else nt
    aliases = {}
    in_specs = [
        pl.BlockSpec((tm, K), lambda i: (i, 0)),
        pl.BlockSpec((1, K), lambda i: (0, 0)),
        pl.BlockSpec((K, nz * tn), lambda i: (0, 0)),
    ]
    args = [x, g.reshape(1, K), w]
    out_shape = [jax.ShapeDtypeStruct((nz, M, tn), F32)]
    out_specs = [pl.BlockSpec((nz, tm, tn), lambda i: (0, i, 0))]
    if with_gates:
        in_specs += [pl.BlockSpec((K, LANES), lambda i: (0, 0)),
                     pl.BlockSpec((1, LANES), lambda i: (0, 0))]
        args += [ws, bs]
    if nt:
        in_specs.append(pl.BlockSpec((nt, tn, K), lambda i: (0, 0, 0)))
        args.append(wt)
    if with_gates:
        out_shape.append(jax.ShapeDtypeStruct((M, LANES), F32))
        out_specs.append(pl.BlockSpec((tm, LANES), lambda i: (i, 0)))
    if nt:
        tpb = rows_per_batch // tm
        for j in range(nt):
            if t_prev is not None:
                in_specs.append(pl.BlockSpec(memory_space=pl.ANY))
                args.append(t_prev[j])
                aliases[len(args) - 1] = len(out_shape)
            out_shape.append(jax.ShapeDtypeStruct((t_depth, M // rows_per_batch, tn, rows_per_batch), F32))
            out_specs.append(pl.BlockSpec((None, 1, tn, tm), lambda i: (t_layer, i // tpb, 0, i % tpb)))
    res = pl.pallas_call(
        functools.partial(_norm_proj_kernel, tn=tn, nz=nz, with_gates=with_gates, nt=nt, n_alias=n_alias),
        grid=(M // tm,),
        in_specs=in_specs,
        out_specs=out_specs,
        out_shape=out_shape,
        input_output_aliases=aliases,
        compiler_params=_cparams(("arbitrary",)),
        name="norm_proj_gates" if with_gates else "norm_proj",
    )(*args)
    return res if len(res) > 1 else res[0]


def _tri_incl(n, upper):
    r = lax.broadcasted_iota(jnp.int32, (n, n), 0)
    c = lax.broadcasted_iota(jnp.int32, (n, n), 1)
    return ((r <= c) if upper else (c <= r)).astype(BF16)


BIAS_ROWS = 16


def _fox_cum_kernel(zs_ref, ft_ref, *, nb):
    n = nb * FOX_H
    lf = jnp.concatenate([zs_ref[u * LANES:(u + 1) * LANES, :].T[0:FOX_H, :] for u in range(nb)], axis=0)
    cum = _dot3_lhs(lf, _tri_incl(LANES, True))
    r = lax.broadcasted_iota(jnp.int32, (n, n), 0)
    c = lax.broadcasted_iota(jnp.int32, (n, n), 1)
    earlier = ((c % FOX_H == r % FOX_H) & (c // FOX_H < r // FOX_H)).astype(BF16)
    prefix = _dot3_rhs(earlier, jnp.broadcast_to(cum[:, LANES - 1:LANES], (n, LANES)))
    pieces = _split3((cum + prefix) * (-LOG2E))
    ro = lax.broadcasted_iota(jnp.int32, (nb * (FOX_H // 2) * BIAS_ROWS, n), 0)
    co = lax.broadcasted_iota(jnp.int32, (nb * (FOX_H // 2) * BIAS_ROWS, n), 1)
    blk = ro // ((FOX_H // 2) * BIAS_ROWS) == co // FOX_H
    pair = (ro // BIAS_ROWS) % (FOX_H // 2)
    slot = ro % BIAS_ROWS
    head = co % FOX_H
    out = None
    for idx, piece in enumerate(pieces):
        sel = blk & (((slot == idx) & (head == 2 * pair)) | ((slot == 3 + idx) & (head == 2 * pair + 1)))
        term = _dot(sel.astype(BF16), piece)
        out = term if out is None else out + term
    ft_ref[0] = out.reshape(nb, FOX_H // 2, BIAS_ROWS, LANES)


def _fox_cum(zs, B, S):
    nb = S // LANES
    return pl.pallas_call(
        functools.partial(_fox_cum_kernel, nb=nb),
        grid=(B,),
        in_specs=[pl.BlockSpec((S, LANES), lambda b: (b, 0))],
        out_specs=pl.BlockSpec((1, nb, FOX_H // 2, BIAS_ROWS, LANES), lambda b: (b, 0, 0, 0, 0)),
        out_shape=jax.ShapeDtypeStruct((B, nb, FOX_H // 2, BIAS_ROWS, LANES), F32),
        compiler_params=_cparams(("arbitrary",)),
        name="fox_cum",
    )(zs)


def _fox_prompt_kernel(ti_ref, tj_ref, q_ref, kt_ref, vt_ref, ft_ref, o_ref, m_s, l_s, acc_s, *, tq):
    t = pl.program_id(1)
    i = ti_ref[t]
    j = tj_ref[t]
    sub = tq // LANES
    npair = FOX_H // 2
    low = lax.broadcasted_iota(jnp.int32, (tq, LANES), 1) < FOX_DH

    @pl.when(j == 0)
    def _():
        m_s[...] = jnp.full(m_s.shape, NEG, F32)
        l_s[...] = jnp.zeros(l_s.shape, F32)
        acc_s[...] = jnp.zeros(acc_s.shape, F32)

    def update(masked):
        if masked:
            causal = (lax.broadcasted_iota(jnp.int32, (tq, tq), 1)
                      <= lax.broadcasted_iota(jnp.int32, (tq, tq), 0))
        for p in range(npair):
            rows = slice(p * LANES, (p + 1) * LANES)
            qp = q_ref[0, :, rows] * (FOX_DH ** -0.5 * LOG2E)
            bias = jnp.concatenate([ft_ref[0, u, p] for u in range(sub)], axis=1).astype(BF16)
            ka = jnp.concatenate(
                [kt_ref[rows, :].astype(BF16), bias, jnp.zeros((LANES - BIAS_ROWS, tq), BF16)], axis=0)
            vb = vt_ref[rows, :].astype(BF16)
            for e in range(2):
                h = 2 * p + e
                qh = jnp.where(low if e == 0 else jnp.logical_not(low), qp, 0.0)
                ones = ((bias_lane >= 3 * e) & (bias_lane < 3 * e + 3)).astype(F32)
                qa = jnp.concatenate([qh, ones], axis=1).astype(BF16)
                s = _dot(qa, ka)
                if masked:
                    s = jnp.where(causal, s, NEG)
                sc = [s[:, c * LANES:(c + 1) * LANES] for c in range(sub)]
                mx = functools.reduce(jnp.maximum, sc)
                m_prev = m_s[h]
                m_new = jnp.maximum(m_prev, jnp.max(mx, axis=1, keepdims=True))
                alpha = jnp.exp2(m_prev - m_new)
                ps = [jnp.exp2(c - m_new) for c in sc]
                l_s[h] = alpha * l_s[h] + jnp.sum(functools.reduce(jnp.add, ps), axis=1, keepdims=True)
                pr = jnp.concatenate([c.astype(BF16) for c in ps], axis=1)
                acc_s[h] = alpha * acc_s[h] + _dot_nt(pr, vb)
                m_s[h] = m_new

    bias_lane = lax.broadcasted_iota(jnp.int32, (tq, LANES), 1)

    @pl.when(j < i)
    def _():
        update(False)

    @pl.when(j == i)
    def _():
        update(True)
        for p in range(npair):
            oe = acc_s[2 * p] / l_s[2 * p]
            oo = acc_s[2 * p + 1] / l_s[2 * p + 1]
            o_ref[:, p * LANES:(p + 1) * LANES] = jnp.where(low, oe, oo).astype(o_ref.dtype)


def _fox_prompt(z3, kt, vt, ft, B, S, iq, layer, tq=512):
    tq = min(tq, S)
    nq = S // tq
    M = B * S
    sub = tq // LANES
    pairs = [(i, j) for i in range(nq) for j in range(i + 1)]
    ti = jnp.asarray([p[0] for p in pairs], jnp.int32)
    tj = jnp.asarray([p[1] for p in pairs], jnp.int32)
    grid_spec = pltpu.PrefetchScalarGridSpec(
        num_scalar_prefetch=2,
        grid=(B, len(pairs)),
        in_specs=[
            pl.BlockSpec((1, tq, FOX_W), lambda b, t, ti, tj: (iq, b * nq + ti[t], 0)),
            pl.BlockSpec((None, None, FOX_W, tq), lambda b, t, ti, tj: (layer, b, 0, tj[t])),
            pl.BlockSpec((None, None, FOX_W, tq), lambda b, t, ti, tj: (layer, b, 0, tj[t])),
            pl.BlockSpec((1, sub, FOX_H // 2, BIAS_ROWS, LANES), lambda b, t, ti, tj: (b, tj[t], 0, 0, 0)),
        ],
        out_specs=pl.BlockSpec((tq, FOX_W), lambda b, t, ti, tj: (b * nq + ti[t], 0)),
        scratch_shapes=[
            pltpu.VMEM((FOX_H, tq, LANES), F32),
            pltpu.VMEM((FOX_H, tq, LANES), F32),
            pltpu.VMEM((FOX_H, tq, LANES), F32),
        ],
    )
    return pl.pallas_call(
        functools.partial(_fox_prompt_kernel, tq=tq),
        grid_spec=grid_spec,
        out_shape=jax.ShapeDtypeStruct((M, FOX_W), BF16),
        compiler_params=_cparams(("arbitrary", "arbitrary")),
        name="fox_prompt",
    )(ti, tj, z3, kt, vt, ft)


def _fox_sample_kernel(pt_ref, *refs, npg):
    del pt_ref
    q_ref, kn_ref, vn_ref, zs_ref = refs[0:4]
    lf_refs = refs[4:4 + npg]
    k_refs = refs[4 + npg:4 + 2 * npg]
    v_refs = refs[4 + 2 * npg:4 + 3 * npg]
    o_ref = refs[4 + 3 * npg]
    qbd_ref, m_ref, l_ref, acc_ref, carry_ref = refs[5 + 3 * npg:]
    step = pl.program_id(1)
    nreal = GROUP // 2
    rows = nreal * FOX_H

    hrow = lax.broadcasted_iota(jnp.int32, (FOX_H, FOX_W), 0)
    hcol = lax.broadcasted_iota(jnp.int32, (FOX_H, FOX_W), 1) // FOX_DH
    head_sel = jnp.concatenate([hrow == hcol] * nreal, axis=0)

    @pl.when(step == 0)
    def _():
        q = q_ref[...] * (FOX_DH ** -0.5)
        qbd = jnp.concatenate(
            [jnp.broadcast_to(q[nreal + t:nreal + t + 1, :], (FOX_H, FOX_W)) for t in range(nreal)], axis=0)
        qbd_ref[...] = jnp.where(head_sel, qbd, 0.0).astype(BF16)
        m_ref[...] = jnp.full(m_ref.shape, NEG, F32)
        l_ref[...] = jnp.zeros(l_ref.shape, F32)
        acc_ref[...] = jnp.zeros(acc_ref.shape, F32)
        carry_ref[...] = jnp.zeros(carry_ref.shape, F32)

    lf = jnp.concatenate([r[...] for r in lf_refs], axis=0)
    cum = _dot3_lhs(lf, _tri_incl(LANES, True))
    n = npg * FOX_H
    er = lax.broadcasted_iota(jnp.int32, (n, n), 0)
    ec = lax.broadcasted_iota(jnp.int32, (n, n), 1)
    earlier = ((ec % FOX_H == er % FOX_H) & (ec // FOX_H < er // FOX_H)).astype(BF16)
    prefix = _dot3_rhs(earlier, jnp.broadcast_to(cum[:, LANES - 1:LANES], (n, LANES)))
    f_all = cum + prefix + jnp.concatenate([carry_ref[...]] * npg, axis=0)
    carry = f_all[n - FOX_H:, LANES - 1:LANES]
    fcat = jnp.concatenate([f_all[pg * FOX_H:(pg + 1) * FOX_H, :] for pg in range(npg)], axis=1)
    kcat = jnp.concatenate([r[...].astype(BF16) for r in k_refs], axis=1)
    vcat = jnp.concatenate([r[...].astype(BF16) for r in v_refs], axis=1)
    qbd = qbd_ref[...]
    m, l, acc = m_ref[...], l_ref[...], acc_ref[...]
    eye = (lax.broadcasted_iota(jnp.int32, (rows, rows), 0)
           == lax.broadcasted_iota(jnp.int32, (rows, rows), 1))

    def as_row(col):
        return jnp.sum(jnp.where(eye, jnp.broadcast_to(col, (rows, rows)), 0.0), axis=0, keepdims=True)

    s = _dot(qbd, kcat) - jnp.concatenate([fcat] * nreal, axis=0)
    m_new = jnp.maximum(m, jnp.max(s, axis=1, keepdims=True))
    alpha = jnp.exp(m - m_new)
    pr = jnp.exp(s - m_new)
    l = alpha * l + jnp.sum(pr, axis=1, keepdims=True)
    acc = as_row(alpha) * acc + _dot_nt(vcat, pr.astype(BF16))
    m = m_new
    m_ref[...], l_ref[...], acc_ref[...] = m, l, acc
    carry_ref[...] = jnp.broadcast_to(carry, carry_ref.shape)

    @pl.when(step == pl.num_programs(1) - 1)
    def _():
        zpad = jnp.zeros((LANES - GROUP, FOX_W), F32)
        kn = jnp.concatenate([kn_ref[0], zpad], axis=0).astype(BF16)
        vn_t = jnp.concatenate([vn_ref[0], zpad], axis=0).T.astype(BF16)
        s = _dot_nt(qbd, kn)
        zs = jnp.concatenate([zs_ref[...], jnp.zeros((LANES - GROUP, LANES), F32)], axis=0)
        lf_new = zs.T[0:FOX_H, :]
        kr = lax.broadcasted_iota(jnp.int32, (LANES, LANES), 0)
        kc = lax.broadcasted_iota(jnp.int32, (LANES, LANES), 1)
        inc = ((kr <= kc) & (kr >= nreal) & (kr < GROUP)).astype(BF16)
        f_new = _dot3_lhs(lf_new, inc) + carry
        f_new = jnp.concatenate([f_new] * nreal, axis=0)
        qi = lax.broadcasted_iota(jnp.int32, (rows, LANES), 0) // FOX_H
        kj = lax.broadcasted_iota(jnp.int32, (rows, LANES), 1) - nreal
        s = jnp.where((kj >= 0) & (kj <= qi), s - f_new, NEG)
        m_new = jnp.maximum(m, jnp.max(s, axis=1, keepdims=True))
        alpha = jnp.exp(m - m_new)
        pr = jnp.exp(s - m_new)
        lt = alpha * l + jnp.sum(pr, axis=1, keepdims=True)
        out_t = (as_row(alpha) * acc + _dot_nt(vn_t, pr.astype(BF16))) / as_row(lt)
        own = (lax.broadcasted_iota(jnp.int32, (FOX_W, rows), 0) // FOX_DH
               == lax.broadcasted_iota(jnp.int32, (FOX_W, rows), 1) % FOX_H)
        out_t = jnp.where(own, out_t, 0.0).astype(BF16)
        gr = lax.broadcasted_iota(jnp.int32, (GROUP, rows), 0) - nreal
        gc = lax.broadcasted_iota(jnp.int32, (GROUP, rows), 1) // FOX_H
        o_ref[...] = _dot_nt((gr == gc).astype(BF16), out_t)


def _fox_sample(z3, zs, cache_k, cache_v, cache_lf, page_table, layer, iq, ik, iv, npg=16):
    Bd, n_pages = page_table.shape
    npg = min(npg, n_pages)
    steps = n_pages // npg
    rows = (GROUP // 2) * FOX_H

    def page_spec(shape, g):
        return pl.BlockSpec(shape, lambda b, s, pt: (layer, pt[b, s * npg + g], 0, 0))

    in_specs = [
        pl.BlockSpec((None, GROUP, FOX_W), lambda b, s, pt: (iq, b, 0)),
        pl.BlockSpec((1, GROUP, FOX_W), lambda b, s, pt: (ik, b, 0)),
        pl.BlockSpec((1, GROUP, FOX_W), lambda b, s, pt: (iv, b, 0)),
        pl.BlockSpec((GROUP, LANES), lambda b, s, pt: (b, 0)),
    ]
    in_specs += [page_spec((None, None, FOX_H, PAGE), g) for g in range(npg)]
    in_specs += [page_spec((None, None, FOX_W, PAGE), g) for g in range(npg)]
    in_specs += [page_spec((None, None, FOX_W, PAGE), g) for g in range(npg)]
    grid_spec = pltpu.PrefetchScalarGridSpec(
        num_scalar_prefetch=1,
        grid=(Bd, steps),
        in_specs=in_specs,
        out_specs=pl.BlockSpec((GROUP, FOX_W), lambda b, s, pt: (b, 0)),
        scratch_shapes=[
            pltpu.VMEM((rows, FOX_W), BF16),
            pltpu.VMEM((rows, 1), F32),
            pltpu.VMEM((rows, 1), F32),
            pltpu.VMEM((FOX_W, rows), F32),
            pltpu.VMEM((FOX_H, LANES), F32),
        ],
    )
    return pl.pallas_call(
        functools.partial(_fox_sample_kernel, npg=npg),
        grid_spec=grid_spec,
        out_shape=jax.ShapeDtypeStruct((Bd * GROUP, FOX_W), F32),
        compiler_params=_cparams(("arbitrary", "arbitrary")),
        name="fox_sample",
    )(page_table, z3, z3, z3, zs, *([cache_lf] * npg), *([cache_k] * npg), *([cache_v] * npg))


def _mlstm_kernel(*refs, rin, chunk, sample):
    if sample:
        (qk_ref, v_ref, og_ref, zs_ref, inj_ref, wc_ref, bc_ref, gh_ref, c0_ref, nm0_ref,
         h_ref, c_out_ref, nm_out_ref, xbuf, c_s, nm_s) = refs
    else:
        (qk_ref, v_ref, og_ref, zs_ref, wc_ref, bc_ref, gh_ref, c0_ref, nm0_ref,
         h_ref, c_out_ref, nm_out_ref, xbuf, c_s, nm_s) = refs
    t = pl.program_id(1)
    rc = LANES
    nsub = rc // chunk

    @pl.when(t == 0)
    def _():
        xbuf[0:8, :] = jnp.zeros((8, 2 * ML_W), F32)
        c_s[...] = c0_ref[0]
        nm_s[...] = nm0_ref[0]

    x = jnp.concatenate([qk_ref[0], qk_ref[1]], axis=1)
    if sample:
        slot = lax.broadcasted_iota(jnp.int32, (rin, 1), 0)
        x = jnp.where((slot >= 1) & (slot < GROUP // 2), inj_ref[...], x)
    xbuf[8:8 + rin, :] = x
    y = bc_ref[...] + wc_ref[3:4, :] * x
    for jj in range(ML_CONV - 1):
        y = y + wc_ref[jj:jj + 1, :] * xbuf[pl.ds(8 - (ML_CONV - 1) + jj, rin), :]
    xbuf[0:8, :] = xbuf[rin:rin + 8, :]
    y = y * _sigmoid(y)
    q_all = y[:, 0:ML_W]
    k_all = y[:, ML_W:] * (ML_DH ** -0.5)
    v_all = v_ref[0]
    zs = zs_ref[...]
    if rin < rc:
        pad = jnp.zeros((rc - rin, ML_W), F32)
        q_all = jnp.concatenate([q_all, pad], axis=0)
        k_all = jnp.concatenate([k_all, pad], axis=0)
        v_all = jnp.concatenate([v_all, pad], axis=0)
        zs = jnp.concatenate([zs, jnp.zeros((rc - rin, LANES), F32)], axis=0)
    if sample:
        row = lax.broadcasted_iota(jnp.int32, (rc, LANES), 0)
        lane = lax.broadcasted_iota(jnp.int32, (rc, LANES), 1)
        padrow = (row < GROUP // 2) | (row >= GROUP)
        is_i = (lane >= FOX_H) & (lane < FOX_H + ML_H)
        zs = jnp.where(padrow, jnp.where(is_i, NEG, 0.0), zs)

    r = lax.broadcasted_iota(jnp.int32, (rc, rc), 0)
    c = lax.broadcasted_iota(jnp.int32, (rc, rc), 1)
    same = (r // chunk) == (c // chunk)
    lower = (same & (c <= r)).astype(BF16)
    upper = (same & (r <= c)).astype(BF16)
    colcum = _dot3_rhs(lower, zs)
    rowraw = zs.T[FOX_H:FOX_H + 2 * ML_H, :]
    rowcum = _dot3_lhs(rowraw, upper)
    tr = lax.broadcasted_iota(jnp.int32, (chunk, chunk), 0)
    tc = lax.broadcasted_iota(jnp.int32, (chunk, chunk), 1)
    causal = tc <= tr

    outs = []
    for u in range(nsub):
        rs = slice(u * chunk, (u + 1) * chunk)
        heads = []
        for h in range(ML_H):
            cs = slice(h * ML_DH, (h + 1) * ML_DH)
            qf = q_all[rs, cs]
            kf = k_all[rs, cs]
            qh = qf.astype(BF16)
            kh = kf.astype(BF16)
            vh = v_all[rs, cs].astype(BF16)
            icol = zs[rs, FOX_H + h:FOX_H + h + 1]
            bcol = colcum[rs, FOX_H + ML_H + h:FOX_H + ML_H + h + 1]
            irow = rowraw[h:h + 1, rs]
            brow = rowcum[ML_H + h:ML_H + h + 1, rs]
            cmat = c_s[h]
            nrow = nm_s[h, 0:1, :]
            m = nm_s[h, 1:2, 0:1]

            dlog = jnp.where(causal, bcol - brow + irow, NEG)
            g = bcol + m
            mt = jnp.maximum(g, jnp.max(dlog, axis=1, keepdims=True))
            w_intra = jnp.exp(dlog - mt)
            w_inter = jnp.exp(g - mt)
            sc = _dot_nt(qh, kh) * w_intra
            num = _dot(sc.astype(BF16), vh) + _dot(qh, cmat.astype(BF16)) * w_inter
            den = jnp.sum(sc, axis=1, keepdims=True) + w_inter * jnp.sum(qf * nrow, axis=1, keepdims=True)
            hh = num / jnp.maximum(jnp.abs(den), jnp.exp(-mt))

            b_last = bcol[chunk - 1:chunk, :]
            a = b_last - bcol + icol
            m_new = jnp.maximum(b_last + m, jnp.max(a, axis=0, keepdims=True))
            ws = jnp.exp(a - m_new)
            decay = jnp.exp(b_last + m - m_new)
            kw = kf * ws
            c_s[h] = decay * cmat + _dot_tn(kw.astype(BF16), vh)
            nm_s[h, 0:1, :] = decay * nrow + jnp.sum(kw, axis=0, keepdims=True)
            nm_s[h, 1:2, :] = jnp.broadcast_to(m_new, (1, ML_DH))

            hn = _rms(hh, gh_ref[:, cs])
            heads.append(hn)
        outs.append(jnp.concatenate(heads, axis=1))
    hfull = outs[0] if nsub == 1 else jnp.concatenate(outs, axis=0)
    hfull = hfull[0:rin] * _sigmoid(og_ref[0])
    h_ref[...] = hfull.astype(h_ref.dtype)

    @pl.when(t == pl.num_programs(1) - 1)
    def _():
        c_out_ref[0] = c_s[...]
        nm_out_ref[0] = nm_s[...]


def _mlstm(z3, zs, wconv, bconv, ghead, c0, nm0, B, L, iqk, iv, io, inj=None):
    sample = inj is not None
    rin = GROUP if sample else LANES
    chunk = LANES if sample else ML_CHUNK
    nt = L // rin
    M = B * L
    in_specs = [
        pl.BlockSpec((2, rin, ML_W), lambda b, t: (iqk, b * nt + t, 0)),
        pl.BlockSpec((1, rin, ML_W), lambda b, t: (iv, b * nt + t, 0)),
        pl.BlockSpec((1, rin, ML_W), lambda b, t: (io, b * nt + t, 0)),
        pl.BlockSpec((rin, LANES), lambda b, t: (b * nt + t, 0)),
    ]
    args = [z3, z3, z3, zs]
    if sample:
        in_specs.append(pl.BlockSpec((rin, 2 * ML_W), lambda b, t: (b * nt + t, 0)))
        args.append(inj)
    in_specs += [
        pl.BlockSpec((ML_CONV, 2 * ML_W), lambda b, t: (0, 0)),
        pl.BlockSpec((1, 2 * ML_W), lambda b, t: (0, 0)),
        pl.BlockSpec((1, ML_W), lambda b, t: (0, 0)),
        pl.BlockSpec((1, ML_H, ML_DH, ML_DH), lambda b, t: (b, 0, 0, 0)),
        pl.BlockSpec((1, ML_H, 8, ML_DH), lambda b, t: (b, 0, 0, 0)),
    ]
    args += [wconv, bconv.reshape(1, -1), ghead.reshape(1, -1), c0, nm0]
    return pl.pallas_call(
        functools.partial(_mlstm_kernel, rin=rin, chunk=chunk, sample=sample),
        grid=(B, nt),
        in_specs=in_specs,
        out_specs=[
            pl.BlockSpec((rin, ML_W), lambda b, t: (b * nt + t, 0)),
            pl.BlockSpec((1, ML_H, ML_DH, ML_DH), lambda b, t: (b, 0, 0, 0)),
            pl.BlockSpec((1, ML_H, 8, ML_DH), lambda b, t: (b, 0, 0, 0)),
        ],
        out_shape=[
            jax.ShapeDtypeStruct((M, ML_W), F32 if sample else BF16),
            jax.ShapeDtypeStruct((B, ML_H, ML_DH, ML_DH), F32),
            jax.ShapeDtypeStruct((B, ML_H, 8, ML_DH), F32),
        ],
        scratch_shapes=[
            pltpu.VMEM((rin + 8, 2 * ML_W), F32),
            pltpu.VMEM((ML_H, ML_DH, ML_DH), F32),
            pltpu.VMEM((ML_H, 8, ML_DH), F32),
        ],
        compiler_params=_cparams(("arbitrary", "arbitrary")),
        name="mlstm_sample" if sample else "mlstm_prompt",
    )(*args)


def _mix_xattn_kernel(fo_ref, hm_ref, x_ref, wo_ref, gpm_ref, gpx_ref, wq_ref, mk_ref, mv_ref, wxo_ref, gox_ref, o_ref):
    mix = (_dot(fo_ref[...].astype(BF16), wo_ref[0:FOX_W, :])
           + _dot(hm_ref[...].astype(BF16), wo_ref[FOX_W:, :]))
    x1 = x_ref[...] + _rms(mix, gpm_ref[...])
    hq = _rms(x1, gpx_ref[...]).astype(BF16)
    xq = _dot(hq, wq_ref[...]).astype(BF16)
    heads = []
    for h in range(X_H):
        cs = slice(h * X_DH, (h + 1) * X_DH)
        s = _dot_nt(xq[:, cs], mk_ref[0, :, cs].astype(BF16)) * (X_DH ** -0.5)
        e = jnp.exp(s - jnp.max(s, axis=1, keepdims=True))
        o = _dot(e.astype(BF16), mv_ref[0, :, cs].astype(BF16)) / jnp.sum(e, axis=1, keepdims=True)
        heads.append(o.astype(BF16))
    xo = _dot(jnp.concatenate(heads, axis=1), wxo_ref[...])
    o_ref[...] = x1 + _rms(xo, gox_ref[...])


def _mix_xattn(fo, hm, x, wo, gpm, gpx, wq, mk, mv, wxo, gox, rows_per_batch, tm):
    M, D = x.shape
    tpb = rows_per_batch // tm
    n_mem = mk.shape[1]
    vec = lambda: pl.BlockSpec((1, D), lambda i: (0, 0))
    return pl.pallas_call(
        _mix_xattn_kernel,
        grid=(M // tm,),
        in_specs=[
            pl.BlockSpec((tm, FOX_W), lambda i: (i, 0)),
            pl.BlockSpec((tm, ML_W), lambda i: (i, 0)),
            pl.BlockSpec((tm, D), lambda i: (i, 0)),
            pl.BlockSpec((FOX_W + ML_W, D), lambda i: (0, 0)),
            vec(), vec(),
            pl.BlockSpec((D, X_W), lambda i: (0, 0)),
            pl.BlockSpec((1, n_mem, X_W), lambda i: (i // tpb, 0, 0)),
            pl.BlockSpec((1, n_mem, X_W), lambda i: (i // tpb, 0, 0)),
            pl.BlockSpec((X_W, D), lambda i: (0, 0)),
            vec(),
        ],
        out_specs=pl.BlockSpec((tm, D), lambda i: (i, 0)),
        out_shape=jax.ShapeDtypeStruct((M, D), F32),
        compiler_params=_cparams(("arbitrary",)),
        name="mix_xattn",
    )(fo, hm, x, wo, gpm.reshape(1, D), gpx.reshape(1, D), wq, mk, mv, wxo, gox.reshape(1, D))


def _convffn_kernel(*refs, tm, tiles_per_batch, sample):
    if sample:
        (x_ref, gpre_ref, wg_ref, wu_ref, wc_ref, bc_ref, wd_ref, gpost_ref, inj_ref,
         o_ref, gate_ref, xn_ref, acc_ref, gbuf, tails) = refs
    else:
        (x_ref, gpre_ref, wg_ref, wu_ref, wc_ref, bc_ref, wd_ref, gpost_ref,
         o_ref, gate_ref, xn_ref, acc_ref, gbuf, tails) = refs
    i = pl.program_id(0)
    j = pl.program_id(1)

    @pl.when(j == 0)
    def _():
        xn_ref[...] = _rms(x_ref[...], gpre_ref[...]).astype(BF16)
        acc_ref[...] = jnp.zeros_like(acc_ref)

    xn = xn_ref[...]
    g = _dot(xn, wg_ref[...])
    if sample:
        slot = lax.broadcasted_iota(jnp.int32, (tm, 1), 0) % GROUP
        g = jnp.where((slot >= GROUP // 2 - (FFN_CONV - 1)) & (slot < GROUP // 2), inj_ref[...], g)
        gate_ref[...] = g
        gbuf[0:8, :] = jnp.zeros((8, g.shape[1]), F32)
    else:
        first = (i % tiles_per_batch) == 0
        gbuf[0:8, :] = jnp.where(first, 0.0, tails[j])
        gate_ref[0] = g[tm - 8:tm, :]
        tails[j] = g[tm - 8:tm, :]
    gbuf[8:8 + tm, :] = g
    t = bc_ref[...] + wc_ref[2:3, :] * g
    for jj in range(FFN_CONV - 1):
        t = t + wc_ref[jj:jj + 1, :] * gbuf[pl.ds(8 - (FFN_CONV - 1) + jj, tm), :]
    a = _gelu_tanh(t) * _dot(xn, wu_ref[...])
    acc_ref[...] += _dot(a.astype(BF16), wd_ref[...])

    @pl.when(j == pl.num_programs(1) - 1)
    def _():
        o_ref[...] = x_ref[...] + _rms(acc_ref[...], gpost_ref[...])


def _convffn(x, gpre, wg, wu, wc, bc, wd, gpost, rows_per_batch, tm, tn, inj=None):
    M, D = x.shape
    F = wg.shape[1]
    sample = inj is not None
    nj = F // tn
    tiles_per_batch = max(rows_per_batch // tm, 1)
    vec = lambda: pl.BlockSpec((1, D), lambda i, j: (0, 0))
    in_specs = [
        pl.BlockSpec((tm, D), lambda i, j: (i, 0)),
        vec(),
        pl.BlockSpec((D, tn), lambda i, j: (0, j)),
        pl.BlockSpec((D, tn), lambda i, j: (0, j)),
        pl.BlockSpec((FFN_CONV, tn), lambda i, j: (0, j)),
        pl.BlockSpec((1, tn), lambda i, j: (0, j)),
        pl.BlockSpec((tn, D), lambda i, j: (j, 0)),
        vec(),
    ]
    args = [x, gpre.reshape(1, D), wg, wu, wc, bc.reshape(1, F), wd, gpost.reshape(1, D)]
    if sample:
        in_specs.append(pl.BlockSpec((tm, tn), lambda i, j: (i, j)))
        args.append(inj)
        gate_shape = jax.ShapeDtypeStruct((M, F), F32)
        gate_spec = pl.BlockSpec((tm, tn), lambda i, j: (i, j))
    else:
        gate_shape = jax.ShapeDtypeStruct((M // tm, 8, F), F32)
        gate_spec = pl.BlockSpec((1, 8, tn), lambda i, j: (i, 0, j))
    return pl.pallas_call(
        functools.partial(_convffn_kernel, tm=tm, tiles_per_batch=tiles_per_batch, sample=sample),
        grid=(M // tm, nj),
        in_specs=in_specs,
        out_specs=[pl.BlockSpec((tm, D), lambda i, j: (i, 0)), gate_spec],
        out_shape=[jax.ShapeDtypeStruct((M, D), F32), gate_shape],
        scratch_shapes=[
            pltpu.VMEM((tm, D), BF16),
            pltpu.VMEM((tm, D), F32),
            pltpu.VMEM((tm + 8, tn), F32),
            pltpu.VMEM((nj, 8, tn), F32),
        ],
        compiler_params=_cparams(("arbitrary", "arbitrary")),
        name="convffn_sample" if sample else "convffn_prompt",
    )(*args)


_IQK, _IMV, _IMO, _IFQ, _IFK, _IFV = 0, 2, 3, 4, 5, 6


def _prep_layer_weights(l, w_in, b_fox_f, b_ml_i, b_ml_f, w_out, w_xq, w_xk, w_xv, w_xo, w_gate, w_up, w_down):
    w = w_in[l]
    o_ff = 3 * FOX_W
    o_mq = o_ff + FOX_H
    o_mi = o_mq + 3 * ML_W
    o_mo = o_mi + 2 * ML_H
    wbig = jnp.concatenate([w[:, o_mq:o_mi], w[:, o_mo:], w[:, :o_ff]], axis=1).astype(BF16)
    K = w.shape[0]
    wsmall = jnp.concatenate(
        [w[:, o_ff:o_mq], w[:, o_mi:o_mo], jnp.zeros((K, LANES - FOX_H - 2 * ML_H), F32)], axis=1).astype(BF16)
    bsmall = jnp.concatenate(
        [b_fox_f[l], b_ml_i[l], b_ml_f[l], jnp.zeros((LANES - FOX_H - 2 * ML_H,), F32)]).reshape(1, LANES)
    wkv_t = jnp.transpose(w[:, FOX_W:o_ff].reshape(K, 2, FOX_W), (1, 2, 0)).astype(BF16)
    return dict(
        wbig=wbig, wsmall=wsmall, bsmall=bsmall, wkv_t=wkv_t,
        w_out=w_out[l].astype(BF16), w_xq=w_xq[l].astype(BF16),
        w_xkv=jnp.concatenate([w_xk[l], w_xv[l]], axis=1).astype(BF16),
        w_xo=w_xo[l].astype(BF16), w_gate=w_gate[l].astype(BF16), w_up=w_up[l].astype(BF16),
        w_down=w_down[l].astype(BF16))


def kernel(x_prompt, x_sample, mem_prompt, cache_fox_k, cache_fox_v, cache_fox_logf, state_ml_C, state_ml_n, state_ml_m, state_ml_conv, state_ffn_conv, cache_mem_k, cache_mem_v, page_table, g_pre_mix, w_in, b_fox_f, w_ml_conv, b_ml_conv, b_ml_i, b_ml_f, g_ml_head, w_out, g_post_mix, g_pre_x, g_mem, w_xq, w_xk, w_xv, w_xo, g_post_x, g_pre_ffn, w_gate, w_up, w_ffn_conv, b_ffn_conv, w_down, g_post_ffn):
    B, S, D = x_prompt.shape
    Bd, Ld, _ = x_sample.shape
    depth = w_in.shape[0]
    n_mem = mem_prompt.shape[1]
    d_ff = w_gate.shape[2]
    pool = cache_fox_k.shape[1]
    pad = GROUP - Ld
    assert Ld == GROUP // 2 and S % 256 == 0

    xp = x_prompt.reshape(B * S, D)
    xs = jnp.pad(x_sample, ((0, 0), (pad, 0), (0, 0))).reshape(Bd * GROUP, D)
    memx = mem_prompt.reshape(B * n_mem, D)
    ck = jnp.transpose(cache_fox_k, (0, 1, 3, 4, 2)).reshape(depth, pool, FOX_W, PAGE)
    cv = jnp.transpose(cache_fox_v, (0, 1, 3, 4, 2)).reshape(depth, pool, FOX_W, PAGE)
    clf = jnp.transpose(cache_fox_logf, (0, 1, 3, 2))
    ffn_tn = d_ff // 2 if (d_ff // 2) % LANES == 0 else d_ff

    outs = {k: [] for k in ("fk_p", "fv_p", "lf_p", "fk_s", "fv_s", "lf_s", "C_p", "n_p", "m_p", "cv_p",
                            "C_s", "n_s", "m_s", "cv_s", "fc_p", "fc_s", "mk_p", "mv_p")}
    for l in range(depth):
        W = _prep_layer_weights(l, w_in, b_fox_f, b_ml_i, b_ml_f, w_out, w_xq, w_xk, w_xv, w_xo, w_gate, w_up, w_down)

        mkv = _norm_proj(memx, g_mem[l], W["w_xkv"], X_W)
        mk = mkv[0].reshape(B, n_mem, X_W)
        mv = mkv[1].reshape(B, n_mem, X_W)
        z3, zs, kt_all, vt_all = _norm_proj(
            xp, g_pre_mix[l], W["wbig"][:, :_IFK * FOX_W], FOX_W, W["wsmall"], W["bsmall"], wt=W["wkv_t"],
            rows_per_batch=S, t_layer=l, t_depth=depth, t_prev=None if l == 0 else (kt_all, vt_all))
        ft = _fox_cum(zs, B, S)
        fo = _fox_prompt(z3, kt_all, vt_all, ft, B, S, _IFQ, l)
        hm, c_new, nm_new = _mlstm(
            z3, zs, w_ml_conv[l], b_ml_conv[l], g_ml_head[l],
            jnp.zeros((B, ML_H, ML_DH, ML_DH), F32), jnp.zeros((B, ML_H, 8, ML_DH), F32),
            B, S, _IQK, _IMV, _IMO)
        xp = _mix_xattn(fo, hm, xp, W["w_out"], g_post_mix[l], g_pre_x[l], W["w_xq"], mk, mv, W["w_xo"],
                        g_post_x[l], S, min(512, S))
        xp, gate_tail = _convffn(xp, g_pre_ffn[l], W["w_gate"], W["w_up"], w_ffn_conv[l], b_ffn_conv[l],
                                 W["w_down"], g_post_ffn[l], S, min(512, S), ffn_tn)
        outs["lf_p"].append(zs[:, :FOX_H].reshape(B, S, FOX_H))
        outs["C_p"].append(c_new)
        outs["n_p"].append(nm_new[:, :, 0, :])
        outs["m_p"].append(nm_new[:, :, 1, 0])
        tail = lambda a: a.reshape(B, S, ML_W)[:, S - (ML_CONV - 1):, :]
        outs["cv_p"].append(jnp.concatenate([tail(z3[0]), tail(z3[1])], axis=-1))
        outs["fc_p"].append(gate_tail.reshape(B, -1, 8, d_ff)[:, -1, 8 - (FFN_CONV - 1):, :])
        outs["mk_p"].append(mk.reshape(B, n_mem, X_H, X_DH))
        outs["mv_p"].append(mv.reshape(B, n_mem, X_H, X_DH))

        z3, zs = _norm_proj(xs, g_pre_mix[l], W["wbig"], FOX_W, W["wsmall"], W["bsmall"])
        fo = _fox_sample(z3, zs, ck, cv, clf, page_table, l, _IFQ, _IFK, _IFV)
        inj_ml = jnp.pad(state_ml_conv[l], ((0, 0), (1, GROUP - ML_CONV), (0, 0))).reshape(Bd * GROUP, 2 * ML_W)
        nm0 = jnp.concatenate(
            [state_ml_n[l][:, :, None, :],
             jnp.broadcast_to(state_ml_m[l][:, :, None, None], (Bd, ML_H, 1, ML_DH)),
             jnp.zeros((Bd, ML_H, 6, ML_DH), F32)], axis=2)
        hm, c_new, nm_new = _mlstm(
            z3, zs, w_ml_conv[l], b_ml_conv[l], g_ml_head[l], state_ml_C[l], nm0,
            Bd, GROUP, _IQK, _IMV, _IMO, inj=inj_ml)
        smk = cache_mem_k[l].reshape(Bd, n_mem, X_W)
        smv = cache_mem_v[l].reshape(Bd, n_mem, X_W)
        xs = _mix_xattn(fo, hm, xs, W["w_out"], g_post_mix[l], g_pre_x[l], W["w_xq"], smk, smv, W["w_xo"],
                        g_post_x[l], GROUP, GROUP)
        inj_ffn = jnp.pad(state_ffn_conv[l], ((0, 0), (GROUP // 2 - (FFN_CONV - 1), GROUP // 2), (0, 0)))
        inj_ffn = inj_ffn.reshape(Bd * GROUP, d_ff)
        xs, gate_full = _convffn(xs, g_pre_ffn[l], W["w_gate"], W["w_up"], w_ffn_conv[l], b_ffn_conv[l],
                                 W["w_down"], g_post_ffn[l], Bd * GROUP, Bd * GROUP, ffn_tn, inj=inj_ffn)
        real = lambda a: a.reshape((Bd, GROUP) + a.shape[1:])[:, pad:]
        outs["fk_s"].append(real(z3[_IFK]).reshape(Bd, Ld, FOX_H, FOX_DH))
        outs["fv_s"].append(real(z3[_IFV]).reshape(Bd, Ld, FOX_H, FOX_DH))
        outs["lf_s"].append(real(zs)[:, :, :FOX_H])
        outs["C_s"].append(c_new)
        outs["n_s"].append(nm_new[:, :, 0, :])
        outs["m_s"].append(nm_new[:, :, 1, 0])
        qk_pre = jnp.concatenate([z3[0], z3[1]], axis=-1).reshape(Bd, GROUP, 2 * ML_W)
        outs["cv_s"].append(qk_pre[:, GROUP - (ML_CONV - 1):, :])
        outs["fc_s"].append(gate_full.reshape(Bd, GROUP, d_ff)[:, GROUP - (FFN_CONV - 1):, :])

    st = lambda k: jnp.stack(outs[k], axis=0)
    y_p = xp.reshape(B, S, D)
    y_s = xs.reshape(Bd, GROUP, D)[:, pad:]
    to_tokens = lambda a: jnp.transpose(a.reshape(depth, B, FOX_H, FOX_DH, S), (0, 1, 4, 2, 3))
    return (y_p, y_s, to_tokens(kt_all), to_tokens(vt_all), st("lf_p"), st("fk_s"), st("fv_s"), st("lf_s"),
            st("C_p"), st("n_p"), st("m_p"), st("cv_p"), st("C_s"), st("n_s"), st("m_s"), st("cv_s"),
            st("fc_p"), st("fc_s"), st("mk_p"), st("mv_p"))
```

```python
import functools

import jax
import jax.numpy as jnp
from jax import lax
from jax.experimental import pallas as pl
from jax.experimental.pallas import tpu as pltpu

FOX_H, FOX_DH = 8, 64
FOX_W = FOX_H * FOX_DH
ML_H, ML_DH = 4, 128
ML_W = ML_H * ML_DH
ML_CONV = 4
ML_CHUNK = 64
X_H, X_DH = 4, 128
X_W = X_H * X_DH
FFN_CONV = 3
EPS = 1e-6
PAGE = 128
NEG = -1e30
LOG2E = 1.4426950408889634
GROUP = 8
LANES = 128
VMEM_LIMIT = 56 * 1024 * 1024

F32 = jnp.float32
BF16 = jnp.bfloat16


def _cparams(sem):
    return pltpu.CompilerParams(dimension_semantics=sem, vmem_limit_bytes=VMEM_LIMIT)


def _split3(x):
    hi = x.astype(BF16)
    r1 = x - hi.astype(F32)
    mid = r1.astype(BF16)
    lo = (r1 - mid.astype(F32)).astype(BF16)
    return hi, mid, lo


def _dot(a, b):
    return jnp.dot(a, b, preferred_element_type=F32)


def _dot_nt(a, b):
    return lax.dot_general(a, b, (((1,), (1,)), ((), ())), preferred_element_type=F32)


def _dot_tn(a, b):
    return lax.dot_general(a, b, (((0,), (0,)), ((), ())), preferred_element_type=F32)


def _dot3_lhs(x, w):
    hi, mid, lo = _split3(x)
    return _dot(hi, w) + _dot(mid, w) + _dot(lo, w)


def _dot3_rhs(w, x):
    hi, mid, lo = _split3(x)
    return _dot(w, hi) + _dot(w, mid) + _dot(w, lo)


def _rms(x, g):
    return x * lax.rsqrt(jnp.mean(x * x, axis=-1, keepdims=True) + EPS) * g


def _log_sigmoid(x):
    return jnp.minimum(x, 0.0) - jnp.log1p(jnp.exp(-jnp.abs(x)))


def _sigmoid(x):
    return 1.0 / (1.0 + jnp.exp(-x))


def _gelu_tanh(x):
    c = 0.7978845608028654
    return x * (0.5 * (1.0 + jnp.tanh(c * (x + 0.044715 * (x * x * x)))))


def _norm_proj_kernel(*refs, tn, nz, with_gates, nt, n_alias):
    refs = list(refs)
    x_ref, g_ref, w_ref = refs[0:3]
    pos = 3
    if with_gates:
        ws_ref, bs_ref = refs[pos:pos + 2]
        pos += 2
    if nt:
        wt_ref = refs[pos]
        pos += 1
    pos += n_alias
    z_ref = refs[pos]
    pos += 1
    if with_gates:
        zs_ref = refs[pos]
        pos += 1
    t_refs = refs[pos:pos + nt]

    xn = _rms(x_ref[...], g_ref[...]).astype(BF16)
    if with_gates:
        zs = _dot(xn, ws_ref[...]) + bs_ref[...]
        lane = lax.broadcasted_iota(jnp.int32, zs.shape, 1)
        forget = (lane < FOX_H) | ((lane >= FOX_H + ML_H) & (lane < FOX_H + 2 * ML_H))
        zs_ref[...] = jnp.where(forget, _log_sigmoid(zs), zs)
    for j in range(nz):
        z_ref[j] = _dot(xn, w_ref[:, j * tn:(j + 1) * tn])
    for j in range(nt):
        t_refs[j][0] = _dot_nt(wt_ref[j], xn)


def _layer_spec(l, *shape):
    return pl.BlockSpec((None,) + shape, lambda *_: (l,) + (0,) * len(shape))


def _norm_proj(x, l, P, nz, tn, rows_per_batch=None, t_prev=None, tm=512):
    M, K = x.shape
    tm = min(tm, M)
    depth = P["wbig"].shape[0]
    nt = 0 if rows_per_batch is None else P["wkv_t"].shape[1]
    n_alias = 0 if t_prev is None else nt
    aliases = {}
    in_specs = [
        pl.BlockSpec((tm, K), lambda i: (i, 0)),
        _layer_spec(l, 1, K),
        _layer_spec(l, K, nz * tn),
        _layer_spec(l, K, LANES),
        _layer_spec(l, 1, LANES),
    ]
    args = [x, P["g_pre_mix"], P["wbig"], P["wsmall"], P["bsmall"]]
    if nt:
        in_specs.append(_layer_spec(l, nt, tn, K))
        args.append(P["wkv_t"])
    out_shape = [jax.ShapeDtypeStruct((nz, M, tn), F32), jax.ShapeDtypeStruct((M, LANES), F32)]
    out_specs = [pl.BlockSpec((nz, tm, tn), lambda i: (0, i, 0)), pl.BlockSpec((tm, LANES), lambda i: (i, 0))]
    if nt:
        tpb = rows_per_batch // tm
        for j in range(nt):
            if t_prev is not None:
                in_specs.append(pl.BlockSpec(memory_space=pl.ANY))
                args.append(t_prev[j])
                aliases[len(args) - 1] = len(out_shape)
            out_shape.append(jax.ShapeDtypeStruct((depth, M // rows_per_batch, tn, rows_per_batch), F32))
            out_specs.append(pl.BlockSpec((None, 1, tn, tm), lambda i: (l, i // tpb, 0, i % tpb)))
    return pl.pallas_call(
        functools.partial(_norm_proj_kernel, tn=tn, nz=nz, with_gates=True, nt=nt, n_alias=n_alias),
        grid=(M // tm,),
        in_specs=in_specs,
        out_specs=out_specs,
        out_shape=out_shape,
        input_output_aliases=aliases,
        compiler_params=_cparams(("arbitrary",)),
        name="norm_proj",
    )(*args)


def _mem_proj_kernel(x_ref, g_ref, w_ref, *refs):
    k_ref, v_ref = refs[-2:]
    xn = _rms(x_ref[...], g_ref[...]).astype(BF16)
    n = x_ref.shape[0]
    for j, o_ref in enumerate((k_ref, v_ref)):
        for h in range(X_H):
            col = j * X_W + h * X_DH
            o_ref[0, pl.ds(h, n, stride=X_H), :] = _dot(xn, w_ref[:, col:col + X_DH])


def _mem_proj(x, l, P, B, prev=None):
    M, K = x.shape
    n_mem = M // B
    depth = P["w_xkv"].shape[0]
    in_specs = [pl.BlockSpec((n_mem, K), lambda b: (b, 0)), _layer_spec(l, 1, K), _layer_spec(l, K, 2 * X_W)]
    args = [x, P["g_mem"], P["w_xkv"]]
    aliases = {}
    if prev is not None:
        for j in range(2):
            in_specs.append(pl.BlockSpec(memory_space=pl.ANY))
            args.append(prev[j])
            aliases[len(args) - 1] = j
    shape = jax.ShapeDtypeStruct((depth, B, n_mem * X_H, X_DH), F32)
    spec = pl.BlockSpec((None, 1, n_mem * X_H, X_DH), lambda b: (l, b, 0, 0))
    return pl.pallas_call(
        _mem_proj_kernel,
        grid=(B,),
        in_specs=in_specs,
        out_specs=[spec, spec],
        out_shape=[shape, shape],
        input_output_aliases=aliases,
        compiler_params=_cparams(("arbitrary",)),
        name="mem_proj",
    )(*args)


def _tri_incl(n, upper):
    r = lax.broadcasted_iota(jnp.int32, (n, n), 0)
    c = lax.broadcasted_iota(jnp.int32, (n, n), 1)
    return ((r <= c) if upper else (c <= r)).astype(BF16)


BIAS_ROWS = 16


def _fox_cum_kernel(zs_ref, ft_ref, *, nb):
    n = nb * FOX_H
    lf = jnp.concatenate([zs_ref[u * LANES:(u + 1) * LANES, :].T[0:FOX_H, :] for u in range(nb)], axis=0)
    cum = _dot3_lhs(lf, _tri_incl(LANES, True))
    r = lax.broadcasted_iota(jnp.int32, (n, n), 0)
    c = lax.broadcasted_iota(jnp.int32, (n, n), 1)
    earlier = ((c % FOX_H == r % FOX_H) & (c // FOX_H < r // FOX_H)).astype(BF16)
    prefix = _dot3_rhs(earlier, jnp.broadcast_to(cum[:, LANES - 1:LANES], (n, LANES)))
    pieces = _split3((cum + prefix) * (-LOG2E))
    ro = lax.broadcasted_iota(jnp.int32, (nb * (FOX_H // 2) * BIAS_ROWS, n), 0)
    co = lax.broadcasted_iota(jnp.int32, (nb * (FOX_H // 2) * BIAS_ROWS, n), 1)
    blk = ro // ((FOX_H // 2) * BIAS_ROWS) == co // FOX_H
    pair = (ro // BIAS_ROWS) % (FOX_H // 2)
    slot = ro % BIAS_ROWS
    head = co % FOX_H
    out = None
    for idx, piece in enumerate(pieces):
        sel = blk & (((slot == idx) & (head == 2 * pair)) | ((slot == 3 + idx) & (head == 2 * pair + 1)))
        term = _dot(sel.astype(BF16), piece)
        out = term if out is None else out + term
    ft_ref[0] = out.reshape(nb, FOX_H // 2, BIAS_ROWS, LANES)


def _fox_cum(zs, B, S):
    nb = S // LANES
    return pl.pallas_call(
        functools.partial(_fox_cum_kernel, nb=nb),
        grid=(B,),
        in_specs=[pl.BlockSpec((S, LANES), lambda b: (b, 0))],
        out_specs=pl.BlockSpec((1, nb, FOX_H // 2, BIAS_ROWS, LANES), lambda b: (b, 0, 0, 0, 0)),
        out_shape=jax.ShapeDtypeStruct((B, nb, FOX_H // 2, BIAS_ROWS, LANES), F32),
        compiler_params=_cparams(("arbitrary",)),
        name="fox_cum",
    )(zs)


def _fox_prompt_kernel(ti_ref, tj_ref, q_ref, kt_ref, vt_ref, ft_ref, o_ref, m_s, l_s, acc_s, *, tq):
    t = pl.program_id(1)
    i = ti_ref[t]
    j = tj_ref[t]
    sub = tq // LANES
    npair = FOX_H // 2
    low = lax.broadcasted_iota(jnp.int32, (tq, LANES), 1) < FOX_DH

    @pl.when(j == 0)
    def _():
        m_s[...] = jnp.full(m_s.shape, NEG, F32)
        l_s[...] = jnp.zeros(l_s.shape, F32)
        acc_s[...] = jnp.zeros(acc_s.shape, F32)

    def update(masked):
        if masked:
            causal = (lax.broadcasted_iota(jnp.int32, (tq, tq), 1)
                      <= lax.broadcasted_iota(jnp.int32, (tq, tq), 0))
        for p in range(npair):
            rows = slice(p * LANES, (p + 1) * LANES)
            qp = q_ref[0, :, rows] * (FOX_DH ** -0.5 * LOG2E)
            bias = jnp.concatenate([ft_ref[0, u, p] for u in range(sub)], axis=1).astype(BF16)
            ka = jnp.concatenate(
                [kt_ref[rows, :].astype(BF16), bias, jnp.zeros((LANES - BIAS_ROWS, tq), BF16)], axis=0)
            vb = vt_ref[rows, :].astype(BF16)
            for e in range(2):
                h = 2 * p + e
                qh = jnp.where(low if e == 0 else jnp.logical_not(low), qp, 0.0)
                ones = ((bias_lane >= 3 * e) & (bias_lane < 3 * e + 3)).astype(F32)
                qa = jnp.concatenate([qh, ones], axis=1).astype(BF16)
                s = _dot(qa, ka)
                if masked:
                    s = jnp.where(causal, s, NEG)
                sc = [s[:, c * LANES:(c + 1) * LANES] for c in range(sub)]
                mx = functools.reduce(jnp.maximum, sc)
                m_prev = m_s[h]
                m_new = jnp.maximum(m_prev, jnp.max(mx, axis=1, keepdims=True))
                alpha = jnp.exp2(m_prev - m_new)
                ps = [jnp.exp2(c - m_new) for c in sc]
                l_s[h] = alpha * l_s[h] + jnp.sum(functools.reduce(jnp.add, ps), axis=1, keepdims=True)
                pr = jnp.concatenate([c.astype(BF16) for c in ps], axis=1)
                acc_s[h] = alpha * acc_s[h] + _dot_nt(pr, vb)
                m_s[h] = m_new

    bias_lane = lax.broadcasted_iota(jnp.int32, (tq, LANES), 1)

    @pl.when(j < i)
    def _():
        update(False)

    @pl.when(j == i)
    def _():
        update(True)
        for p in range(npair):
            oe = acc_s[2 * p] / l_s[2 * p]
            oo = acc_s[2 * p + 1] / l_s[2 * p + 1]
            o_ref[:, p * LANES:(p + 1) * LANES] = jnp.where(low, oe, oo).astype(o_ref.dtype)


def _fox_prompt(z3, kt, vt, ft, B, S, iq, layer, tq=512):
    tq = min(tq, S)
    nq = S // tq
    M = B * S
    sub = tq // LANES
    pairs = [(i, j) for i in range(nq) for j in range(i + 1)]
    ti = jnp.asarray([p[0] for p in pairs], jnp.int32)
    tj = jnp.asarray([p[1] for p in pairs], jnp.int32)
    grid_spec = pltpu.PrefetchScalarGridSpec(
        num_scalar_prefetch=2,
        grid=(B, len(pairs)),
        in_specs=[
            pl.BlockSpec((1, tq, FOX_W), lambda b, t, ti, tj: (iq, b * nq + ti[t], 0)),
            pl.BlockSpec((None, None, FOX_W, tq), lambda b, t, ti, tj: (layer, b, 0, tj[t])),
            pl.BlockSpec((None, None, FOX_W, tq), lambda b, t, ti, tj: (layer, b, 0, tj[t])),
            pl.BlockSpec((1, sub, FOX_H // 2, BIAS_ROWS, LANES), lambda b, t, ti, tj: (b, tj[t], 0, 0, 0)),
        ],
        out_specs=pl.BlockSpec((tq, FOX_W), lambda b, t, ti, tj: (b * nq + ti[t], 0)),
        scratch_shapes=[
            pltpu.VMEM((FOX_H, tq, LANES), F32),
            pltpu.VMEM((FOX_H, tq, LANES), F32),
            pltpu.VMEM((FOX_H, tq, LANES), F32),
        ],
    )
    return pl.pallas_call(
        functools.partial(_fox_prompt_kernel, tq=tq),
        grid_spec=grid_spec,
        out_shape=jax.ShapeDtypeStruct((M, FOX_W), BF16),
        compiler_params=_cparams(("arbitrary", "arbitrary")),
        name="fox_prompt",
    )(ti, tj, z3, kt, vt, ft)


def _fox_sample_kernel(pt_ref, *refs, npg):
    del pt_ref
    q_ref, kn_ref, vn_ref, zs_ref = refs[0:4]
    lf_refs = refs[4:4 + npg]
    k_refs = refs[4 + npg:4 + 2 * npg]
    v_refs = refs[4 + 2 * npg:4 + 3 * npg]
    o_ref = refs[4 + 3 * npg]
    qbd_ref, m_ref, l_ref, acc_ref, carry_ref = refs[5 + 3 * npg:]
    step = pl.program_id(1)
    nreal = GROUP // 2
    rows = nreal * FOX_H

    hrow = lax.broadcasted_iota(jnp.int32, (FOX_H, FOX_W), 0)
    hcol = lax.broadcasted_iota(jnp.int32, (FOX_H, FOX_W), 1) // FOX_DH
    head_sel = jnp.concatenate([hrow == hcol] * nreal, axis=0)

    @pl.when(step == 0)
    def _():
        q = q_ref[...] * (FOX_DH ** -0.5)
        qbd = jnp.concatenate(
            [jnp.broadcast_to(q[nreal + t:nreal + t + 1, :], (FOX_H, FOX_W)) for t in range(nreal)], axis=0)
        qbd_ref[...] = jnp.where(head_sel, qbd, 0.0).astype(BF16)
        m_ref[...] = jnp.full(m_ref.shape, NEG, F32)
        l_ref[...] = jnp.zeros(l_ref.shape, F32)
        acc_ref[...] = jnp.zeros(acc_ref.shape, F32)
        carry_ref[...] = jnp.zeros(carry_ref.shape, F32)

    lf = jnp.concatenate([r[...] for r in lf_refs], axis=0)
    cum = _dot3_lhs(lf, _tri_incl(LANES, True))
    n = npg * FOX_H
    er = lax.broadcasted_iota(jnp.int32, (n, n), 0)
    ec = lax.broadcasted_iota(jnp.int32, (n, n), 1)
    earlier = ((ec % FOX_H == er % FOX_H) & (ec // FOX_H < er // FOX_H)).astype(BF16)
    prefix = _dot3_rhs(earlier, jnp.broadcast_to(cum[:, LANES - 1:LANES], (n, LANES)))
    f_all = cum + prefix + jnp.concatenate([carry_ref[...]] * npg, axis=0)
    carry = f_all[n - FOX_H:, LANES - 1:LANES]
    fcat = jnp.concatenate([f_all[pg * FOX_H:(pg + 1) * FOX_H, :] for pg in range(npg)], axis=1)
    kcat = jnp.concatenate([r[...].astype(BF16) for r in k_refs], axis=1)
    vcat = jnp.concatenate([r[...].astype(BF16) for r in v_refs], axis=1)
    qbd = qbd_ref[...]
    m, l, acc = m_ref[...], l_ref[...], acc_ref[...]
    eye = (lax.broadcasted_iota(jnp.int32, (rows, rows), 0)
           == lax.broadcasted_iota(jnp.int32, (rows, rows), 1))

    def as_row(col):
        return jnp.sum(jnp.where(eye, jnp.broadcast_to(col, (rows, rows)), 0.0), axis=0, keepdims=True)

    s = _dot(qbd, kcat) - jnp.concatenate([fcat] * nreal, axis=0)
    m_new = jnp.maximum(m, jnp.max(s, axis=1, keepdims=True))
    alpha = jnp.exp(m - m_new)
    pr = jnp.exp(s - m_new)
    l = alpha * l + jnp.sum(pr, axis=1, keepdims=True)
    acc = as_row(alpha) * acc + _dot_nt(vcat, pr.astype(BF16))
    m = m_new
    m_ref[...], l_ref[...], acc_ref[...] = m, l, acc
    carry_ref[...] = jnp.broadcast_to(carry, carry_ref.shape)

    @pl.when(step == pl.num_programs(1) - 1)
    def _():
        zpad = jnp.zeros((LANES - GROUP, FOX_W), F32)
        kn = jnp.concatenate([kn_ref[0], zpad], axis=0).astype(BF16)
        vn_t = jnp.concatenate([vn_ref[0], zpad], axis=0).T.astype(BF16)
        s = _dot_nt(qbd, kn)
        zs = jnp.concatenate([zs_ref[...], jnp.zeros((LANES - GROUP, LANES), F32)], axis=0)
        lf_new = zs.T[0:FOX_H, :]
        kr = lax.broadcasted_iota(jnp.int32, (LANES, LANES), 0)
        kc = lax.broadcasted_iota(jnp.int32, (LANES, LANES), 1)
        inc = ((kr <= kc) & (kr >= nreal) & (kr < GROUP)).astype(BF16)
        f_new = _dot3_lhs(lf_new, inc) + carry
        f_new = jnp.concatenate([f_new] * nreal, axis=0)
        qi = lax.broadcasted_iota(jnp.int32, (rows, LANES), 0) // FOX_H
        kj = lax.broadcasted_iota(jnp.int32, (rows, LANES), 1) - nreal
        s = jnp.where((kj >= 0) & (kj <= qi), s - f_new, NEG)
        m_new = jnp.maximum(m, jnp.max(s, axis=1, keepdims=True))
        alpha = jnp.exp(m - m_new)
        pr = jnp.exp(s - m_new)
        lt = alpha * l + jnp.sum(pr, axis=1, keepdims=True)
        out_t = (as_row(alpha) * acc + _dot_nt(vn_t, pr.astype(BF16))) / as_row(lt)
        own = (lax.broadcasted_iota(jnp.int32, (FOX_W, rows), 0) // FOX_DH
               == lax.broadcasted_iota(jnp.int32, (FOX_W, rows), 1) % FOX_H)
        out_t = jnp.where(own, out_t, 0.0).astype(BF16)
        gr = lax.broadcasted_iota(jnp.int32, (GROUP, rows), 0) - nreal
        gc = lax.broadcasted_iota(jnp.int32, (GROUP, rows), 1) // FOX_H
        o_ref[...] = _dot_nt((gr == gc).astype(BF16), out_t)


def _fox_sample(z3, zs, cache_k, cache_v, cache_lf, page_table, layer, iq, ik, iv, npg=16):
    Bd, n_pages = page_table.shape
    npg = min(npg, n_pages)
    steps = n_pages // npg
    rows = (GROUP // 2) * FOX_H

    def page_spec(shape, g):
        return pl.BlockSpec(shape, lambda b, s, pt: (layer, pt[b, s * npg + g], 0, 0))

    in_specs = [
        pl.BlockSpec((None, GROUP, FOX_W), lambda b, s, pt: (iq, b, 0)),
        pl.BlockSpec((1, GROUP, FOX_W), lambda b, s, pt: (ik, b, 0)),
        pl.BlockSpec((1, GROUP, FOX_W), lambda b, s, pt: (iv, b, 0)),
        pl.BlockSpec((GROUP, LANES), lambda b, s, pt: (b, 0)),
    ]
    in_specs += [page_spec((None, None, FOX_H, PAGE), g) for g in range(npg)]
    in_specs += [page_spec((None, None, FOX_W, PAGE), g) for g in range(npg)]
    in_specs += [page_spec((None, None, FOX_W, PAGE), g) for g in range(npg)]
    grid_spec = pltpu.PrefetchScalarGridSpec(
        num_scalar_prefetch=1,
        grid=(Bd, steps),
        in_specs=in_specs,
        out_specs=pl.BlockSpec((GROUP, FOX_W), lambda b, s, pt: (b, 0)),
        scratch_shapes=[
            pltpu.VMEM((rows, FOX_W), BF16),
            pltpu.VMEM((rows, 1), F32),
            pltpu.VMEM((rows, 1), F32),
            pltpu.VMEM((FOX_W, rows), F32),
            pltpu.VMEM((FOX_H, LANES), F32),
        ],
    )
    return pl.pallas_call(
        functools.partial(_fox_sample_kernel, npg=npg),
        grid_spec=grid_spec,
        out_shape=jax.ShapeDtypeStruct((Bd * GROUP, FOX_W), F32),
        compiler_params=_cparams(("arbitrary", "arbitrary")),
        name="fox_sample",
    )(page_table, z3, z3, z3, zs, *([cache_lf] * npg), *([cache_k] * npg), *([cache_v] * npg))


def _mlstm_kernel(*refs, rin, sample):
    qk_ref, v_ref, og_ref, zs_ref, wc_ref, bc_ref, gh_ref = refs[0:7]
    if sample:
        inj_ref, c0_ref, nm0_ref = refs[7:10]
    h_ref, c_out_ref, nm_out_ref, xbuf, c_s, nm_s = refs[-6:]
    t = pl.program_id(1)
    rc = LANES

    @pl.when(t == 0)
    def _():
        xbuf[0:8, :] = jnp.zeros((8, 2 * ML_W), F32)
        if sample:
            c_s[...] = c0_ref[0]
            nm_s[...] = nm0_ref[0]
        else:
            c_s[...] = jnp.zeros(c_s.shape, F32)
            nm_s[...] = jnp.zeros(nm_s.shape, F32)

    x = jnp.concatenate([qk_ref[0], qk_ref[1]], axis=1)
    if sample:
        slot = lax.broadcasted_iota(jnp.int32, (rin, 1), 0)
        x = jnp.where((slot >= 1) & (slot < GROUP // 2), inj_ref[...], x)
    xbuf[8:8 + rin, :] = x
    y = bc_ref[...] + wc_ref[3:4, :] * x
    for jj in range(ML_CONV - 1):
        y = y + wc_ref[jj:jj + 1, :] * xbuf[pl.ds(8 - (ML_CONV - 1) + jj, rin), :]
    xbuf[0:8, :] = xbuf[rin:rin + 8, :]
    y = y * _sigmoid(y)
    q_all = y[:, 0:ML_W]
    k_all = y[:, ML_W:] * (ML_DH ** -0.5)
    v_all = v_ref[0]
    zs = zs_ref[...]
    if rin < rc:
        pad = jnp.zeros((rc - rin, ML_W), F32)
        q_all = jnp.concatenate([q_all, pad], axis=0)
        k_all = jnp.concatenate([k_all, pad], axis=0)
        v_all = jnp.concatenate([v_all, pad], axis=0)
        zs = jnp.concatenate([zs, jnp.zeros((rc - rin, LANES), F32)], axis=0)
    if sample:
        row = lax.broadcasted_iota(jnp.int32, (rc, LANES), 0)
        lane = lax.broadcasted_iota(jnp.int32, (rc, LANES), 1)
        padrow = (row < GROUP // 2) | (row >= GROUP)
        is_i = (lane >= FOX_H) & (lane < FOX_H + ML_H)
        zs = jnp.where(padrow, jnp.where(is_i, NEG, 0.0), zs)

    colcum = _dot3_rhs(_tri_incl(rc, False), zs)
    rowraw = zs.T[FOX_H:FOX_H + 2 * ML_H, :]
    rowcum = _dot3_lhs(rowraw, _tri_incl(rc, True))
    causal = (lax.broadcasted_iota(jnp.int32, (rc, rc), 1) <= lax.broadcasted_iota(jnp.int32, (rc, rc), 0))

    heads = []
    for h in range(ML_H):
        cs = slice(h * ML_DH, (h + 1) * ML_DH)
        qf = q_all[:, cs]
        kf = k_all[:, cs]
        qh = qf.astype(BF16)
        kh = kf.astype(BF16)
        vh = v_all[:, cs].astype(BF16)
        icol = jnp.broadcast_to(zs[:, FOX_H + h:FOX_H + h + 1], (rc, LANES))
        bcol = jnp.broadcast_to(colcum[:, FOX_H + ML_H + h:FOX_H + ML_H + h + 1], (rc, LANES))
        irow = rowraw[h:h + 1, :]
        brow = rowcum[ML_H + h:ML_H + h + 1, :]
        cmat = c_s[h]
        nrow = nm_s[h, 0:1, :]
        m = nm_s[h, 1:2, :]

        dlog = jnp.where(causal, bcol - brow + irow, NEG)
        g = bcol + m
        mt = jnp.maximum(g, jnp.max(dlog, axis=1, keepdims=True))
        w_intra = jnp.exp(dlog - mt)
        w_inter = jnp.exp(g - mt)
        sc = _dot_nt(qh, kh) * w_intra
        num = _dot(sc.astype(BF16), vh) + _dot(qh, cmat.astype(BF16)) * w_inter
        den = jnp.sum(sc, axis=1, keepdims=True) + w_inter * jnp.sum(qf * nrow, axis=1, keepdims=True)
        hh = num / jnp.maximum(jnp.abs(den), jnp.exp(-mt))

        b_last = bcol[rc - 1:rc, :]
        a = b_last - bcol + icol
        m_new = jnp.maximum(b_last + m, jnp.max(a, axis=0, keepdims=True))
        ws = jnp.exp(a - m_new)
        decay = jnp.exp(b_last + m - m_new)
        kw = kf * ws
        c_s[h] = decay * cmat + _dot_tn(kw.astype(BF16), vh)
        nm_s[h, 0:1, :] = decay * nrow + jnp.sum(kw, axis=0, keepdims=True)
        nm_s[h, 1:2, :] = m_new

        heads.append(_rms(hh, gh_ref[:, cs]))
    hfull = jnp.concatenate(heads, axis=1)
    hfull = hfull[0:rin] * _sigmoid(og_ref[0])
    h_ref[...] = hfull.astype(h_ref.dtype)

    @pl.when(t == pl.num_programs(1) - 1)
    def _():
        c_out_ref[0] = c_s[...]
        nm_out_ref[0] = nm_s[...]


def _mlstm(z3, zs, l, P, B, L, iqk, iv, io, state=None, c_prev=None):
    sample = state is not None
    rin = GROUP if sample else LANES
    nt = L // rin
    M = B * L
    depth = P["w_ml_conv"].shape[0]
    in_specs = [
        pl.BlockSpec((2, rin, ML_W), lambda b, t: (iqk, b * nt + t, 0)),
        pl.BlockSpec((1, rin, ML_W), lambda b, t: (iv, b * nt + t, 0)),
        pl.BlockSpec((1, rin, ML_W), lambda b, t: (io, b * nt + t, 0)),
        pl.BlockSpec((rin, LANES), lambda b, t: (b * nt + t, 0)),
        _layer_spec(l, ML_CONV, 2 * ML_W),
        _layer_spec(l, 1, 2 * ML_W),
        _layer_spec(l, 1, ML_W),
    ]
    args = [z3, z3, z3, zs, P["w_ml_conv"], P["b_ml_conv"], P["g_ml_head"]]
    if sample:
        in_specs += [
            pl.BlockSpec((None, rin, 2 * ML_W), lambda b, t: (l, b * nt + t, 0)),
            pl.BlockSpec((None, 1, ML_H, ML_DH, ML_DH), lambda b, t: (l, b, 0, 0, 0)),
            pl.BlockSpec((None, 1, ML_H, 8, ML_DH), lambda b, t: (l, b, 0, 0, 0)),
        ]
        args += list(state)
    aliases = {}
    if c_prev is not None:
        in_specs.append(pl.BlockSpec(memory_space=pl.ANY))
        args.append(c_prev)
        aliases[len(args) - 1] = 1
    return pl.pallas_call(
        functools.partial(_mlstm_kernel, rin=rin, sample=sample),
        grid=(B, nt),
        in_specs=in_specs,
        out_specs=[
            pl.BlockSpec((rin, ML_W), lambda b, t: (b * nt + t, 0)),
            pl.BlockSpec((None, 1, ML_H, ML_DH, ML_DH), lambda b, t: (l, b, 0, 0, 0)),
            pl.BlockSpec((1, ML_H, 8, ML_DH), lambda b, t: (b, 0, 0, 0)),
        ],
        out_shape=[
            jax.ShapeDtypeStruct((M, ML_W), F32 if sample else BF16),
            jax.ShapeDtypeStruct((depth, B, ML_H, ML_DH, ML_DH), F32),
            jax.ShapeDtypeStruct((B, ML_H, 8, ML_DH), F32),
        ],
        input_output_aliases=aliases,
        scratch_shapes=[
            pltpu.VMEM((rin + 8, 2 * ML_W), F32),
            pltpu.VMEM((ML_H, ML_DH, ML_DH), F32),
            pltpu.VMEM((ML_H, 8, ML_DH), F32),
        ],
        compiler_params=_cparams(("arbitrary", "arbitrary")),
        name="mlstm_sample" if sample else "mlstm_prompt",
    )(*args)


def _mix_xattn_kernel(fo_ref, hm_ref, x_ref, wo_ref, gpm_ref, gpx_ref, wq_ref, mk_ref, mv_ref, wxo_ref, gox_ref, o_ref):
    mix = (_dot(fo_ref[...].astype(BF16), wo_ref[0:FOX_W, :])
           + _dot(hm_ref[...].astype(BF16), wo_ref[FOX_W:, :]))
    x1 = x_ref[...] + _rms(mix, gpm_ref[...])
    hq = _rms(x1, gpx_ref[...]).astype(BF16)
    xq = _dot(hq, wq_ref[...]).astype(BF16)
    n_mem = mk_ref.shape[0] // X_H
    heads = []
    for h in range(X_H):
        cs = slice(h * X_DH, (h + 1) * X_DH)
        mk = mk_ref[pl.ds(h, n_mem, stride=X_H), :].astype(BF16)
        mv = mv_ref[pl.ds(h, n_mem, stride=X_H), :].astype(BF16)
        s = _dot_nt(xq[:, cs], mk) * (X_DH ** -0.5)
        e = jnp.exp(s - jnp.max(s, axis=1, keepdims=True))
        o = _dot(e.astype(BF16), mv) / jnp.sum(e, axis=1, keepdims=True)
        heads.append(o.astype(BF16))
    xo = _dot(jnp.concatenate(heads, axis=1), wxo_ref[...])
    o_ref[...] = x1 + _rms(xo, gox_ref[...])


def _mix_xattn(fo, hm, x, l, P, mk, mv, rows_per_batch, tm):
    M, D = x.shape
    tpb = rows_per_batch // tm
    mem_spec = pl.BlockSpec((None, None) + mk.shape[2:], lambda i: (l, i // tpb, 0, 0))
    return pl.pallas_call(
        _mix_xattn_kernel,
        grid=(M // tm,),
        in_specs=[
            pl.BlockSpec((tm, FOX_W), lambda i: (i, 0)),
            pl.BlockSpec((tm, ML_W), lambda i: (i, 0)),
            pl.BlockSpec((tm, D), lambda i: (i, 0)),
            _layer_spec(l, FOX_W + ML_W, D),
            _layer_spec(l, 1, D), _layer_spec(l, 1, D),
            _layer_spec(l, D, X_W),
            mem_spec, mem_spec,
            _layer_spec(l, X_W, D),
            _layer_spec(l, 1, D),
        ],
        out_specs=pl.BlockSpec((tm, D), lambda i: (i, 0)),
        out_shape=jax.ShapeDtypeStruct((M, D), F32),
        compiler_params=_cparams(("arbitrary",)),
        name="mix_xattn",
    )(fo, hm, x, P["w_out"], P["g_post_mix"], P["g_pre_x"], P["w_xq"], mk, mv, P["w_xo"], P["g_post_x"])


def _convffn_kernel(*refs, tm, tiles_per_batch, sample):
    if sample:
        (x_ref, gpre_ref, wg_ref, wu_ref, wc_ref, bc_ref, wd_ref, gpost_ref, inj_ref,
         o_ref, gate_ref, xn_ref, acc_ref, gbuf, tails) = refs
    else:
        (x_ref, gpre_ref, wg_ref, wu_ref, wc_ref, bc_ref, wd_ref, gpost_ref,
         o_ref, gate_ref, xn_ref, acc_ref, gbuf, tails) = refs
    i = pl.program_id(0)
    j = pl.program_id(1)

    @pl.when(j == 0)
    def _():
        xn_ref[...] = _rms(x_ref[...], gpre_ref[...]).astype(BF16)
        acc_ref[...] = jnp.zeros_like(acc_ref)

    xn = xn_ref[...]
    g = _dot(xn, wg_ref[...])
    if sample:
        slot = lax.broadcasted_iota(jnp.int32, (tm, 1), 0) % GROUP
        g = jnp.where((slot >= GROUP // 2 - (FFN_CONV - 1)) & (slot < GROUP // 2), inj_ref[...], g)
        gate_ref[...] = g
        gbuf[0:8, :] = jnp.zeros((8, g.shape[1]), F32)
    else:
        first = (i % tiles_per_batch) == 0
        gbuf[0:8, :] = jnp.where(first, 0.0, tails[j])
        gate_ref[0] = g[tm - 8:tm, :]
        tails[j] = g[tm - 8:tm, :]
    gbuf[8:8 + tm, :] = g
    t = bc_ref[...] + wc_ref[2:3, :] * g
    for jj in range(FFN_CONV - 1):
        t = t + wc_ref[jj:jj + 1, :] * gbuf[pl.ds(8 - (FFN_CONV - 1) + jj, tm), :]
    a = _gelu_tanh(t) * _dot(xn, wu_ref[...])
    acc_ref[...] += _dot(a.astype(BF16), wd_ref[...])

    @pl.when(j == pl.num_programs(1) - 1)
    def _():
        o_ref[...] = x_ref[...] + _rms(acc_ref[...], gpost_ref[...])


def _convffn(x, l, P, rows_per_batch, tm, tn, inj=None):
    M, D = x.shape
    F = P["w_gate"].shape[2]
    sample = inj is not None
    nj = F // tn
    tiles_per_batch = max(rows_per_batch // tm, 1)
    in_specs = [
        pl.BlockSpec((tm, D), lambda i, j: (i, 0)),
        _layer_spec(l, 1, D),
        pl.BlockSpec((None, D, tn), lambda i, j: (l, 0, j)),
        pl.BlockSpec((None, D, tn), lambda i, j: (l, 0, j)),
        pl.BlockSpec((None, FFN_CONV, tn), lambda i, j: (l, 0, j)),
        pl.BlockSpec((None, 1, tn), lambda i, j: (l, 0, j)),
        pl.BlockSpec((None, tn, D), lambda i, j: (l, j, 0)),
        _layer_spec(l, 1, D),
    ]
    args = [x, P["g_pre_ffn"], P["w_gate"], P["w_up"], P["w_ffn_conv"], P["b_ffn_conv"], P["w_down"],
            P["g_post_ffn"]]
    if sample:
        in_specs.append(pl.BlockSpec((None, tm, tn), lambda i, j: (l, i, j)))
        args.append(inj)
        gate_shape = jax.ShapeDtypeStruct((M, F), F32)
        gate_spec = pl.BlockSpec((tm, tn), lambda i, j: (i, j))
    else:
        gate_shape = jax.ShapeDtypeStruct((M // tm, 8, F), F32)
        gate_spec = pl.BlockSpec((1, 8, tn), lambda i, j: (i, 0, j))
    return pl.pallas_call(
        functools.partial(_convffn_kernel, tm=tm, tiles_per_batch=tiles_per_batch, sample=sample),
        grid=(M // tm, nj),
        in_specs=in_specs,
        out_specs=[pl.BlockSpec((tm, D), lambda i, j: (i, 0)), gate_spec],
        out_shape=[jax.ShapeDtypeStruct((M, D), F32), gate_shape],
        scratch_shapes=[
            pltpu.VMEM((tm, D), BF16),
            pltpu.VMEM((tm, D), F32),
            pltpu.VMEM((tm + 8, tn), F32),
            pltpu.VMEM((nj, 8, tn), F32),
        ],
        compiler_params=_cparams(("arbitrary", "arbitrary")),
        name="convffn_sample" if sample else "convffn_prompt",
    )(*args)


_IQK, _IMV, _IMO, _IFQ, _IFK, _IFV = 0, 2, 3, 4, 5, 6


def _prep_params(w_in, b_fox_f, b_ml_i, b_ml_f, w_out, w_xq, w_xk, w_xv, w_xo, w_gate, w_up, w_down, vectors):
    depth, K, _ = w_in.shape
    o_ff = 3 * FOX_W
    o_mq = o_ff + FOX_H
    o_mi = o_mq + 3 * ML_W
    o_mo = o_mi + 2 * ML_H
    n_gate = FOX_H + 2 * ML_H
    P = dict(
        wbig=jnp.concatenate([w_in[:, :, o_mq:o_mi], w_in[:, :, o_mo:], w_in[:, :, :o_ff]], axis=2).astype(BF16),
        wsmall=jnp.concatenate([w_in[:, :, o_ff:o_mq], w_in[:, :, o_mi:o_mo],
                                jnp.zeros((depth, K, LANES - n_gate), F32)], axis=2).astype(BF16),
        bsmall=jnp.concatenate([b_fox_f, b_ml_i, b_ml_f, jnp.zeros((depth, LANES - n_gate), F32)],
                               axis=1).reshape(depth, 1, LANES),
        wkv_t=jnp.transpose(w_in[:, :, FOX_W:o_ff].reshape(depth, K, 2, FOX_W), (0, 2, 3, 1)).astype(BF16),
        w_out=w_out.astype(BF16), w_xq=w_xq.astype(BF16),
        w_xkv=jnp.concatenate([w_xk, w_xv], axis=2).astype(BF16),
        w_xo=w_xo.astype(BF16), w_gate=w_gate.astype(BF16), w_up=w_up.astype(BF16), w_down=w_down.astype(BF16))
    for name, v in vectors.items():
        P[name] = v.reshape(depth, 1, v.shape[-1]) if v.ndim == 2 else v
    return P


def kernel(x_prompt, x_sample, mem_prompt, cache_fox_k, cache_fox_v, cache_fox_logf, state_ml_C, state_ml_n, state_ml_m, state_ml_conv, state_ffn_conv, cache_mem_k, cache_mem_v, page_table, g_pre_mix, w_in, b_fox_f, w_ml_conv, b_ml_conv, b_ml_i, b_ml_f, g_ml_head, w_out, g_post_mix, g_pre_x, g_mem, w_xq, w_xk, w_xv, w_xo, g_post_x, g_pre_ffn, w_gate, w_up, w_ffn_conv, b_ffn_conv, w_down, g_post_ffn):
    B, S, D = x_prompt.shape
    Bd, Ld, _ = x_sample.shape
    depth = w_in.shape[0]
    n_mem = mem_prompt.shape[1]
    d_ff = w_gate.shape[2]
    pool = cache_fox_k.shape[1]
    pad = GROUP - Ld
    assert Ld == GROUP // 2 and S % 256 == 0

    xp = x_prompt.reshape(B * S, D)
    xs = jnp.pad(x_sample, ((0, 0), (pad, 0), (0, 0))).reshape(Bd * GROUP, D)
    memx = mem_prompt.reshape(B * n_mem, D)
    ck = jnp.transpose(cache_fox_k, (0, 1, 3, 4, 2)).reshape(depth, pool, FOX_W, PAGE)
    cv = jnp.transpose(cache_fox_v, (0, 1, 3, 4, 2)).reshape(depth, pool, FOX_W, PAGE)
    clf = jnp.transpose(cache_fox_logf, (0, 1, 3, 2))
    ffn_tn = d_ff // 2 if (d_ff // 2) % LANES == 0 else d_ff

    P = _prep_params(
        w_in, b_fox_f, b_ml_i, b_ml_f, w_out, w_xq, w_xk, w_xv, w_xo, w_gate, w_up, w_down,
        dict(g_pre_mix=g_pre_mix, g_mem=g_mem, w_ml_conv=w_ml_conv, b_ml_conv=b_ml_conv, g_ml_head=g_ml_head,
             g_post_mix=g_post_mix, g_pre_x=g_pre_x, g_post_x=g_post_x, g_pre_ffn=g_pre_ffn,
             w_ffn_conv=w_ffn_conv, b_ffn_conv=b_ffn_conv, g_post_ffn=g_post_ffn))
    inj_ml = jnp.pad(state_ml_conv, ((0, 0), (0, 0), (1, GROUP - ML_CONV), (0, 0))).reshape(depth, Bd * GROUP, 2 * ML_W)
    inj_ffn = jnp.pad(state_ffn_conv, ((0, 0), (0, 0), (GROUP // 2 - (FFN_CONV - 1), GROUP // 2), (0, 0)))
    inj_ffn = inj_ffn.reshape(depth, Bd * GROUP, d_ff)
    nm0 = jnp.concatenate(
        [state_ml_n[:, :, :, None, :],
         jnp.broadcast_to(state_ml_m[:, :, :, None, None], (depth, Bd, ML_H, 1, ML_DH)),
         jnp.zeros((depth, Bd, ML_H, 6, ML_DH), F32)], axis=3)
    smk = cache_mem_k.reshape(depth, Bd, n_mem * X_H, X_DH)
    smv = cache_mem_v.reshape(depth, Bd, n_mem * X_H, X_DH)

    outs = {k: [] for k in ("lf_p", "fk_s", "fv_s", "lf_s", "nm_p", "cv_p", "nm_s", "cv_s", "fc_p", "fc_s")}
    kv_p = mem_p = c_p = c_s = None
    for l in range(depth):
        mem_p = _mem_proj(memx, l, P, B, prev=mem_p)
        z3, zs, *kv_p = _norm_proj(xp, l, P, _IFK, FOX_W, rows_per_batch=S, t_prev=kv_p)
        ft = _fox_cum(zs, B, S)
        fo = _fox_prompt(z3, kv_p[0], kv_p[1], ft, B, S, _IFQ, l)
        hm, c_p, nm_new = _mlstm(z3, zs, l, P, B, S, _IQK, _IMV, _IMO, c_prev=c_p)
        xp = _mix_xattn(fo, hm, xp, l, P, mem_p[0], mem_p[1], S, min(512, S))
        xp, gate_tail = _convffn(xp, l, P, S, min(512, S), ffn_tn)
        outs["lf_p"].append(zs[:, :FOX_H].reshape(B, S, FOX_H))
        outs["nm_p"].append(nm_new)
        tails = z3.reshape(z3.shape[0], B, S, ML_W)[0:2, :, S - (ML_CONV - 1):, :]
        outs["cv_p"].append(jnp.transpose(tails, (1, 2, 0, 3)).reshape(B, ML_CONV - 1, 2 * ML_W))
        outs["fc_p"].append(gate_tail.reshape(B, -1, 8, d_ff)[:, -1, 8 - (FFN_CONV - 1):, :])

        z3, zs = _norm_proj(xs, l, P, _IFV + 1, FOX_W)
        fo = _fox_sample(z3, zs, ck, cv, clf, page_table, l, _IFQ, _IFK, _IFV)
        hm, c_s, nm_new = _mlstm(z3, zs, l, P, Bd, GROUP, _IQK, _IMV, _IMO,
                                 state=(inj_ml, state_ml_C, nm0), c_prev=c_s)
        xs = _mix_xattn(fo, hm, xs, l, P, smk, smv, GROUP, GROUP)
        xs, gate_full = _convffn(xs, l, P, Bd * GROUP, Bd * GROUP, ffn_tn, inj=inj_ffn)
        real = lambda a: a.reshape((Bd, GROUP) + a.shape[1:])[:, pad:]
        outs["fk_s"].append(real(z3[_IFK]).reshape(Bd, Ld, FOX_H, FOX_DH))
        outs["fv_s"].append(real(z3[_IFV]).reshape(Bd, Ld, FOX_H, FOX_DH))
        outs["lf_s"].append(real(zs)[:, :, :FOX_H])
        outs["nm_s"].append(nm_new)
        tails = z3.reshape(z3.shape[0], Bd, GROUP, ML_W)[0:2, :, GROUP - (ML_CONV - 1):, :]
        outs["cv_s"].append(jnp.transpose(tails, (1, 2, 0, 3)).reshape(Bd, ML_CONV - 1, 2 * ML_W))
        outs["fc_s"].append(gate_full.reshape(Bd, GROUP, d_ff)[:, GROUP - (FFN_CONV - 1):, :])

    st = lambda k: jnp.stack(outs[k], axis=0)
    y_p = xp.reshape(B, S, D)
    y_s = xs.reshape(Bd, GROUP, D)[:, pad:]
    to_tokens = lambda a: jnp.transpose(a.reshape(depth, B, FOX_H, FOX_DH, S), (0, 1, 4, 2, 3))
    nm_p, nm_s = st("nm_p"), st("nm_s")
    mem_leaf = lambda a: a.reshape(depth, B, n_mem, X_H, X_DH)
    return (y_p, y_s, to_tokens(kv_p[0]), to_tokens(kv_p[1]), st("lf_p"), st("fk_s"), st("fv_s"), st("lf_s"),
            c_p, nm_p[:, :, :, 0, :], nm_p[:, :, :, 1, 0], st("cv_p"),
            c_s, nm_s[:, :, :, 0, :], nm_s[:, :, :, 1, 0], st("cv_s"),
            st("fc_p"), st("fc_s"), mem_leaf(mem_p[0]), mem_leaf(mem_p[1]))
```

```python
import functools

import jax
import jax.numpy as jnp
from jax import lax
from jax.experimental import pallas as pl
from jax.experimental.pallas import tpu as pltpu

FOX_H, FOX_DH = 8, 64
FOX_W = FOX_H * FOX_DH
ML_H, ML_DH = 4, 128
ML_W = ML_H * ML_DH
ML_CONV = 4
X_H, X_DH = 4, 128
X_W = X_H * X_DH
FFN_CONV = 3
EPS = 1e-6
PAGE = 128
NEG = -1e30
LOG2E = 1.4426950408889634
GROUP = 8
LANES = 128
VMEM_LIMIT = 56 * 1024 * 1024

F32 = jnp.float32
BF16 = jnp.bfloat16


def _cparams(sem):
    return pltpu.CompilerParams(dimension_semantics=sem, vmem_limit_bytes=VMEM_LIMIT)


def _split3(x):
    hi = x.astype(BF16)
    r1 = x - hi.astype(F32)
    mid = r1.astype(BF16)
    lo = (r1 - mid.astype(F32)).astype(BF16)
    return hi, mid, lo


def _dot(a, b):
    return jnp.dot(a, b, preferred_element_type=F32)


def _dot_nt(a, b):
    return lax.dot_general(a, b, (((1,), (1,)), ((), ())), preferred_element_type=F32)


def _dot_tn(a, b):
    return lax.dot_general(a, b, (((0,), (0,)), ((), ())), preferred_element_type=F32)


def _dot3_lhs(x, w):
    hi, mid, lo = _split3(x)
    return _dot(hi, w) + _dot(mid, w) + _dot(lo, w)


def _dot3_rhs(w, x):
    hi, mid, lo = _split3(x)
    return _dot(w, hi) + _dot(w, mid) + _dot(w, lo)


def _rms(x, g):
    return x * lax.rsqrt(jnp.mean(x * x, axis=-1, keepdims=True) + EPS) * g


def _log_sigmoid(x):
    return jnp.minimum(x, 0.0) - jnp.log1p(jnp.exp(-jnp.abs(x)))


def _sigmoid(x):
    return 1.0 / (1.0 + jnp.exp(-x))


def _gelu_tanh(x):
    c = 0.7978845608028654
    return x * (0.5 * (1.0 + jnp.tanh(c * (x + 0.044715 * (x * x * x)))))


def _slab_spec(l, depth, first, block, index_map):
    if first:
        return pl.BlockSpec((depth,) + block, lambda *a: (0,) + tuple(index_map(*a)))
    return pl.BlockSpec((1,) + block, lambda *a: (l,) + tuple(index_map(*a)))


def _store_slab(ref, slab, value):
    li, nl = slab
    for k in range(nl):
        ref[k] = value if k == li else jnp.zeros_like(value)


def _norm_proj_kernel(*refs, tn, nz, with_gates, nt, n_alias, slab):
    refs = list(refs)
    x_ref, g_ref, w_ref = refs[0:3]
    pos = 3
    if with_gates:
        ws_ref, bs_ref = refs[pos:pos + 2]
        pos += 2
    if nt:
        wt_ref = refs[pos]
        pos += 1
    pos += n_alias
    z_ref = refs[pos]
    pos += 1
    if with_gates:
        zs_ref = refs[pos]
        pos += 1
    t_refs = refs[pos:pos + nt]

    xn = _rms(x_ref[...], g_ref[...]).astype(BF16)
    if with_gates:
        zs = _dot(xn, ws_ref[...]) + bs_ref[...]
        lane = lax.broadcasted_iota(jnp.int32, zs.shape, 1)
        forget = (lane < FOX_H) | ((lane >= FOX_H + ML_H) & (lane < FOX_H + 2 * ML_H))
        zs_ref[...] = jnp.where(forget, _log_sigmoid(zs), zs)
    for j in range(nz):
        z_ref[j] = _dot(xn, w_ref[:, j * tn:(j + 1) * tn])
    for j in range(nt):
        _store_slab(t_refs[j], slab, _dot_nt(wt_ref[j], xn)[None])


def _layer_spec(l, *shape):
    return pl.BlockSpec((None,) + shape, lambda *_: (l,) + (0,) * len(shape))


def _norm_proj(x, l, P, nz, tn, rows_per_batch=None, t_prev=None, tm=512):
    M, K = x.shape
    tm = min(tm, M)
    depth = P["wbig"].shape[0]
    nt = 0 if rows_per_batch is None else P["wkv_t"].shape[1]
    n_alias = 0 if t_prev is None else nt
    aliases = {}
    in_specs = [
        pl.BlockSpec((tm, K), lambda i: (i, 0)),
        _layer_spec(l, 1, K),
        _layer_spec(l, K, nz * tn),
        _layer_spec(l, K, LANES),
        _layer_spec(l, 1, LANES),
    ]
    args = [x, P["g_pre_mix"], P["wbig"], P["wsmall"], P["bsmall"]]
    if nt:
        in_specs.append(_layer_spec(l, nt, tn, K))
        args.append(P["wkv_t"])
    out_shape = [jax.ShapeDtypeStruct((nz, M, tn), F32), jax.ShapeDtypeStruct((M, LANES), F32)]
    out_specs = [pl.BlockSpec((nz, tm, tn), lambda i: (0, i, 0)), pl.BlockSpec((tm, LANES), lambda i: (i, 0))]
    if nt:
        tpb = rows_per_batch // tm
        for j in range(nt):
            if t_prev is not None:
                in_specs.append(pl.BlockSpec(memory_space=pl.ANY))
                args.append(t_prev[j])
                aliases[len(args) - 1] = len(out_shape)
            out_shape.append(jax.ShapeDtypeStruct((depth, M // rows_per_batch, tn, rows_per_batch), F32))
            out_specs.append(_slab_spec(l, depth, t_prev is None, (1, tn, tm), lambda i: (i // tpb, 0, i % tpb)))
    slab = (l, depth) if t_prev is None else (0, 1)
    return pl.pallas_call(
        functools.partial(_norm_proj_kernel, tn=tn, nz=nz, with_gates=True, nt=nt, n_alias=n_alias, slab=slab),
        grid=(M // tm,),
        in_specs=in_specs,
        out_specs=out_specs,
        out_shape=out_shape,
        input_output_aliases=aliases,
        compiler_params=_cparams(("arbitrary",)),
        name="norm_proj",
    )(*args)


def _mem_proj_kernel(x_ref, g_ref, w_ref, *refs, slab):
    k_ref, v_ref = refs[-2:]
    li, nl = slab
    xn = _rms(x_ref[...], g_ref[...]).astype(BF16)
    n = x_ref.shape[0]
    for j, o_ref in enumerate((k_ref, v_ref)):
        for k in range(nl):
            if k != li:
                o_ref[k] = jnp.zeros(o_ref.shape[1:], F32)
        for h in range(X_H):
            col = j * X_W + h * X_DH
            o_ref[li, 0, pl.ds(h, n, stride=X_H), :] = _dot(xn, w_ref[:, col:col + X_DH])


def _mem_proj(x, l, P, B, prev=None):
    M, K = x.shape
    n_mem = M // B
    depth = P["w_xkv"].shape[0]
    in_specs = [pl.BlockSpec((n_mem, K), lambda b: (b, 0)), _layer_spec(l, 1, K), _layer_spec(l, K, 2 * X_W)]
    args = [x, P["g_mem"], P["w_xkv"]]
    aliases = {}
    if prev is not None:
        for j in range(2):
            in_specs.append(pl.BlockSpec(memory_space=pl.ANY))
            args.append(prev[j])
            aliases[len(args) - 1] = j
    shape = jax.ShapeDtypeStruct((depth, B, n_mem * X_H, X_DH), F32)
    spec = _slab_spec(l, depth, prev is None, (1, n_mem * X_H, X_DH), lambda b: (b, 0, 0))
    return pl.pallas_call(
        functools.partial(_mem_proj_kernel, slab=(l, depth) if prev is None else (0, 1)),
        grid=(B,),
        in_specs=in_specs,
        out_specs=[spec, spec],
        out_shape=[shape, shape],
        input_output_aliases=aliases,
        compiler_params=_cparams(("arbitrary",)),
        name="mem_proj",
    )(*args)


def _tri_incl(n, upper):
    r = lax.broadcasted_iota(jnp.int32, (n, n), 0)
    c = lax.broadcasted_iota(jnp.int32, (n, n), 1)
    return ((r <= c) if upper else (c <= r)).astype(BF16)


BIAS_ROWS = 16


def _fox_cum_kernel(zs_ref, ft_ref, *, nb):
    n = nb * FOX_H
    lf = jnp.concatenate([zs_ref[u * LANES:(u + 1) * LANES, :].T[0:FOX_H, :] for u in range(nb)], axis=0)
    cum = _dot3_lhs(lf, _tri_incl(LANES, True))
    r = lax.broadcasted_iota(jnp.int32, (n, n), 0)
    c = lax.broadcasted_iota(jnp.int32, (n, n), 1)
    earlier = ((c % FOX_H == r % FOX_H) & (c // FOX_H < r // FOX_H)).astype(BF16)
    prefix = _dot3_rhs(earlier, jnp.broadcast_to(cum[:, LANES - 1:LANES], (n, LANES)))
    pieces = _split3((cum + prefix) * (-LOG2E))
    ro = lax.broadcasted_iota(jnp.int32, (nb * (FOX_H // 2) * BIAS_ROWS, n), 0)
    co = lax.broadcasted_iota(jnp.int32, (nb * (FOX_H // 2) * BIAS_ROWS, n), 1)
    blk = ro // ((FOX_H // 2) * BIAS_ROWS) == co // FOX_H
    pair = (ro // BIAS_ROWS) % (FOX_H // 2)
    slot = ro % BIAS_ROWS
    head = co % FOX_H
    out = None
    for idx, piece in enumerate(pieces):
        sel = blk & (((slot == idx) & (head == 2 * pair)) | ((slot == 3 + idx) & (head == 2 * pair + 1)))
        term = _dot(sel.astype(BF16), piece)
        out = term if out is None else out + term
    ft_ref[0] = out.reshape(nb, FOX_H // 2, BIAS_ROWS, LANES)


def _fox_cum(zs, B, S):
    nb = S // LANES
    return pl.pallas_call(
        functools.partial(_fox_cum_kernel, nb=nb),
        grid=(B,),
        in_specs=[pl.BlockSpec((S, LANES), lambda b: (b, 0))],
        out_specs=pl.BlockSpec((1, nb, FOX_H // 2, BIAS_ROWS, LANES), lambda b: (b, 0, 0, 0, 0)),
        out_shape=jax.ShapeDtypeStruct((B, nb, FOX_H // 2, BIAS_ROWS, LANES), F32),
        compiler_params=_cparams(("arbitrary",)),
        name="fox_cum",
    )(zs)


def _fox_prompt_kernel(ti_ref, tj_ref, q_ref, kt_ref, vt_ref, ft_ref, o_ref, m_s, l_s, acc_s, *, tq):
    t = pl.program_id(1)
    i = ti_ref[t]
    j = tj_ref[t]
    sub = tq // LANES
    npair = FOX_H // 2
    low = lax.broadcasted_iota(jnp.int32, (tq, LANES), 1) < FOX_DH

    @pl.when(j == 0)
    def _():
        m_s[...] = jnp.full(m_s.shape, NEG, F32)
        l_s[...] = jnp.zeros(l_s.shape, F32)
        acc_s[...] = jnp.zeros(acc_s.shape, F32)

    def update(masked):
        if masked:
            causal = (lax.broadcasted_iota(jnp.int32, (tq, tq), 1)
                      <= lax.broadcasted_iota(jnp.int32, (tq, tq), 0))
        for p in range(npair):
            rows = slice(p * LANES, (p + 1) * LANES)
            qp = q_ref[0, :, rows] * (FOX_DH ** -0.5 * LOG2E)
            bias = jnp.concatenate([ft_ref[0, u, p] for u in range(sub)], axis=1).astype(BF16)
            ka = jnp.concatenate(
                [kt_ref[rows, :].astype(BF16), bias, jnp.zeros((LANES - BIAS_ROWS, tq), BF16)], axis=0)
            vb = vt_ref[rows, :].astype(BF16)
            for e in range(2):
                h = 2 * p + e
                qh = jnp.where(low if e == 0 else jnp.logical_not(low), qp, 0.0)
                ones = ((bias_lane >= 3 * e) & (bias_lane < 3 * e + 3)).astype(F32)
                qa = jnp.concatenate([qh, ones], axis=1).astype(BF16)
                s = _dot(qa, ka)
                if masked:
                    s = jnp.where(causal, s, NEG)
                sc = [s[:, c * LANES:(c + 1) * LANES] for c in range(sub)]
                mx = functools.reduce(jnp.maximum, sc)
                m_prev = m_s[h]
                m_new = jnp.maximum(m_prev, jnp.max(mx, axis=1, keepdims=True))
                alpha = jnp.exp2(m_prev - m_new)
                ps = [jnp.exp2(c - m_new) for c in sc]
                l_s[h] = alpha * l_s[h] + jnp.sum(functools.reduce(jnp.add, ps), axis=1, keepdims=True)
                pr = jnp.concatenate([c.astype(BF16) for c in ps], axis=1)
                acc_s[h] = alpha * acc_s[h] + _dot_nt(pr, vb)
                m_s[h] = m_new

    bias_lane = lax.broadcasted_iota(jnp.int32, (tq, LANES), 1)

    @pl.when(j < i)
    def _():
        update(False)

    @pl.when(j == i)
    def _():
        update(True)
        for p in range(npair):
            oe = acc_s[2 * p] / l_s[2 * p]
            oo = acc_s[2 * p + 1] / l_s[2 * p + 1]
            o_ref[:, p * LANES:(p + 1) * LANES] = jnp.where(low, oe, oo).astype(o_ref.dtype)


def _fox_prompt(z3, kt, vt, ft, B, S, iq, layer, tq=512):
    tq = min(tq, S)
    nq = S // tq
    M = B * S
    sub = tq // LANES
    pairs = [(i, j) for i in range(nq) for j in range(i + 1)]
    ti = jnp.asarray([p[0] for p in pairs], jnp.int32)
    tj = jnp.asarray([p[1] for p in pairs], jnp.int32)
    grid_spec = pltpu.PrefetchScalarGridSpec(
        num_scalar_prefetch=2,
        grid=(B, len(pairs)),
        in_specs=[
            pl.BlockSpec((1, tq, FOX_W), lambda b, t, ti, tj: (iq, b * nq + ti[t], 0)),
            pl.BlockSpec((None, None, FOX_W, tq), lambda b, t, ti, tj: (layer, b, 0, tj[t])),
            pl.BlockSpec((None, None, FOX_W, tq), lambda b, t, ti, tj: (layer, b, 0, tj[t])),
            pl.BlockSpec((1, sub, FOX_H // 2, BIAS_ROWS, LANES), lambda b, t, ti, tj: (b, tj[t], 0, 0, 0)),
        ],
        out_specs=pl.BlockSpec((tq, FOX_W), lambda b, t, ti, tj: (b * nq + ti[t], 0)),
        scratch_shapes=[
            pltpu.VMEM((FOX_H, tq, LANES), F32),
            pltpu.VMEM((FOX_H, tq, LANES), F32),
            pltpu.VMEM((FOX_H, tq, LANES), F32),
        ],
    )
    return pl.pallas_call(
        functools.partial(_fox_prompt_kernel, tq=tq),
        grid_spec=grid_spec,
        out_shape=jax.ShapeDtypeStruct((M, FOX_W), BF16),
        compiler_params=_cparams(("arbitrary", "arbitrary")),
        name="fox_prompt",
    )(ti, tj, z3, kt, vt, ft)


SKIP_LOG = -110.0


def _fox_decode_kernel(pt_ref, q_ref, kn_ref, vn_ref, zs_ref, lf_hbm, k_hbm, v_hbm, o_ref,
                       kbuf, vbuf, lfbuf, sbuf, m_ref, l_ref, acc_ref, need_ref, ksem, vsem, lfsem,
                       *, layer, npg, ng):
    b = pl.program_id(0)
    nreal = GROUP // 2
    rows = nreal * FOX_H
    n = npg * FOX_H

    def copies(hbm, buf, sem, g, slot):
        return [pltpu.make_async_copy(hbm.at[layer, pt_ref[b, g * npg + i]], buf.at[slot, i], sem.at[slot])
                for i in range(npg)]

    def start_keys(g, slot):
        for c in copies(k_hbm, kbuf, ksem, g, slot) + copies(lf_hbm, lfbuf, lfsem, g, slot):
            c.start()

    def wait_keys(g, slot):
        for c in copies(k_hbm, kbuf, ksem, g, slot) + copies(lf_hbm, lfbuf, lfsem, g, slot):
            c.wait()

    order = [ng - 1 - it for it in range(ng)]
    start_keys(order[0], 0)

    hrow = lax.broadcasted_iota(jnp.int32, (FOX_H, FOX_W), 0)
    hcol = lax.broadcasted_iota(jnp.int32, (FOX_H, FOX_W), 1) // FOX_DH
    head_sel = jnp.concatenate([hrow == hcol] * nreal, axis=0)
    q = q_ref[...] * (FOX_DH ** -0.5)
    qbd = jnp.concatenate(
        [jnp.broadcast_to(q[nreal + t:nreal + t + 1, :], (FOX_H, FOX_W)) for t in range(nreal)], axis=0)
    qbd = jnp.where(head_sel, qbd, 0.0).astype(BF16)
    eye = (lax.broadcasted_iota(jnp.int32, (rows, rows), 0)
           == lax.broadcasted_iota(jnp.int32, (rows, rows), 1))

    def as_row(col):
        return jnp.sum(jnp.where(eye, jnp.broadcast_to(col, (rows, rows)), 0.0), axis=0, keepdims=True)

    zpad = jnp.zeros((LANES - GROUP, FOX_W), F32)
    kn = jnp.concatenate([kn_ref[0], zpad], axis=0).astype(BF16)
    vn_t = jnp.concatenate([vn_ref[0], zpad], axis=0).T.astype(BF16)
    zs = jnp.concatenate([zs_ref[...], jnp.zeros((LANES - GROUP, LANES), F32)], axis=0)
    lf_new = zs.T[0:FOX_H, :]
    kr = lax.broadcasted_iota(jnp.int32, (LANES, LANES), 0)
    kc = lax.broadcasted_iota(jnp.int32, (LANES, LANES), 1)
    inc = ((kr <= kc) & (kr >= nreal) & (kr < GROUP)).astype(BF16)
    f_new = jnp.concatenate([_dot3_lhs(lf_new, inc)] * nreal, axis=0)
    qi = lax.broadcasted_iota(jnp.int32, (rows, LANES), 0) // FOX_H
    kj = lax.broadcasted_iota(jnp.int32, (rows, LANES), 1) - nreal
    s = jnp.where((kj >= 0) & (kj <= qi), _dot_nt(qbd, kn) - f_new, NEG)
    m0 = jnp.max(s, axis=1, keepdims=True)
    p0 = jnp.exp(s - m0)
    m_ref[...] = m0
    l_ref[...] = jnp.sum(p0, axis=1, keepdims=True)
    acc_ref[...] = _dot_nt(vn_t, p0.astype(BF16))

    er = lax.broadcasted_iota(jnp.int32, (n, n), 0)
    ec = lax.broadcasted_iota(jnp.int32, (n, n), 1)
    later_pages = ((ec % FOX_H == er % FOX_H) & (ec // FOX_H > er // FOX_H)).astype(BF16)

    def update(g, slot):
        @pl.when(need_ref[slot] == 1)
        def _():
            for c in copies(v_hbm, vbuf, vsem, g, slot):
                c.wait()
            vcat = jnp.concatenate([vbuf[slot, i].astype(BF16) for i in range(npg)], axis=1)
            sg = sbuf[slot]
            m = m_ref[...]
            m_new = jnp.maximum(m, jnp.max(sg, axis=1, keepdims=True))
            alpha = jnp.exp(m - m_new)
            pr = jnp.exp(sg - m_new)
            l_ref[...] = alpha * l_ref[...] + jnp.sum(pr, axis=1, keepdims=True)
            acc_ref[...] = as_row(alpha) * acc_ref[...] + _dot_nt(vcat, pr.astype(BF16))
            m_ref[...] = m_new

    carry = jnp.zeros((FOX_H, 1), F32)
    for it, g in enumerate(order):
        slot = it % 2
        if it + 1 < ng:
            start_keys(order[it + 1], 1 - slot)
        wait_keys(g, slot)
        lf = jnp.concatenate([lfbuf[slot, i] for i in range(npg)], axis=0)
        incl = _dot3_lhs(lf, _tri_incl(LANES, False))
        later = _dot3_rhs(later_pages, jnp.broadcast_to(incl[:, 0:1], (n, LANES)))
        after = incl - lf + later + jnp.concatenate([jnp.broadcast_to(carry, (FOX_H, LANES))] * npg, axis=0)
        carry = after[0:FOX_H, 0:1] + lf[0:FOX_H, 0:1]
        gcat = jnp.concatenate([after[i * FOX_H:(i + 1) * FOX_H, :] for i in range(npg)], axis=1)
        kcat = jnp.concatenate([kbuf[slot, i].astype(BF16) for i in range(npg)], axis=1)
        sg = _dot(qbd, kcat) + jnp.concatenate([gcat] * nreal, axis=0)
        sbuf[slot] = sg
        need = jnp.max(jnp.max(sg, axis=1, keepdims=True) - m_ref[...]) > SKIP_LOG
        need_ref[slot] = need.astype(jnp.int32)

        @pl.when(need)
        def _():
            for c in copies(v_hbm, vbuf, vsem, g, slot):
                c.start()

        if it >= 1:
            update(order[it - 1], 1 - slot)
    update(order[ng - 1], (ng - 1) % 2)

    out_t = acc_ref[...] / as_row(l_ref[...])
    own = (lax.broadcasted_iota(jnp.int32, (FOX_W, rows), 0) // FOX_DH
           == lax.broadcasted_iota(jnp.int32, (FOX_W, rows), 1) % FOX_H)
    out_t = jnp.where(own, out_t, 0.0).astype(BF16)
    gr = lax.broadcasted_iota(jnp.int32, (GROUP, rows), 0) - nreal
    gc = lax.broadcasted_iota(jnp.int32, (GROUP, rows), 1) // FOX_H
    o_ref[...] = _dot_nt((gr == gc).astype(BF16), out_t)


def _fox_decode(z3, zs, cache_k, cache_v, cache_lf, page_table, layer, iq, ik, iv, npg=16):
    Bd, n_pages = page_table.shape
    npg = min(npg, n_pages)
    ng = n_pages // npg
    rows = (GROUP // 2) * FOX_H
    any_spec = pl.BlockSpec(memory_space=pl.ANY)
    grid_spec = pltpu.PrefetchScalarGridSpec(
        num_scalar_prefetch=1,
        grid=(Bd,),
        in_specs=[
            pl.BlockSpec((None, GROUP, FOX_W), lambda b, pt: (iq, b, 0)),
            pl.BlockSpec((1, GROUP, FOX_W), lambda b, pt: (ik, b, 0)),
            pl.BlockSpec((1, GROUP, FOX_W), lambda b, pt: (iv, b, 0)),
            pl.BlockSpec((GROUP, LANES), lambda b, pt: (b, 0)),
            any_spec, any_spec, any_spec,
        ],
        out_specs=pl.BlockSpec((GROUP, FOX_W), lambda b, pt: (b, 0)),
        scratch_shapes=[
            pltpu.VMEM((2, npg, FOX_W, PAGE), F32),
            pltpu.VMEM((2, npg, FOX_W, PAGE), F32),
            pltpu.VMEM((2, npg, FOX_H, PAGE), F32),
            pltpu.VMEM((2, rows, npg * PAGE), F32),
            pltpu.VMEM((rows, 1), F32),
            pltpu.VMEM((rows, 1), F32),
            pltpu.VMEM((FOX_W, rows), F32),
            pltpu.SMEM((2,), jnp.int32),
            pltpu.SemaphoreType.DMA((2,)),
            pltpu.SemaphoreType.DMA((2,)),
            pltpu.SemaphoreType.DMA((2,)),
        ],
    )
    return pl.pallas_call(
        functools.partial(_fox_decode_kernel, layer=layer, npg=npg, ng=ng),
        grid_spec=grid_spec,
        out_shape=jax.ShapeDtypeStruct((Bd * GROUP, FOX_W), F32),
        compiler_params=_cparams(("arbitrary",)),
        name="fox_decode",
    )(page_table, z3, z3, z3, zs, cache_lf, cache_k, cache_v)


def _mlstm_kernel(*refs, rin, sample, slab):
    qk_ref, v_ref, og_ref, zs_ref, wc_ref, bc_ref, gh_ref = refs[0:7]
    if sample:
        inj_ref, c0_ref, nm0_ref = refs[7:10]
    h_ref, c_out_ref, nm_out_ref, xbuf, c_s, nm_s = refs[-6:]
    t = pl.program_id(1)
    rc = LANES

    @pl.when(t == 0)
    def _():
        xbuf[0:8, :] = jnp.zeros((8, 2 * ML_W), F32)
        if sample:
            c_s[...] = c0_ref[0]
            nm_s[...] = nm0_ref[0]
        else:
            c_s[...] = jnp.zeros(c_s.shape, F32)
            nm_s[...] = jnp.zeros(nm_s.shape, F32)

    x = jnp.concatenate([qk_ref[0], qk_ref[1]], axis=1)
    if sample:
        slot = lax.broadcasted_iota(jnp.int32, (rin, 1), 0)
        x = jnp.where((slot >= 1) & (slot < GROUP // 2), inj_ref[...], x)
    xbuf[8:8 + rin, :] = x
    y = bc_ref[...] + wc_ref[3:4, :] * x
    for jj in range(ML_CONV - 1):
        y = y + wc_ref[jj:jj + 1, :] * xbuf[pl.ds(8 - (ML_CONV - 1) + jj, rin), :]
    xbuf[0:8, :] = xbuf[rin:rin + 8, :]
    y = y * _sigmoid(y)
    q_all = y[:, 0:ML_W]
    k_all = y[:, ML_W:] * (ML_DH ** -0.5)
    v_all = v_ref[0]
    zs = zs_ref[...]
    if rin < rc:
        pad = jnp.zeros((rc - rin, ML_W), F32)
        q_all = jnp.concatenate([q_all, pad], axis=0)
        k_all = jnp.concatenate([k_all, pad], axis=0)
        v_all = jnp.concatenate([v_all, pad], axis=0)
        zs = jnp.concatenate([zs, jnp.zeros((rc - rin, LANES), F32)], axis=0)
    if sample:
        row = lax.broadcasted_iota(jnp.int32, (rc, LANES), 0)
        lane = lax.broadcasted_iota(jnp.int32, (rc, LANES), 1)
        padrow = (row < GROUP // 2) | (row >= GROUP)
        is_i = (lane >= FOX_H) & (lane < FOX_H + ML_H)
        zs = jnp.where(padrow, jnp.where(is_i, NEG, 0.0), zs)

    colcum = _dot3_rhs(_tri_incl(rc, False), zs)
    rowraw = zs.T[FOX_H:FOX_H + 2 * ML_H, :]
    rowcum = _dot3_lhs(rowraw, _tri_incl(rc, True))
    causal = (lax.broadcasted_iota(jnp.int32, (rc, rc), 1) <= lax.broadcasted_iota(jnp.int32, (rc, rc), 0))

    heads = []
    for h in range(ML_H):
        cs = slice(h * ML_DH, (h + 1) * ML_DH)
        qf = q_all[:, cs]
        kf = k_all[:, cs]
        qh = qf.astype(BF16)
        kh = kf.astype(BF16)
        vh = v_all[:, cs].astype(BF16)
        icol = jnp.broadcast_to(zs[:, FOX_H + h:FOX_H + h + 1], (rc, LANES))
        bcol = jnp.broadcast_to(colcum[:, FOX_H + ML_H + h:FOX_H + ML_H + h + 1], (rc, LANES))
        irow = rowraw[h:h + 1, :]
        brow = rowcum[ML_H + h:ML_H + h + 1, :]
        cmat = c_s[h]
        nrow = nm_s[h, 0:1, :]
        m = nm_s[h, 1:2, :]

        dlog = jnp.where(causal, bcol - brow + irow, NEG)
        g = bcol + m
        mt = jnp.maximum(g, jnp.max(dlog, axis=1, keepdims=True))
        w_intra = jnp.exp(dlog - mt)
        w_inter = jnp.exp(g - mt)
        sc = _dot_nt(qh, kh) * w_intra
        num = _dot(sc.astype(BF16), vh) + _dot(qh, cmat.astype(BF16)) * w_inter
        den = jnp.sum(sc, axis=1, keepdims=True) + w_inter * jnp.sum(qf * nrow, axis=1, keepdims=True)
        hh = num / jnp.maximum(jnp.abs(den), jnp.exp(-mt))

        b_last = bcol[rc - 1:rc, :]
        a = b_last - bcol + icol
        m_new = jnp.maximum(b_last + m, jnp.max(a, axis=0, keepdims=True))
        ws = jnp.exp(a - m_new)
        decay = jnp.exp(b_last + m - m_new)
        kw = kf * ws
        c_s[h] = decay * cmat + _dot_tn(kw.astype(BF16), vh)
        nm_s[h, 0:1, :] = decay * nrow + jnp.sum(kw, axis=0, keepdims=True)
        nm_s[h, 1:2, :] = m_new

        heads.append(_rms(hh, gh_ref[:, cs]))
    hfull = jnp.concatenate(heads, axis=1)
    hfull = hfull[0:rin] * _sigmoid(og_ref[0])
    h_ref[...] = hfull.astype(h_ref.dtype)

    @pl.when(t == pl.num_programs(1) - 1)
    def _():
        _store_slab(c_out_ref, slab, c_s[...][None])
        nm_out_ref[0] = nm_s[...]


def _mlstm(z3, zs, l, P, B, L, iqk, iv, io, state=None, c_prev=None):
    sample = state is not None
    rin = GROUP if sample else LANES
    nt = L // rin
    M = B * L
    depth = P["w_ml_conv"].shape[0]
    in_specs = [
        pl.BlockSpec((2, rin, ML_W), lambda b, t: (iqk, b * nt + t, 0)),
        pl.BlockSpec((1, rin, ML_W), lambda b, t: (iv, b * nt + t, 0)),
        pl.BlockSpec((1, rin, ML_W), lambda b, t: (io, b * nt + t, 0)),
        pl.BlockSpec((rin, LANES), lambda b, t: (b * nt + t, 0)),
        _layer_spec(l, ML_CONV, 2 * ML_W),
        _layer_spec(l, 1, 2 * ML_W),
        _layer_spec(l, 1, ML_W),
    ]
    args = [z3, z3, z3, zs, P["w_ml_conv"], P["b_ml_conv"], P["g_ml_head"]]
    if sample:
        in_specs += [
            pl.BlockSpec((None, rin, 2 * ML_W), lambda b, t: (l, b * nt + t, 0)),
            pl.BlockSpec((None, 1, ML_H, ML_DH, ML_DH), lambda b, t: (l, b, 0, 0, 0)),
            pl.BlockSpec((None, 1, ML_H, 8, ML_DH), lambda b, t: (l, b, 0, 0, 0)),
        ]
        args += list(state)
    aliases = {}
    if c_prev is not None:
        in_specs.append(pl.BlockSpec(memory_space=pl.ANY))
        args.append(c_prev)
        aliases[len(args) - 1] = 1
    return pl.pallas_call(
        functools.partial(_mlstm_kernel, rin=rin, sample=sample,
                          slab=(l, depth) if c_prev is None else (0, 1)),
        grid=(B, nt),
        in_specs=in_specs,
        out_specs=[
            pl.BlockSpec((rin, ML_W), lambda b, t: (b * nt + t, 0)),
            _slab_spec(l, depth, c_prev is None, (1, ML_H, ML_DH, ML_DH), lambda b, t: (b, 0, 0, 0)),
            pl.BlockSpec((1, ML_H, 8, ML_DH), lambda b, t: (b, 0, 0, 0)),
        ],
        out_shape=[
            jax.ShapeDtypeStruct((M, ML_W), F32 if sample else BF16),
            jax.ShapeDtypeStruct((depth, B, ML_H, ML_DH, ML_DH), F32),
            jax.ShapeDtypeStruct((B, ML_H, 8, ML_DH), F32),
        ],
        input_output_aliases=aliases,
        scratch_shapes=[
            pltpu.VMEM((rin + 8, 2 * ML_W), F32),
            pltpu.VMEM((ML_H, ML_DH, ML_DH), F32),
            pltpu.VMEM((ML_H, 8, ML_DH), F32),
        ],
        compiler_params=_cparams(("arbitrary", "arbitrary")),
        name="mlstm_sample" if sample else "mlstm_prompt",
    )(*args)


def _mix_xattn_kernel(fo_ref, hm_ref, x_ref, wo_ref, gpm_ref, gpx_ref, wq_ref, mk_ref, mv_ref, wxo_ref, gox_ref, o_ref):
    mix = (_dot(fo_ref[...].astype(BF16), wo_ref[0:FOX_W, :])
           + _dot(hm_ref[...].astype(BF16), wo_ref[FOX_W:, :]))
    x1 = x_ref[...] + _rms(mix, gpm_ref[...])
    hq = _rms(x1, gpx_ref[...]).astype(BF16)
    xq = _dot(hq, wq_ref[...]).astype(BF16)
    n_mem = mk_ref.shape[0] // X_H
    heads = []
    for h in range(X_H):
        cs = slice(h * X_DH, (h + 1) * X_DH)
        mk = mk_ref[pl.ds(h, n_mem, stride=X_H), :].astype(BF16)
        mv = mv_ref[pl.ds(h, n_mem, stride=X_H), :].astype(BF16)
        s = _dot_nt(xq[:, cs], mk) * (X_DH ** -0.5)
        e = jnp.exp(s - jnp.max(s, axis=1, keepdims=True))
        o = _dot(e.astype(BF16), mv) / jnp.sum(e, axis=1, keepdims=True)
        heads.append(o.astype(BF16))
    xo = _dot(jnp.concatenate(heads, axis=1), wxo_ref[...])
    o_ref[...] = x1 + _rms(xo, gox_ref[...])


def _mix_xattn(fo, hm, x, l, P, mk, mv, rows_per_batch, tm):
    M, D = x.shape
    tpb = rows_per_batch // tm
    mem_spec = pl.BlockSpec((None, None) + mk.shape[2:], lambda i: (l, i // tpb, 0, 0))
    return pl.pallas_call(
        _mix_xattn_kernel,
        grid=(M // tm,),
        in_specs=[
            pl.BlockSpec((tm, FOX_W), lambda i: (i, 0)),
            pl.BlockSpec((tm, ML_W), lambda i: (i, 0)),
            pl.BlockSpec((tm, D), lambda i: (i, 0)),
            _layer_spec(l, FOX_W + ML_W, D),
            _layer_spec(l, 1, D), _layer_spec(l, 1, D),
            _layer_spec(l, D, X_W),
            mem_spec, mem_spec,
            _layer_spec(l, X_W, D),
            _layer_spec(l, 1, D),
        ],
        out_specs=pl.BlockSpec((tm, D), lambda i: (i, 0)),
        out_shape=jax.ShapeDtypeStruct((M, D), F32),
        compiler_params=_cparams(("arbitrary",)),
        name="mix_xattn",
    )(fo, hm, x, P["w_out"], P["g_post_mix"], P["g_pre_x"], P["w_xq"], mk, mv, P["w_xo"], P["g_post_x"])


def _convffn_kernel(*refs, tm, tiles_per_batch, sample):
    if sample:
        (x_ref, gpre_ref, wg_ref, wu_ref, wc_ref, bc_ref, wd_ref, gpost_ref, inj_ref,
         o_ref, gate_ref, xn_ref, acc_ref, gbuf, tails) = refs
    else:
        (x_ref, gpre_ref, wg_ref, wu_ref, wc_ref, bc_ref, wd_ref, gpost_ref,
         o_ref, gate_ref, xn_ref, acc_ref, gbuf, tails) = refs
    i = pl.program_id(0)
    j = pl.program_id(1)

    @pl.when(j == 0)
    def _():
        xn_ref[...] = _rms(x_ref[...], gpre_ref[...]).astype(BF16)
        acc_ref[...] = jnp.zeros_like(acc_ref)

    xn = xn_ref[...]
    g = _dot(xn, wg_ref[...])
    if sample:
        slot = lax.broadcasted_iota(jnp.int32, (tm, 1), 0) % GROUP
        g = jnp.where((slot >= GROUP // 2 - (FFN_CONV - 1)) & (slot < GROUP // 2), inj_ref[...], g)
        gate_ref[...] = g
        gbuf[0:8, :] = jnp.zeros((8, g.shape[1]), F32)
    else:
        first = (i % tiles_per_batch) == 0
        gbuf[0:8, :] = jnp.where(first, 0.0, tails[j])
        gate_ref[0] = g[tm - 8:tm, :]
        tails[j] = g[tm - 8:tm, :]
    gbuf[8:8 + tm, :] = g
    t = bc_ref[...] + wc_ref[2:3, :] * g
    for jj in range(FFN_CONV - 1):
        t = t + wc_ref[jj:jj + 1, :] * gbuf[pl.ds(8 - (FFN_CONV - 1) + jj, tm), :]
    a = _gelu_tanh(t) * _dot(xn, wu_ref[...])
    acc_ref[...] += _dot(a.astype(BF16), wd_ref[...])

    @pl.when(j == pl.num_programs(1) - 1)
    def _():
        o_ref[...] = x_ref[...] + _rms(acc_ref[...], gpost_ref[...])


def _convffn(x, l, P, rows_per_batch, tm, tn, inj=None):
    M, D = x.shape
    F = P["w_gate"].shape[2]
    sample = inj is not None
    nj = F // tn
    tiles_per_batch = max(rows_per_batch // tm, 1)
    in_specs = [
        pl.BlockSpec((tm, D), lambda i, j: (i, 0)),
        _layer_spec(l, 1, D),
        pl.BlockSpec((None, D, tn), lambda i, j: (l, 0, j)),
        pl.BlockSpec((None, D, tn), lambda i, j: (l, 0, j)),
        pl.BlockSpec((None, FFN_CONV, tn), lambda i, j: (l, 0, j)),
        pl.BlockSpec((None, 1, tn), lambda i, j: (l, 0, j)),
        pl.BlockSpec((None, tn, D), lambda i, j: (l, j, 0)),
        _layer_spec(l, 1, D),
    ]
    args = [x, P["g_pre_ffn"], P["w_gate"], P["w_up"], P["w_ffn_conv"], P["b_ffn_conv"], P["w_down"],
            P["g_post_ffn"]]
    if sample:
        in_specs.append(pl.BlockSpec((None, tm, tn), lambda i, j: (l, i, j)))
        args.append(inj)
        gate_shape = jax.ShapeDtypeStruct((M, F), F32)
        gate_spec = pl.BlockSpec((tm, tn), lambda i, j: (i, j))
    else:
        gate_shape = jax.ShapeDtypeStruct((M // tm, 8, F), F32)
        gate_spec = pl.BlockSpec((1, 8, tn), lambda i, j: (i, 0, j))
    return pl.pallas_call(
        functools.partial(_convffn_kernel, tm=tm, tiles_per_batch=tiles_per_batch, sample=sample),
        grid=(M // tm, nj),
        in_specs=in_specs,
        out_specs=[pl.BlockSpec((tm, D), lambda i, j: (i, 0)), gate_spec],
        out_shape=[jax.ShapeDtypeStruct((M, D), F32), gate_shape],
        scratch_shapes=[
            pltpu.VMEM((tm, D), BF16),
            pltpu.VMEM((tm, D), F32),
            pltpu.VMEM((tm + 8, tn), F32),
            pltpu.VMEM((nj, 8, tn), F32),
        ],
        compiler_params=_cparams(("arbitrary", "arbitrary")),
        name="convffn_sample" if sample else "convffn_prompt",
    )(*args)


_IQK, _IMV, _IMO, _IFQ, _IFK, _IFV = 0, 2, 3, 4, 5, 6


def _prep_params(w_in, b_fox_f, b_ml_i, b_ml_f, w_out, w_xq, w_xk, w_xv, w_xo, w_gate, w_up, w_down, vectors):
    depth, K, _ = w_in.shape
    o_ff = 3 * FOX_W
    o_mq = o_ff + FOX_H
    o_mi = o_mq + 3 * ML_W
    o_mo = o_mi + 2 * ML_H
    n_gate = FOX_H + 2 * ML_H
    P = dict(
        wbig=jnp.concatenate([w_in[:, :, o_mq:o_mi], w_in[:, :, o_mo:], w_in[:, :, :o_ff]], axis=2).astype(BF16),
        wsmall=jnp.concatenate([w_in[:, :, o_ff:o_mq], w_in[:, :, o_mi:o_mo],
                                jnp.zeros((depth, K, LANES - n_gate), F32)], axis=2).astype(BF16),
        bsmall=jnp.concatenate([b_fox_f, b_ml_i, b_ml_f, jnp.zeros((depth, LANES - n_gate), F32)],
                               axis=1).reshape(depth, 1, LANES),
        wkv_t=jnp.transpose(w_in[:, :, FOX_W:o_ff].reshape(depth, K, 2, FOX_W), (0, 2, 3, 1)).astype(BF16),
        w_out=w_out.astype(BF16), w_xq=w_xq.astype(BF16),
        w_xkv=jnp.concatenate([w_xk, w_xv], axis=2).astype(BF16),
        w_xo=w_xo.astype(BF16), w_gate=w_gate.astype(BF16), w_up=w_up.astype(BF16), w_down=w_down.astype(BF16))
    for name, v in vectors.items():
        P[name] = v.reshape(depth, 1, v.shape[-1]) if v.ndim == 2 else v
    return P


def kernel(x_prompt, x_sample, mem_prompt, cache_fox_k, cache_fox_v, cache_fox_logf, state_ml_C, state_ml_n, state_ml_m, state_ml_conv, state_ffn_conv, cache_mem_k, cache_mem_v, page_table, g_pre_mix, w_in, b_fox_f, w_ml_conv, b_ml_conv, b_ml_i, b_ml_f, g_ml_head, w_out, g_post_mix, g_pre_x, g_mem, w_xq, w_xk, w_xv, w_xo, g_post_x, g_pre_ffn, w_gate, w_up, w_ffn_conv, b_ffn_conv, w_down, g_post_ffn):
    B, S, D = x_prompt.shape
    Bd, Ld, _ = x_sample.shape
    depth = w_in.shape[0]
    n_mem = mem_prompt.shape[1]
    d_ff = w_gate.shape[2]
    pool = cache_fox_k.shape[1]
    pad = GROUP - Ld
    assert Ld == GROUP // 2 and S % 256 == 0

    xp = x_prompt.reshape(B * S, D)
    xs = jnp.pad(x_sample, ((0, 0), (pad, 0), (0, 0))).reshape(Bd * GROUP, D)
    memx = mem_prompt.reshape(B * n_mem, D)
    ck = jnp.transpose(cache_fox_k, (0, 1, 3, 4, 2)).reshape(depth, pool, FOX_W, PAGE)
    cv = jnp.transpose(cache_fox_v, (0, 1, 3, 4, 2)).reshape(depth, pool, FOX_W, PAGE)
    clf = jnp.transpose(cache_fox_logf, (0, 1, 3, 2))
    ffn_tn = d_ff // 2 if (d_ff // 2) % LANES == 0 else d_ff

    P = _prep_params(
        w_in, b_fox_f, b_ml_i, b_ml_f, w_out, w_xq, w_xk, w_xv, w_xo, w_gate, w_up, w_down,
        dict(g_pre_mix=g_pre_mix, g_mem=g_mem, w_ml_conv=w_ml_conv, b_ml_conv=b_ml_conv, g_ml_head=g_ml_head,
             g_post_mix=g_post_mix, g_pre_x=g_pre_x, g_post_x=g_post_x, g_pre_ffn=g_pre_ffn,
             w_ffn_conv=w_ffn_conv, b_ffn_conv=b_ffn_conv, g_post_ffn=g_post_ffn))
    inj_ml = jnp.pad(state_ml_conv, ((0, 0), (0, 0), (1, GROUP - ML_CONV), (0, 0))).reshape(depth, Bd * GROUP, 2 * ML_W)
    inj_ffn = jnp.pad(state_ffn_conv, ((0, 0), (0, 0), (GROUP // 2 - (FFN_CONV - 1), GROUP // 2), (0, 0)))
    inj_ffn = inj_ffn.reshape(depth, Bd * GROUP, d_ff)
    nm0 = jnp.concatenate(
        [state_ml_n[:, :, :, None, :],
         jnp.broadcast_to(state_ml_m[:, :, :, None, None], (depth, Bd, ML_H, 1, ML_DH)),
         jnp.zeros((depth, Bd, ML_H, 6, ML_DH), F32)], axis=3)
    smk = cache_mem_k.reshape(depth, Bd, n_mem * X_H, X_DH)
    smv = cache_mem_v.reshape(depth, Bd, n_mem * X_H, X_DH)

    outs = {k: [] for k in ("lf_p", "fk_s", "fv_s", "lf_s", "nm_p", "cv_p", "nm_s", "cv_s", "fc_p", "fc_s")}
    kv_p = mem_p = c_p = c_s = None
    for l in range(depth):
        mem_p = _mem_proj(memx, l, P, B, prev=mem_p)
        z3, zs, *kv_p = _norm_proj(xp, l, P, _IFK, FOX_W, rows_per_batch=S, t_prev=kv_p)
        ft = _fox_cum(zs, B, S)
        fo = _fox_prompt(z3, kv_p[0], kv_p[1], ft, B, S, _IFQ, l)
        hm, c_p, nm_new = _mlstm(z3, zs, l, P, B, S, _IQK, _IMV, _IMO, c_prev=c_p)
        xp = _mix_xattn(fo, hm, xp, l, P, mem_p[0], mem_p[1], S, min(512, S))
        xp, gate_tail = _convffn(xp, l, P, S, min(512, S), ffn_tn)
        outs["lf_p"].append(zs[:, :FOX_H].reshape(B, S, FOX_H))
        outs["nm_p"].append(nm_new)
        tails = z3.reshape(z3.shape[0], B, S, ML_W)[0:2, :, S - (ML_CONV - 1):, :]
        outs["cv_p"].append(jnp.transpose(tails, (1, 2, 0, 3)).reshape(B, ML_CONV - 1, 2 * ML_W))
        outs["fc_p"].append(gate_tail.reshape(B, -1, 8, d_ff)[:, -1, 8 - (FFN_CONV - 1):, :])

        z3, zs = _norm_proj(xs, l, P, _IFV + 1, FOX_W)
        fo = _fox_decode(z3, zs, ck, cv, clf, page_table, l, _IFQ, _IFK, _IFV)
        hm, c_s, nm_new = _mlstm(z3, zs, l, P, Bd, GROUP, _IQK, _IMV, _IMO,
                                 state=(inj_ml, state_ml_C, nm0), c_prev=c_s)
        xs = _mix_xattn(fo, hm, xs, l, P, smk, smv, GROUP, GROUP)
        xs, gate_full = _convffn(xs, l, P, Bd * GROUP, Bd * GROUP, ffn_tn, inj=inj_ffn)
        real = lambda a: a.reshape((Bd, GROUP) + a.shape[1:])[:, pad:]
        outs["fk_s"].append(real(z3[_IFK]).reshape(Bd, Ld, FOX_H, FOX_DH))
        outs["fv_s"].append(real(z3[_IFV]).reshape(Bd, Ld, FOX_H, FOX_DH))
        outs["lf_s"].append(real(zs)[:, :, :FOX_H])
        outs["nm_s"].append(nm_new)
        tails = z3.reshape(z3.shape[0], Bd, GROUP, ML_W)[0:2, :, GROUP - (ML_CONV - 1):, :]
        outs["cv_s"].append(jnp.transpose(tails, (1, 2, 0, 3)).reshape(Bd, ML_CONV - 1, 2 * ML_W))
        outs["fc_s"].append(gate_full.reshape(Bd, GROUP, d_ff)[:, GROUP - (FFN_CONV - 1):, :])

    st = lambda k: jnp.stack(outs[k], axis=0)
    y_p = xp.reshape(B, S, D)
    y_s = xs.reshape(Bd, GROUP, D)[:, pad:]
    to_tokens = lambda a: jnp.transpose(a.reshape(depth, B, FOX_H, FOX_DH, S), (0, 1, 4, 2, 3))
    nm_p, nm_s = st("nm_p"), st("nm_s")
    mem_leaf = lambda a: a.reshape(depth, B, n_mem, X_H, X_DH)
    return (y_p, y_s, to_tokens(kv_p[0]), to_tokens(kv_p[1]), st("lf_p"), st("fk_s"), st("fv_s"), st("lf_s"),
            c_p, nm_p[:, :, :, 0, :], nm_p[:, :, :, 1, 0], st("cv_p"),
            c_s, nm_s[:, :, :, 0, :], nm_s[:, :, :, 1, 0], st("cv_s"),
            st("fc_p"), st("fc_s"), mem_leaf(mem_p[0]), mem_leaf(mem_p[1]))
```

```python
import functools
import math

import jax
import jax.numpy as jnp
from jax import lax
from jax.experimental import pallas as pl
from jax.experimental.pallas import tpu as pltpu

FOX_H, FOX_DH = 8, 64
FOX_W = FOX_H * FOX_DH
ML_H, ML_DH = 4, 128
ML_W = ML_H * ML_DH
ML_CONV = 4
X_H, X_DH = 4, 128
X_W = X_H * X_DH
FFN_CONV = 3
EPS = 1e-6
PAGE = 128
NEG = -1e30
LOG2E = 1.4426950408889634
GROUP = 8
LANES = 128
VMEM_LIMIT = 56 * 1024 * 1024

F32 = jnp.float32
BF16 = jnp.bfloat16


def _cparams(sem):
    return pltpu.CompilerParams(dimension_semantics=sem, vmem_limit_bytes=VMEM_LIMIT)


def _split3(x):
    hi = x.astype(BF16)
    r1 = x - hi.astype(F32)
    mid = r1.astype(BF16)
    lo = (r1 - mid.astype(F32)).astype(BF16)
    return hi, mid, lo


def _dot(a, b):
    return jnp.dot(a, b, preferred_element_type=F32)


def _dot_nt(a, b):
    return lax.dot_general(a, b, (((1,), (1,)), ((), ())), preferred_element_type=F32)


def _dot_tn(a, b):
    return lax.dot_general(a, b, (((0,), (0,)), ((), ())), preferred_element_type=F32)


def _dot3_lhs(x, w):
    hi, mid, lo = _split3(x)
    return _dot(hi, w) + _dot(mid, w) + _dot(lo, w)


def _dot3_rhs(w, x):
    hi, mid, lo = _split3(x)
    return _dot(w, hi) + _dot(w, mid) + _dot(w, lo)


def _rms(x, g):
    return x * lax.rsqrt(jnp.mean(x * x, axis=-1, keepdims=True) + EPS) * g


def _log_sigmoid(x):
    return jnp.minimum(x, 0.0) - jnp.log1p(jnp.exp(-jnp.abs(x)))


def _sigmoid(x):
    return 1.0 / (1.0 + jnp.exp(-x))


def _gelu_tanh(x):
    c = 0.7978845608028654
    return x * (0.5 * (1.0 + jnp.tanh(c * (x + 0.044715 * (x * x * x)))))


def _slab_spec(l, depth, first, block, index_map):
    if first:
        return pl.BlockSpec((depth,) + block, lambda *a: (0,) + tuple(index_map(*a)))
    return pl.BlockSpec((1,) + block, lambda *a: (l,) + tuple(index_map(*a)))


def _store_slab(ref, slab, value):
    li, nl = slab
    for k in range(nl):
        ref[k] = value if k == li else jnp.zeros_like(value)


def _norm_proj_kernel(*refs, tn, nz, with_gates, nt, n_alias, slab):
    refs = list(refs)
    x_ref, g_ref, w_ref = refs[0:3]
    pos = 3
    if with_gates:
        ws_ref, bs_ref = refs[pos:pos + 2]
        pos += 2
    if nt:
        wt_ref = refs[pos]
        pos += 1
    pos += n_alias
    z_ref = refs[pos]
    pos += 1
    if with_gates:
        zs_ref = refs[pos]
        pos += 1
    t_refs = refs[pos:pos + nt]

    xn = _rms(x_ref[...], g_ref[...]).astype(BF16)
    if with_gates:
        zs = _dot(xn, ws_ref[...]) + bs_ref[...]
        lane = lax.broadcasted_iota(jnp.int32, zs.shape, 1)
        forget = (lane < FOX_H) | ((lane >= FOX_H + ML_H) & (lane < FOX_H + 2 * ML_H))
        zs_ref[...] = jnp.where(forget, _log_sigmoid(zs), zs)
    for j in range(nz):
        z_ref[j] = _dot(xn, w_ref[:, j * tn:(j + 1) * tn])
    for j in range(nt):
        _store_slab(t_refs[j], slab, _dot_nt(wt_ref[j], xn)[None])


def _layer_spec(l, *shape):
    return pl.BlockSpec((None,) + shape, lambda *_: (l,) + (0,) * len(shape))


def _norm_proj(x, l, P, nz, tn, rows_per_batch=None, t_prev=None, tm=512):
    M, K = x.shape
    tm = min(tm, M)
    depth = P["wbig"].shape[0]
    nt = 0 if rows_per_batch is None else P["wkv_t"].shape[1]
    n_alias = 0 if t_prev is None else nt
    aliases = {}
    in_specs = [
        pl.BlockSpec((tm, K), lambda i: (i, 0)),
        _layer_spec(l, 1, K),
        _layer_spec(l, K, nz * tn),
        _layer_spec(l, K, LANES),
        _layer_spec(l, 1, LANES),
    ]
    args = [x, P["g_pre_mix"], P["wbig"], P["wsmall"], P["bsmall"]]
    if nt:
        in_specs.append(_layer_spec(l, nt, tn, K))
        args.append(P["wkv_t"])
    out_shape = [jax.ShapeDtypeStruct((nz, M, tn), F32), jax.ShapeDtypeStruct((M, LANES), F32)]
    out_specs = [pl.BlockSpec((nz, tm, tn), lambda i: (0, i, 0)), pl.BlockSpec((tm, LANES), lambda i: (i, 0))]
    if nt:
        tpb = rows_per_batch // tm
        for j in range(nt):
            if t_prev is not None:
                in_specs.append(pl.BlockSpec(memory_space=pl.ANY))
                args.append(t_prev[j])
                aliases[len(args) - 1] = len(out_shape)
            out_shape.append(jax.ShapeDtypeStruct((depth, M // rows_per_batch, tn, rows_per_batch), F32))
            out_specs.append(_slab_spec(l, depth, t_prev is None, (1, tn, tm), lambda i: (i // tpb, 0, i % tpb)))
    slab = (l, depth) if t_prev is None else (0, 1)
    return pl.pallas_call(
        functools.partial(_norm_proj_kernel, tn=tn, nz=nz, with_gates=True, nt=nt, n_alias=n_alias, slab=slab),
        grid=(M // tm,),
        in_specs=in_specs,
        out_specs=out_specs,
        out_shape=out_shape,
        input_output_aliases=aliases,
        compiler_params=_cparams(("arbitrary",)),
        name="norm_proj",
    )(*args)


def _mem_proj_kernel(x_ref, g_ref, w_ref, *refs, slab):
    k_ref, v_ref = refs[-2:]
    li, nl = slab
    xn = _rms(x_ref[...], g_ref[...]).astype(BF16)
    n = x_ref.shape[0]
    for j, o_ref in enumerate((k_ref, v_ref)):
        for k in range(nl):
            if k != li:
                o_ref[k] = jnp.zeros(o_ref.shape[1:], F32)
        for h in range(X_H):
            col = j * X_W + h * X_DH
            o_ref[li, 0, pl.ds(h, n, stride=X_H), :] = _dot(xn, w_ref[:, col:col + X_DH])


def _mem_proj(x, l, P, B, prev=None):
    M, K = x.shape
    n_mem = M // B
    depth = P["w_xkv"].shape[0]
    in_specs = [pl.BlockSpec((n_mem, K), lambda b: (b, 0)), _layer_spec(l, 1, K), _layer_spec(l, K, 2 * X_W)]
    args = [x, P["g_mem"], P["w_xkv"]]
    aliases = {}
    if prev is not None:
        for j in range(2):
            in_specs.append(pl.BlockSpec(memory_space=pl.ANY))
            args.append(prev[j])
            aliases[len(args) - 1] = j
    shape = jax.ShapeDtypeStruct((depth, B, n_mem * X_H, X_DH), F32)
    spec = _slab_spec(l, depth, prev is None, (1, n_mem * X_H, X_DH), lambda b: (b, 0, 0))
    return pl.pallas_call(
        functools.partial(_mem_proj_kernel, slab=(l, depth) if prev is None else (0, 1)),
        grid=(B,),
        in_specs=in_specs,
        out_specs=[spec, spec],
        out_shape=[shape, shape],
        input_output_aliases=aliases,
        compiler_params=_cparams(("arbitrary",)),
        name="mem_proj",
    )(*args)


def _tri_incl(n, upper):
    r = lax.broadcasted_iota(jnp.int32, (n, n), 0)
    c = lax.broadcasted_iota(jnp.int32, (n, n), 1)
    return ((r <= c) if upper else (c <= r)).astype(BF16)


BIAS_ROWS = 16


def _fox_cum_kernel(zs_ref, ft_ref, *, nb):
    n = nb * FOX_H
    lf = jnp.concatenate([zs_ref[u * LANES:(u + 1) * LANES, :].T[0:FOX_H, :] for u in range(nb)], axis=0)
    cum = _dot3_lhs(lf, _tri_incl(LANES, True))
    r = lax.broadcasted_iota(jnp.int32, (n, n), 0)
    c = lax.broadcasted_iota(jnp.int32, (n, n), 1)
    earlier = ((c % FOX_H == r % FOX_H) & (c // FOX_H < r // FOX_H)).astype(BF16)
    prefix = _dot3_rhs(earlier, jnp.broadcast_to(cum[:, LANES - 1:LANES], (n, LANES)))
    pieces = _split3((cum + prefix) * (-LOG2E))
    ro = lax.broadcasted_iota(jnp.int32, (nb * (FOX_H // 2) * BIAS_ROWS, n), 0)
    co = lax.broadcasted_iota(jnp.int32, (nb * (FOX_H // 2) * BIAS_ROWS, n), 1)
    blk = ro // ((FOX_H // 2) * BIAS_ROWS) == co // FOX_H
    pair = (ro // BIAS_ROWS) % (FOX_H // 2)
    slot = ro % BIAS_ROWS
    head = co % FOX_H
    out = None
    for idx, piece in enumerate(pieces):
        sel = blk & (((slot == idx) & (head == 2 * pair)) | ((slot == 3 + idx) & (head == 2 * pair + 1)))
        term = _dot(sel.astype(BF16), piece)
        out = term if out is None else out + term
    ft_ref[0] = out.reshape(nb, FOX_H // 2, BIAS_ROWS, LANES)


def _fox_cum(zs, B, S):
    nb = S // LANES
    return pl.pallas_call(
        functools.partial(_fox_cum_kernel, nb=nb),
        grid=(B,),
        in_specs=[pl.BlockSpec((S, LANES), lambda b: (b, 0))],
        out_specs=pl.BlockSpec((1, nb, FOX_H // 2, BIAS_ROWS, LANES), lambda b: (b, 0, 0, 0, 0)),
        out_shape=jax.ShapeDtypeStruct((B, nb, FOX_H // 2, BIAS_ROWS, LANES), F32),
        compiler_params=_cparams(("arbitrary",)),
        name="fox_cum",
    )(zs)


def _fox_prompt_kernel(ti_ref, tj_ref, q_ref, kt_ref, vt_ref, ft_ref, o_ref, m_s, l_s, acc_s, *, tq):
    t = pl.program_id(1)
    i = ti_ref[t]
    j = tj_ref[t]
    sub = tq // LANES
    npair = FOX_H // 2
    low = lax.broadcasted_iota(jnp.int32, (tq, LANES), 1) < FOX_DH

    @pl.when(j == 0)
    def _():
        m_s[...] = jnp.full(m_s.shape, NEG, F32)
        l_s[...] = jnp.zeros(l_s.shape, F32)
        acc_s[...] = jnp.zeros(acc_s.shape, F32)

    def update(masked):
        if masked:
            causal = (lax.broadcasted_iota(jnp.int32, (tq, tq), 1)
                      <= lax.broadcasted_iota(jnp.int32, (tq, tq), 0))
        for p in range(npair):
            rows = slice(p * LANES, (p + 1) * LANES)
            qp = q_ref[0, :, rows] * (FOX_DH ** -0.5 * LOG2E)
            bias = jnp.concatenate([ft_ref[0, u, p] for u in range(sub)], axis=1).astype(BF16)
            ka = jnp.concatenate(
                [kt_ref[rows, :].astype(BF16), bias, jnp.zeros((LANES - BIAS_ROWS, tq), BF16)], axis=0)
            vb = vt_ref[rows, :].astype(BF16)
            for e in range(2):
                h = 2 * p + e
                qh = jnp.where(low if e == 0 else jnp.logical_not(low), qp, 0.0)
                ones = ((bias_lane >= 3 * e) & (bias_lane < 3 * e + 3)).astype(F32)
                qa = jnp.concatenate([qh, ones], axis=1).astype(BF16)
                s = _dot(qa, ka)
                if masked:
                    s = jnp.where(causal, s, NEG)
                sc = [s[:, c * LANES:(c + 1) * LANES] for c in range(sub)]
                mx = functools.reduce(jnp.maximum, sc)
                m_prev = m_s[h]
                m_new = jnp.maximum(m_prev, jnp.max(mx, axis=1, keepdims=True))
                alpha = jnp.exp2(m_prev - m_new)
                ps = [jnp.exp2(c - m_new) for c in sc]
                l_s[h] = alpha * l_s[h] + jnp.sum(functools.reduce(jnp.add, ps), axis=1, keepdims=True)
                pr = jnp.concatenate([c.astype(BF16) for c in ps], axis=1)
                acc_s[h] = alpha * acc_s[h] + _dot_nt(pr, vb)
                m_s[h] = m_new

    bias_lane = lax.broadcasted_iota(jnp.int32, (tq, LANES), 1)

    @pl.when(j < i)
    def _():
        update(False)

    @pl.when(j == i)
    def _():
        update(True)
        for p in range(npair):
            oe = acc_s[2 * p] / l_s[2 * p]
            oo = acc_s[2 * p + 1] / l_s[2 * p + 1]
            o_ref[:, p * LANES:(p + 1) * LANES] = jnp.where(low, oe, oo).astype(o_ref.dtype)


def _fox_prompt(z3, kt, vt, ft, B, S, iq, layer, tq=512):
    tq = min(tq, S)
    nq = S // tq
    M = B * S
    sub = tq // LANES
    pairs = [(i, j) for i in range(nq) for j in range(i + 1)]
    ti = jnp.asarray([p[0] for p in pairs], jnp.int32)
    tj = jnp.asarray([p[1] for p in pairs], jnp.int32)
    grid_spec = pltpu.PrefetchScalarGridSpec(
        num_scalar_prefetch=2,
        grid=(B, len(pairs)),
        in_specs=[
            pl.BlockSpec((1, tq, FOX_W), lambda b, t, ti, tj: (iq, b * nq + ti[t], 0)),
            pl.BlockSpec((None, None, FOX_W, tq), lambda b, t, ti, tj: (layer, b, 0, tj[t])),
            pl.BlockSpec((None, None, FOX_W, tq), lambda b, t, ti, tj: (layer, b, 0, tj[t])),
            pl.BlockSpec((1, sub, FOX_H // 2, BIAS_ROWS, LANES), lambda b, t, ti, tj: (b, tj[t], 0, 0, 0)),
        ],
        out_specs=pl.BlockSpec((tq, FOX_W), lambda b, t, ti, tj: (b * nq + ti[t], 0)),
        scratch_shapes=[
            pltpu.VMEM((FOX_H, tq, LANES), F32),
            pltpu.VMEM((FOX_H, tq, LANES), F32),
            pltpu.VMEM((FOX_H, tq, LANES), F32),
        ],
    )
    return pl.pallas_call(
        functools.partial(_fox_prompt_kernel, tq=tq),
        grid_spec=grid_spec,
        out_shape=jax.ShapeDtypeStruct((M, FOX_W), BF16),
        compiler_params=_cparams(("arbitrary", "arbitrary")),
        name="fox_prompt",
    )(ti, tj, z3, kt, vt, ft)


SKIP_LOG = -110.0
KEY_SLOTS = 3


def _fox_decode_kernel(pt_ref, q_ref, kn_ref, vn_ref, zs_ref, lf_hbm, k_hbm, v_hbm, o_ref,
                       kbuf, vbuf, lfbuf, sbuf, m_ref, l_ref, acc_ref, need_ref, ksem, vsem, lfsem,
                       *, layer, npg, ng):
    b = pl.program_id(0)
    nreal = GROUP // 2
    rows = nreal * FOX_H
    n = npg * FOX_H

    def copies(hbm, buf, sem, g, slot):
        return [pltpu.make_async_copy(hbm.at[layer, pt_ref[b, g * npg + i]], buf.at[slot, i], sem.at[slot])
                for i in range(npg)]

    def start_keys(g, slot):
        for c in copies(lf_hbm, lfbuf, lfsem, g, slot) + copies(k_hbm, kbuf, ksem, g, slot):
            c.start()

    order = [ng - 1 - it for it in range(ng)]
    for it in range(min(KEY_SLOTS - 1, ng)):
        start_keys(order[it], it)

    hrow = lax.broadcasted_iota(jnp.int32, (FOX_H, FOX_W), 0)
    hcol = lax.broadcasted_iota(jnp.int32, (FOX_H, FOX_W), 1) // FOX_DH
    head_sel = jnp.concatenate([hrow == hcol] * nreal, axis=0)
    q = q_ref[...] * (FOX_DH ** -0.5)
    qbd = jnp.concatenate(
        [jnp.broadcast_to(q[nreal + t:nreal + t + 1, :], (FOX_H, FOX_W)) for t in range(nreal)], axis=0)
    qbd = jnp.where(head_sel, qbd, 0.0).astype(BF16)
    eye = (lax.broadcasted_iota(jnp.int32, (rows, rows), 0)
           == lax.broadcasted_iota(jnp.int32, (rows, rows), 1))

    def as_row(col):
        return jnp.sum(jnp.where(eye, jnp.broadcast_to(col, (rows, rows)), 0.0), axis=0, keepdims=True)

    zpad = jnp.zeros((LANES - GROUP, FOX_W), F32)
    kn = jnp.concatenate([kn_ref[0], zpad], axis=0).astype(BF16)
    vn_t = jnp.concatenate([vn_ref[0], zpad], axis=0).T.astype(BF16)
    zs = jnp.concatenate([zs_ref[...], jnp.zeros((LANES - GROUP, LANES), F32)], axis=0)
    lf_new = zs.T[0:FOX_H, :]
    kr = lax.broadcasted_iota(jnp.int32, (LANES, LANES), 0)
    kc = lax.broadcasted_iota(jnp.int32, (LANES, LANES), 1)
    inc = ((kr <= kc) & (kr >= nreal) & (kr < GROUP)).astype(BF16)
    f_new = jnp.concatenate([_dot3_lhs(lf_new, inc)] * nreal, axis=0)
    qi = lax.broadcasted_iota(jnp.int32, (rows, LANES), 0) // FOX_H
    kj = lax.broadcasted_iota(jnp.int32, (rows, LANES), 1) - nreal
    s = jnp.where((kj >= 0) & (kj <= qi), _dot_nt(qbd, kn) - f_new, NEG)
    m0 = jnp.max(s, axis=1, keepdims=True)
    p0 = jnp.exp(s - m0)
    m_ref[...] = m0
    l_ref[...] = jnp.sum(p0, axis=1, keepdims=True)
    acc_ref[...] = _dot_nt(vn_t, p0.astype(BF16))

    er = lax.broadcasted_iota(jnp.int32, (n, n), 0)
    ec = lax.broadcasted_iota(jnp.int32, (n, n), 1)
    later_pages = ((ec % FOX_H == er % FOX_H) & (ec // FOX_H > er // FOX_H)).astype(BF16)

    def update(g, slot):
        @pl.when(need_ref[slot] == 1)
        def _():
            for c in copies(v_hbm, vbuf, vsem, g, slot):
                c.wait()
            vcat = jnp.concatenate([vbuf[slot, i].astype(BF16) for i in range(npg)], axis=1)
            sg = sbuf[slot]
            m = m_ref[...]
            m_new = jnp.maximum(m, jnp.max(sg, axis=1, keepdims=True))
            alpha = jnp.exp(m - m_new)
            pr = jnp.exp(sg - m_new)
            l_ref[...] = alpha * l_ref[...] + jnp.sum(pr, axis=1, keepdims=True)
            acc_ref[...] = as_row(alpha) * acc_ref[...] + _dot_nt(vcat, pr.astype(BF16))
            m_ref[...] = m_new

    carry = jnp.zeros((FOX_H, 1), F32)
    for it, g in enumerate(order):
        slot = it % 2
        kslot = it % KEY_SLOTS
        if it + KEY_SLOTS - 1 < ng:
            start_keys(order[it + KEY_SLOTS - 1], (it + KEY_SLOTS - 1) % KEY_SLOTS)
        for c in copies(lf_hbm, lfbuf, lfsem, g, kslot):
            c.wait()
        lf = jnp.concatenate([lfbuf[kslot, i] for i in range(npg)], axis=0)
        incl = _dot3_lhs(lf, _tri_incl(LANES, False))
        later = _dot3_rhs(later_pages, jnp.broadcast_to(incl[:, 0:1], (n, LANES)))
        after = incl - lf + later + jnp.concatenate([jnp.broadcast_to(carry, (FOX_H, LANES))] * npg, axis=0)
        carry = after[0:FOX_H, 0:1] + lf[0:FOX_H, 0:1]
        gcat = jnp.concatenate([after[i * FOX_H:(i + 1) * FOX_H, :] for i in range(npg)], axis=1)
        for c in copies(k_hbm, kbuf, ksem, g, kslot):
            c.wait()
        kcat = jnp.concatenate([kbuf[kslot, i].astype(BF16) for i in range(npg)], axis=1)
        sg = _dot(qbd, kcat) + jnp.concatenate([gcat] * nreal, axis=0)
        sbuf[slot] = sg
        need = jnp.max(jnp.max(sg, axis=1, keepdims=True) - m_ref[...]) > SKIP_LOG
        need_ref[slot] = need.astype(jnp.int32)

        @pl.when(need)
        def _():
            for c in copies(v_hbm, vbuf, vsem, g, slot):
                c.start()

        if it >= 1:
            update(order[it - 1], 1 - slot)
    update(order[ng - 1], (ng - 1) % 2)

    out_t = acc_ref[...] / as_row(l_ref[...])
    own = (lax.broadcasted_iota(jnp.int32, (FOX_W, rows), 0) // FOX_DH
           == lax.broadcasted_iota(jnp.int32, (FOX_W, rows), 1) % FOX_H)
    out_t = jnp.where(own, out_t, 0.0).astype(BF16)
    gr = lax.broadcasted_iota(jnp.int32, (GROUP, rows), 0) - nreal
    gc = lax.broadcasted_iota(jnp.int32, (GROUP, rows), 1) // FOX_H
    o_ref[...] = _dot_nt((gr == gc).astype(BF16), out_t)


def _fox_decode(z3, zs, cache_k, cache_v, cache_lf, page_table, layer, iq, ik, iv, npg=16):
    Bd, n_pages = page_table.shape
    npg = min(npg, n_pages)
    ng = n_pages // npg
    rows = (GROUP // 2) * FOX_H
    any_spec = pl.BlockSpec(memory_space=pl.ANY)
    grid_spec = pltpu.PrefetchScalarGridSpec(
        num_scalar_prefetch=1,
        grid=(Bd,),
        in_specs=[
            pl.BlockSpec((None, GROUP, FOX_W), lambda b, pt: (iq, b, 0)),
            pl.BlockSpec((1, GROUP, FOX_W), lambda b, pt: (ik, b, 0)),
            pl.BlockSpec((1, GROUP, FOX_W), lambda b, pt: (iv, b, 0)),
            pl.BlockSpec((GROUP, LANES), lambda b, pt: (b, 0)),
            any_spec, any_spec, any_spec,
        ],
        out_specs=pl.BlockSpec((GROUP, FOX_W), lambda b, pt: (b, 0)),
        scratch_shapes=[
            pltpu.VMEM((KEY_SLOTS, npg, FOX_W, PAGE), F32),
            pltpu.VMEM((2, npg, FOX_W, PAGE), F32),
            pltpu.VMEM((KEY_SLOTS, npg, FOX_H, PAGE), F32),
            pltpu.VMEM((2, rows, npg * PAGE), F32),
            pltpu.VMEM((rows, 1), F32),
            pltpu.VMEM((rows, 1), F32),
            pltpu.VMEM((FOX_W, rows), F32),
            pltpu.SMEM((2,), jnp.int32),
            pltpu.SemaphoreType.DMA((KEY_SLOTS,)),
            pltpu.SemaphoreType.DMA((2,)),
            pltpu.SemaphoreType.DMA((KEY_SLOTS,)),
        ],
    )
    return pl.pallas_call(
        functools.partial(_fox_decode_kernel, layer=layer, npg=npg, ng=ng),
        grid_spec=grid_spec,
        out_shape=jax.ShapeDtypeStruct((Bd * GROUP, FOX_W), F32),
        compiler_params=_cparams(("arbitrary",)),
        name="fox_decode",
    )(page_table, z3, z3, z3, zs, cache_lf, cache_k, cache_v)


def _mlstm_kernel(*refs, rin, sample, slab):
    qk_ref, v_ref, og_ref, zs_ref, wc_ref, bc_ref, gh_ref = refs[0:7]
    if sample:
        inj_ref, c0_ref, nm0_ref = refs[7:10]
    h_ref, c_out_ref, nm_out_ref, xbuf, c_s, nm_s = refs[-6:]
    t = pl.program_id(1)
    rc = LANES

    @pl.when(t == 0)
    def _():
        xbuf[0:8, :] = jnp.zeros((8, 2 * ML_W), F32)
        if sample:
            c_s[...] = c0_ref[0]
            nm_s[...] = nm0_ref[0]
        else:
            c_s[...] = jnp.zeros(c_s.shape, F32)
            nm_s[...] = jnp.zeros(nm_s.shape, F32)

    x = jnp.concatenate([qk_ref[0], qk_ref[1]], axis=1)
    if sample:
        slot = lax.broadcasted_iota(jnp.int32, (rin, 1), 0)
        x = jnp.where((slot >= 1) & (slot < GROUP // 2), inj_ref[...], x)
    xbuf[8:8 + rin, :] = x
    y = bc_ref[...] + wc_ref[3:4, :] * x
    for jj in range(ML_CONV - 1):
        y = y + wc_ref[jj:jj + 1, :] * xbuf[pl.ds(8 - (ML_CONV - 1) + jj, rin), :]
    xbuf[0:8, :] = xbuf[rin:rin + 8, :]
    y = y * _sigmoid(y)
    q_all = y[:, 0:ML_W]
    k_all = y[:, ML_W:] * (ML_DH ** -0.5)
    v_all = v_ref[0]
    zs = zs_ref[...]
    if rin < rc:
        pad = jnp.zeros((rc - rin, ML_W), F32)
        q_all = jnp.concatenate([q_all, pad], axis=0)
        k_all = jnp.concatenate([k_all, pad], axis=0)
        v_all = jnp.concatenate([v_all, pad], axis=0)
        zs = jnp.concatenate([zs, jnp.zeros((rc - rin, LANES), F32)], axis=0)
    if sample:
        row = lax.broadcasted_iota(jnp.int32, (rc, LANES), 0)
        lane = lax.broadcasted_iota(jnp.int32, (rc, LANES), 1)
        padrow = (row < GROUP // 2) | (row >= GROUP)
        is_i = (lane >= FOX_H) & (lane < FOX_H + ML_H)
        zs = jnp.where(padrow, jnp.where(is_i, NEG, 0.0), zs)

    colcum = _dot3_rhs(_tri_incl(rc, False), zs)
    rowraw = zs.T[FOX_H:FOX_H + 2 * ML_H, :]
    rowcum = _dot3_lhs(rowraw, _tri_incl(rc, True))
    causal = (lax.broadcasted_iota(jnp.int32, (rc, rc), 1) <= lax.broadcasted_iota(jnp.int32, (rc, rc), 0))

    heads = []
    for h in range(ML_H):
        cs = slice(h * ML_DH, (h + 1) * ML_DH)
        qf = q_all[:, cs]
        kf = k_all[:, cs]
        qh = qf.astype(BF16)
        kh = kf.astype(BF16)
        vh = v_all[:, cs].astype(BF16)
        icol = jnp.broadcast_to(zs[:, FOX_H + h:FOX_H + h + 1], (rc, LANES))
        bcol = jnp.broadcast_to(colcum[:, FOX_H + ML_H + h:FOX_H + ML_H + h + 1], (rc, LANES))
        irow = rowraw[h:h + 1, :]
        brow = rowcum[ML_H + h:ML_H + h + 1, :]
        cmat = c_s[h]
        nrow = nm_s[h, 0:1, :]
        m = nm_s[h, 1:2, :]

        dlog = jnp.where(causal, bcol - brow + irow, NEG)
        g = bcol + m
        mt = jnp.maximum(g, jnp.max(dlog, axis=1, keepdims=True))
        w_intra = jnp.exp(dlog - mt)
        w_inter = jnp.exp(g - mt)
        sc = _dot_nt(qh, kh) * w_intra
        num = _dot(sc.astype(BF16), vh) + _dot(qh, cmat.astype(BF16)) * w_inter
        den = jnp.sum(sc, axis=1, keepdims=True) + w_inter * jnp.sum(qf * nrow, axis=1, keepdims=True)
        hh = num / jnp.maximum(jnp.abs(den), jnp.exp(-mt))

        b_last = bcol[rc - 1:rc, :]
        a = b_last - bcol + icol
        m_new = jnp.maximum(b_last + m, jnp.max(a, axis=0, keepdims=True))
        ws = jnp.exp(a - m_new)
        decay = jnp.exp(b_last + m - m_new)
        kw = kf * ws
        c_s[h] = decay * cmat + _dot_tn(kw.astype(BF16), vh)
        nm_s[h, 0:1, :] = decay * nrow + jnp.sum(kw, axis=0, keepdims=True)
        nm_s[h, 1:2, :] = m_new

        heads.append(_rms(hh, gh_ref[:, cs]))
    hfull = jnp.concatenate(heads, axis=1)
    hfull = hfull[0:rin] * _sigmoid(og_ref[0])
    h_ref[...] = hfull.astype(h_ref.dtype)

    @pl.when(t == pl.num_programs(1) - 1)
    def _():
        _store_slab(c_out_ref, slab, c_s[...][None])
        nm_out_ref[0] = nm_s[...]


def _mlstm(z3, zs, l, P, B, L, iqk, iv, io, state=None, c_prev=None):
    sample = state is not None
    rin = GROUP if sample else LANES
    nt = L // rin
    M = B * L
    depth = P["w_ml_conv"].shape[0]
    in_specs = [
        pl.BlockSpec((2, rin, ML_W), lambda b, t: (iqk, b * nt + t, 0)),
        pl.BlockSpec((1, rin, ML_W), lambda b, t: (iv, b * nt + t, 0)),
        pl.BlockSpec((1, rin, ML_W), lambda b, t: (io, b * nt + t, 0)),
        pl.BlockSpec((rin, LANES), lambda b, t: (b * nt + t, 0)),
        _layer_spec(l, ML_CONV, 2 * ML_W),
        _layer_spec(l, 1, 2 * ML_W),
        _layer_spec(l, 1, ML_W),
    ]
    args = [z3, z3, z3, zs, P["w_ml_conv"], P["b_ml_conv"], P["g_ml_head"]]
    if sample:
        in_specs += [
            pl.BlockSpec((None, rin, 2 * ML_W), lambda b, t: (l, b * nt + t, 0)),
            pl.BlockSpec((None, 1, ML_H, ML_DH, ML_DH), lambda b, t: (l, b, 0, 0, 0)),
            pl.BlockSpec((None, 1, ML_H, 8, ML_DH), lambda b, t: (l, b, 0, 0, 0)),
        ]
        args += list(state)
    aliases = {}
    if c_prev is not None:
        in_specs.append(pl.BlockSpec(memory_space=pl.ANY))
        args.append(c_prev)
        aliases[len(args) - 1] = 1
    return pl.pallas_call(
        functools.partial(_mlstm_kernel, rin=rin, sample=sample,
                          slab=(l, depth) if c_prev is None else (0, 1)),
        grid=(B, nt),
        in_specs=in_specs,
        out_specs=[
            pl.BlockSpec((rin, ML_W), lambda b, t: (b * nt + t, 0)),
            _slab_spec(l, depth, c_prev is None, (1, ML_H, ML_DH, ML_DH), lambda b, t: (b, 0, 0, 0)),
            pl.BlockSpec((1, ML_H, 8, ML_DH), lambda b, t: (b, 0, 0, 0)),
        ],
        out_shape=[
            jax.ShapeDtypeStruct((M, ML_W), F32 if sample else BF16),
            jax.ShapeDtypeStruct((depth, B, ML_H, ML_DH, ML_DH), F32),
            jax.ShapeDtypeStruct((B, ML_H, 8, ML_DH), F32),
        ],
        input_output_aliases=aliases,
        scratch_shapes=[
            pltpu.VMEM((rin + 8, 2 * ML_W), F32),
            pltpu.VMEM((ML_H, ML_DH, ML_DH), F32),
            pltpu.VMEM((ML_H, 8, ML_DH), F32),
        ],
        compiler_params=_cparams(("arbitrary", "arbitrary")),
        name="mlstm_sample" if sample else "mlstm_prompt",
    )(*args)


def _mix_xattn_kernel(fo_ref, hm_ref, x_ref, wo_ref, gpm_ref, gpx_ref, wq_ref, mk_ref, mv_ref, wxo_ref, gox_ref, o_ref):
    mix = (_dot(fo_ref[...].astype(BF16), wo_ref[0:FOX_W, :])
           + _dot(hm_ref[...].astype(BF16), wo_ref[FOX_W:, :]))
    x1 = x_ref[...] + _rms(mix, gpm_ref[...])
    hq = _rms(x1, gpx_ref[...]).astype(BF16)
    xq = _dot(hq, wq_ref[...])
    nb = mk_ref.shape[0]
    rt = xq.shape[0] // nb
    n_mem = mk_ref.shape[1] // X_H
    parts = []
    for j in range(nb):
        heads = []
        for h in range(X_H):
            cs = slice(h * X_DH, (h + 1) * X_DH)
            mk = mk_ref[j, pl.ds(h, n_mem, stride=X_H), :].astype(BF16)
            mv = mv_ref[j, pl.ds(h, n_mem, stride=X_H), :].astype(BF16)
            s = _dot_nt(xq[j * rt:(j + 1) * rt, cs].astype(BF16), mk) * (X_DH ** -0.5)
            e = jnp.exp(s - jnp.max(s, axis=1, keepdims=True))
            heads.append(_dot(e.astype(BF16), mv) / jnp.sum(e, axis=1, keepdims=True))
        parts.append(jnp.concatenate(heads, axis=1))
    xo = _dot((parts[0] if nb == 1 else jnp.concatenate(parts, axis=0)).astype(BF16), wxo_ref[...])
    o_ref[...] = x1 + _rms(xo, gox_ref[...])


def _mix_xattn(fo, hm, x, l, P, mk, mv, rows_per_batch, tm):
    M, D = x.shape
    if tm <= rows_per_batch:
        tpb = rows_per_batch // tm
        mem_spec = pl.BlockSpec((None, 1) + mk.shape[2:], lambda i: (l, i // tpb, 0, 0))
    else:
        nb = tm // rows_per_batch
        mem_spec = pl.BlockSpec((None, nb) + mk.shape[2:], lambda i: (l, i, 0, 0))
    return pl.pallas_call(
        _mix_xattn_kernel,
        grid=(M // tm,),
        in_specs=[
            pl.BlockSpec((tm, FOX_W), lambda i: (i, 0)),
            pl.BlockSpec((tm, ML_W), lambda i: (i, 0)),
            pl.BlockSpec((tm, D), lambda i: (i, 0)),
            _layer_spec(l, FOX_W + ML_W, D),
            _layer_spec(l, 1, D), _layer_spec(l, 1, D),
            _layer_spec(l, D, X_W),
            mem_spec, mem_spec,
            _layer_spec(l, X_W, D),
            _layer_spec(l, 1, D),
        ],
        out_specs=pl.BlockSpec((tm, D), lambda i: (i, 0)),
        out_shape=jax.ShapeDtypeStruct((M, D), F32),
        compiler_params=_cparams(("arbitrary",)),
        name="mix_xattn",
    )(fo, hm, x, P["w_out"], P["g_post_mix"], P["g_pre_x"], P["w_xq"], mk, mv, P["w_xo"], P["g_post_x"])


def _convffn_kernel(*refs, tm, tiles_per_batch, sample):
    if sample:
        (x_ref, gpre_ref, wg_ref, wu_ref, wc_ref, bc_ref, wd_ref, gpost_ref, inj_ref,
         o_ref, gate_ref, xn_ref, acc_ref, gbuf, tails) = refs
    else:
        (x_ref, gpre_ref, wg_ref, wu_ref, wc_ref, bc_ref, wd_ref, gpost_ref,
         o_ref, gate_ref, xn_ref, acc_ref, gbuf, tails) = refs
    i = pl.program_id(0)
    j = pl.program_id(1)

    @pl.when(j == 0)
    def _():
        xn_ref[...] = _rms(x_ref[...], gpre_ref[...]).astype(BF16)
        acc_ref[...] = jnp.zeros_like(acc_ref)

    xn = xn_ref[...]
    g = _dot(xn, wg_ref[...])
    if sample:
        slot = lax.broadcasted_iota(jnp.int32, (tm, 1), 0) % GROUP
        g = jnp.where((slot >= GROUP // 2 - (FFN_CONV - 1)) & (slot < GROUP // 2), inj_ref[...], g)
        gate_ref[...] = g
        gbuf[0:8, :] = jnp.zeros((8, g.shape[1]), F32)
    else:
        first = (i % tiles_per_batch) == 0
        gbuf[0:8, :] = jnp.where(first, 0.0, tails[j])
        gate_ref[0] = g[tm - 8:tm, :]
        tails[j] = g[tm - 8:tm, :]
    gbuf[8:8 + tm, :] = g
    t = bc_ref[...] + wc_ref[2:3, :] * g
    for jj in range(FFN_CONV - 1):
        t = t + wc_ref[jj:jj + 1, :] * gbuf[pl.ds(8 - (FFN_CONV - 1) + jj, tm), :]
    a = _gelu_tanh(t) * _dot(xn, wu_ref[...])
    acc_ref[...] += _dot(a.astype(BF16), wd_ref[...])

    @pl.when(j == pl.num_programs(1) - 1)
    def _():
        o_ref[...] = x_ref[...] + _rms(acc_ref[...], gpost_ref[...])


def _convffn(x, l, P, rows_per_batch, tm, tn, inj=None):
    M, D = x.shape
    F = P["w_gate"].shape[2]
    sample = inj is not None
    nj = F // tn
    tiles_per_batch = max(rows_per_batch // tm, 1)
    in_specs = [
        pl.BlockSpec((tm, D), lambda i, j: (i, 0)),
        _layer_spec(l, 1, D),
        pl.BlockSpec((None, D, tn), lambda i, j: (l, 0, j)),
        pl.BlockSpec((None, D, tn), lambda i, j: (l, 0, j)),
        pl.BlockSpec((None, FFN_CONV, tn), lambda i, j: (l, 0, j)),
        pl.BlockSpec((None, 1, tn), lambda i, j: (l, 0, j)),
        pl.BlockSpec((None, tn, D), lambda i, j: (l, j, 0)),
        _layer_spec(l, 1, D),
    ]
    args = [x, P["g_pre_ffn"], P["w_gate"], P["w_up"], P["w_ffn_conv"], P["b_ffn_conv"], P["w_down"],
            P["g_post_ffn"]]
    if sample:
        in_specs.append(pl.BlockSpec((None, tm, tn), lambda i, j: (l, i, j)))
        args.append(inj)
        gate_shape = jax.ShapeDtypeStruct((M, F), F32)
        gate_spec = pl.BlockSpec((tm, tn), lambda i, j: (i, j))
    else:
        gate_shape = jax.ShapeDtypeStruct((M // tm, 8, F), F32)
        gate_spec = pl.BlockSpec((1, 8, tn), lambda i, j: (i, 0, j))
    return pl.pallas_call(
        functools.partial(_convffn_kernel, tm=tm, tiles_per_batch=tiles_per_batch, sample=sample),
        grid=(M // tm, nj),
        in_specs=in_specs,
        out_specs=[pl.BlockSpec((tm, D), lambda i, j: (i, 0)), gate_spec],
        out_shape=[jax.ShapeDtypeStruct((M, D), F32), gate_shape],
        scratch_shapes=[
            pltpu.VMEM((tm, D), BF16),
            pltpu.VMEM((tm, D), F32),
            pltpu.VMEM((tm + 8, tn), F32),
            pltpu.VMEM((nj, 8, tn), F32),
        ],
        compiler_params=_cparams(("arbitrary", "arbitrary")),
        name="convffn_sample" if sample else "convffn_prompt",
    )(*args)


_IQK, _IMV, _IMO, _IFQ, _IFK, _IFV = 0, 2, 3, 4, 5, 6


def _prep_params(w_in, b_fox_f, b_ml_i, b_ml_f, w_out, w_xq, w_xk, w_xv, w_xo, w_gate, w_up, w_down, vectors):
    depth, K, _ = w_in.shape
    o_ff = 3 * FOX_W
    o_mq = o_ff + FOX_H
    o_mi = o_mq + 3 * ML_W
    o_mo = o_mi + 2 * ML_H
    n_gate = FOX_H + 2 * ML_H
    P = dict(
        wbig=jnp.concatenate([w_in[:, :, o_mq:o_mi], w_in[:, :, o_mo:], w_in[:, :, :o_ff]], axis=2).astype(BF16),
        wsmall=jnp.concatenate([w_in[:, :, o_ff:o_mq], w_in[:, :, o_mi:o_mo],
                                jnp.zeros((depth, K, LANES - n_gate), F32)], axis=2).astype(BF16),
        bsmall=jnp.concatenate([b_fox_f, b_ml_i, b_ml_f, jnp.zeros((depth, LANES - n_gate), F32)],
                               axis=1).reshape(depth, 1, LANES),
        wkv_t=jnp.transpose(w_in[:, :, FOX_W:o_ff].reshape(depth, K, 2, FOX_W), (0, 2, 3, 1)).astype(BF16),
        w_out=w_out.astype(BF16), w_xq=w_xq.astype(BF16),
        w_xkv=jnp.concatenate([w_xk, w_xv], axis=2).astype(BF16),
        w_xo=w_xo.astype(BF16), w_gate=w_gate.astype(BF16), w_up=w_up.astype(BF16), w_down=w_down.astype(BF16))
    for name, v in vectors.items():
        P[name] = v.reshape(depth, 1, v.shape[-1]) if v.ndim == 2 else v
    return P


def kernel(x_prompt, x_sample, mem_prompt, cache_fox_k, cache_fox_v, cache_fox_logf, state_ml_C, state_ml_n, state_ml_m, state_ml_conv, state_ffn_conv, cache_mem_k, cache_mem_v, page_table, g_pre_mix, w_in, b_fox_f, w_ml_conv, b_ml_conv, b_ml_i, b_ml_f, g_ml_head, w_out, g_post_mix, g_pre_x, g_mem, w_xq, w_xk, w_xv, w_xo, g_post_x, g_pre_ffn, w_gate, w_up, w_ffn_conv, b_ffn_conv, w_down, g_post_ffn):
    B, S, D = x_prompt.shape
    Bd, Ld, _ = x_sample.shape
    depth = w_in.shape[0]
    n_mem = mem_prompt.shape[1]
    d_ff = w_gate.shape[2]
    pool = cache_fox_k.shape[1]
    pad = GROUP - Ld
    assert Ld == GROUP // 2 and S % 256 == 0

    xp = x_prompt.reshape(B * S, D)
    xs = jnp.pad(x_sample, ((0, 0), (pad, 0), (0, 0))).reshape(Bd * GROUP, D)
    memx = mem_prompt.reshape(B * n_mem, D)
    ck = jnp.transpose(cache_fox_k, (0, 1, 3, 4, 2)).reshape(depth, pool, FOX_W, PAGE)
    cv = jnp.transpose(cache_fox_v, (0, 1, 3, 4, 2)).reshape(depth, pool, FOX_W, PAGE)
    clf = jnp.transpose(cache_fox_logf, (0, 1, 3, 2))
    ffn_tn = d_ff // 2 if (d_ff // 2) % LANES == 0 else d_ff

    P = _prep_params(
        w_in, b_fox_f, b_ml_i, b_ml_f, w_out, w_xq, w_xk, w_xv, w_xo, w_gate, w_up, w_down,
        dict(g_pre_mix=g_pre_mix, g_mem=g_mem, w_ml_conv=w_ml_conv, b_ml_conv=b_ml_conv, g_ml_head=g_ml_head,
             g_post_mix=g_post_mix, g_pre_x=g_pre_x, g_post_x=g_post_x, g_pre_ffn=g_pre_ffn,
             w_ffn_conv=w_ffn_conv, b_ffn_conv=b_ffn_conv, g_post_ffn=g_post_ffn))
    inj_ml = jnp.pad(state_ml_conv, ((0, 0), (0, 0), (1, GROUP - ML_CONV), (0, 0))).reshape(depth, Bd * GROUP, 2 * ML_W)
    inj_ffn = jnp.pad(state_ffn_conv, ((0, 0), (0, 0), (GROUP // 2 - (FFN_CONV - 1), GROUP // 2), (0, 0)))
    inj_ffn = inj_ffn.reshape(depth, Bd * GROUP, d_ff)
    nm0 = jnp.concatenate(
        [state_ml_n[:, :, :, None, :],
         jnp.broadcast_to(state_ml_m[:, :, :, None, None], (depth, Bd, ML_H, 1, ML_DH)),
         jnp.zeros((depth, Bd, ML_H, 6, ML_DH), F32)], axis=3)
    smk = cache_mem_k.reshape(depth, Bd, n_mem * X_H, X_DH)
    smv = cache_mem_v.reshape(depth, Bd, n_mem * X_H, X_DH)

    outs = {k: [] for k in ("lf_p", "fk_s", "fv_s", "lf_s", "nm_p", "cv_p", "nm_s", "cv_s", "fc_p", "fc_s")}
    kv_p = mem_p = c_p = c_s = None
    for l in range(depth):
        mem_p = _mem_proj(memx, l, P, B, prev=mem_p)
        z3, zs, *kv_p = _norm_proj(xp, l, P, _IFK, FOX_W, rows_per_batch=S, t_prev=kv_p)
        ft = _fox_cum(zs, B, S)
        fo = _fox_prompt(z3, kv_p[0], kv_p[1], ft, B, S, _IFQ, l)
        hm, c_p, nm_new = _mlstm(z3, zs, l, P, B, S, _IQK, _IMV, _IMO, c_prev=c_p)
        xp = _mix_xattn(fo, hm, xp, l, P, mem_p[0], mem_p[1], S, min(512, S))
        xp, gate_tail = _convffn(xp, l, P, S, min(512, S), ffn_tn)
        outs["lf_p"].append(zs[:, :FOX_H].reshape(B, S, FOX_H))
        outs["nm_p"].append(nm_new)
        tails = z3.reshape(z3.shape[0], B, S, ML_W)[0:2, :, S - (ML_CONV - 1):, :]
        outs["cv_p"].append(jnp.transpose(tails, (1, 2, 0, 3)).reshape(B, ML_CONV - 1, 2 * ML_W))
        outs["fc_p"].append(gate_tail.reshape(B, -1, 8, d_ff)[:, -1, 8 - (FFN_CONV - 1):, :])

        z3, zs = _norm_proj(xs, l, P, _IFV + 1, FOX_W)
        fo = _fox_decode(z3, zs, ck, cv, clf, page_table, l, _IFQ, _IFK, _IFV)
        hm, c_s, nm_new = _mlstm(z3, zs, l, P, Bd, GROUP, _IQK, _IMV, _IMO,
                                 state=(inj_ml, state_ml_C, nm0), c_prev=c_s)
        xs = _mix_xattn(fo, hm, xs, l, P, smk, smv, GROUP, GROUP * math.gcd(Bd, 8))
        xs, gate_full = _convffn(xs, l, P, Bd * GROUP, Bd * GROUP, ffn_tn, inj=inj_ffn)
        real = lambda a: a.reshape((Bd, GROUP) + a.shape[1:])[:, pad:]
        outs["fk_s"].append(real(z3[_IFK]).reshape(Bd, Ld, FOX_H, FOX_DH))
        outs["fv_s"].append(real(z3[_IFV]).reshape(Bd, Ld, FOX_H, FOX_DH))
        outs["lf_s"].append(real(zs)[:, :, :FOX_H])
        outs["nm_s"].append(nm_new)
        tails = z3.reshape(z3.shape[0], Bd, GROUP, ML_W)[0:2, :, GROUP - (ML_CONV - 1):, :]
        outs["cv_s"].append(jnp.transpose(tails, (1, 2, 0, 3)).reshape(Bd, ML_CONV - 1, 2 * ML_W))
        outs["fc_s"].append(gate_full.reshape(Bd, GROUP, d_ff)[:, GROUP - (FFN_CONV - 1):, :])

    st = lambda k: jnp.stack(outs[k], axis=0)
    y_p = xp.reshape(B, S, D)
    y_s = xs.reshape(Bd, GROUP, D)[:, pad:]
    to_tokens = lambda a: jnp.transpose(a.reshape(depth, B, FOX_H, FOX_DH, S), (0, 1, 4, 2, 3))
    nm_p, nm_s = st("nm_p"), st("nm_s")
    mem_leaf = lambda a: a.reshape(depth, B, n_mem, X_H, X_DH)
    return (y_p, y_s, to_tokens(kv_p[0]), to_tokens(kv_p[1]), st("lf_p"), st("fk_s"), st("fv_s"), st("lf_s"),
            c_p, nm_p[:, :, :, 0, :], nm_p[:, :, :, 1, 0], st("cv_p"),
            c_s, nm_s[:, :, :, 0, :], nm_s[:, :, :, 1, 0], st("cv_s"),
            st("fc_p"), st("fc_s"), mem_leaf(mem_p[0]), mem_leaf(mem_p[1]))
```

```python
import functools
import math

import jax
import jax.numpy as jnp
from jax import lax
from jax.experimental import pallas as pl
from jax.experimental.pallas import tpu as pltpu

FOX_H, FOX_DH = 8, 64
FOX_W = FOX_H * FOX_DH
ML_H, ML_DH = 4, 128
ML_W = ML_H * ML_DH
ML_CONV = 4
X_H, X_DH = 4, 128
X_W = X_H * X_DH
FFN_CONV = 3
EPS = 1e-6
PAGE = 128
NEG = -1e30
LOG2E = 1.4426950408889634
GROUP = 8
LANES = 128
VMEM_LIMIT = 56 * 1024 * 1024

F32 = jnp.float32
BF16 = jnp.bfloat16


def _cparams(sem):
    return pltpu.CompilerParams(dimension_semantics=sem, vmem_limit_bytes=VMEM_LIMIT)


def _split3(x):
    hi = x.astype(BF16)
    r1 = x - hi.astype(F32)
    mid = r1.astype(BF16)
    lo = (r1 - mid.astype(F32)).astype(BF16)
    return hi, mid, lo


def _dot(a, b):
    return jnp.dot(a, b, preferred_element_type=F32)


def _dot_nt(a, b):
    return lax.dot_general(a, b, (((1,), (1,)), ((), ())), preferred_element_type=F32)


def _dot_tn(a, b):
    return lax.dot_general(a, b, (((0,), (0,)), ((), ())), preferred_element_type=F32)


def _dot3_lhs(x, w):
    hi, mid, lo = _split3(x)
    return _dot(hi, w) + _dot(mid, w) + _dot(lo, w)


def _dot3_rhs(w, x):
    hi, mid, lo = _split3(x)
    return _dot(w, hi) + _dot(w, mid) + _dot(w, lo)


def _rms(x, g):
    return x * lax.rsqrt(jnp.mean(x * x, axis=-1, keepdims=True) + EPS) * g


def _log_sigmoid(x):
    return jnp.minimum(x, 0.0) - jnp.log1p(jnp.exp(-jnp.abs(x)))


def _sigmoid(x):
    return 1.0 / (1.0 + jnp.exp(-x))


def _gelu_tanh(x):
    c = 0.7978845608028654
    return x * (0.5 * (1.0 + jnp.tanh(c * (x + 0.044715 * (x * x * x)))))


def _slab_spec(l, depth, first, block, index_map):
    if first:
        return pl.BlockSpec((depth,) + block, lambda *a: (0,) + tuple(index_map(*a)))
    return pl.BlockSpec((1,) + block, lambda *a: (l,) + tuple(index_map(*a)))


def _store_slab(ref, slab, value):
    li, nl = slab
    for k in range(nl):
        ref[k] = value if k == li else jnp.zeros_like(value)


def _norm_proj_kernel(*refs, tn, nz, with_gates, nt, n_alias, slab):
    refs = list(refs)
    x_ref, g_ref, w_ref = refs[0:3]
    pos = 3
    if with_gates:
        ws_ref, bs_ref = refs[pos:pos + 2]
        pos += 2
    if nt:
        wt_ref = refs[pos]
        pos += 1
    pos += n_alias
    z_ref = refs[pos]
    pos += 1
    if with_gates:
        zs_ref = refs[pos]
        pos += 1
    t_refs = refs[pos:pos + nt]

    xn = _rms(x_ref[...], g_ref[...]).astype(BF16)
    if with_gates:
        zs = _dot(xn, ws_ref[...]) + bs_ref[...]
        lane = lax.broadcasted_iota(jnp.int32, zs.shape, 1)
        forget = (lane < FOX_H) | ((lane >= FOX_H + ML_H) & (lane < FOX_H + 2 * ML_H))
        zs_ref[...] = jnp.where(forget, _log_sigmoid(zs), zs)
    for j in range(nz):
        z_ref[j] = _dot(xn, w_ref[:, j * tn:(j + 1) * tn])
    for j in range(nt):
        _store_slab(t_refs[j], slab, _dot_nt(wt_ref[j], xn)[None])


def _layer_spec(l, *shape):
    return pl.BlockSpec((None,) + shape, lambda *_: (l,) + (0,) * len(shape))


def _norm_proj(x, l, P, nz, tn, rows_per_batch=None, t_prev=None, tm=512):
    M, K = x.shape
    tm = min(tm, M)
    depth = P["wbig"].shape[0]
    nt = 0 if rows_per_batch is None else P["wkv_t"].shape[1]
    n_alias = 0 if t_prev is None else nt
    aliases = {}
    in_specs = [
        pl.BlockSpec((tm, K), lambda i: (i, 0)),
        _layer_spec(l, 1, K),
        _layer_spec(l, K, nz * tn),
        _layer_spec(l, K, LANES),
        _layer_spec(l, 1, LANES),
    ]
    args = [x, P["g_pre_mix"], P["wbig"], P["wsmall"], P["bsmall"]]
    if nt:
        in_specs.append(_layer_spec(l, nt, tn, K))
        args.append(P["wkv_t"])
    out_shape = [jax.ShapeDtypeStruct((nz, M, tn), F32), jax.ShapeDtypeStruct((M, LANES), F32)]
    out_specs = [pl.BlockSpec((nz, tm, tn), lambda i: (0, i, 0)), pl.BlockSpec((tm, LANES), lambda i: (i, 0))]
    if nt:
        tpb = rows_per_batch // tm
        for j in range(nt):
            if t_prev is not None:
                in_specs.append(pl.BlockSpec(memory_space=pl.ANY))
                args.append(t_prev[j])
                aliases[len(args) - 1] = len(out_shape)
            out_shape.append(jax.ShapeDtypeStruct((depth, M // rows_per_batch, tn, rows_per_batch), F32))
            out_specs.append(_slab_spec(l, depth, t_prev is None, (1, tn, tm), lambda i: (i // tpb, 0, i % tpb)))
    slab = (l, depth) if t_prev is None else (0, 1)
    return pl.pallas_call(
        functools.partial(_norm_proj_kernel, tn=tn, nz=nz, with_gates=True, nt=nt, n_alias=n_alias, slab=slab),
        grid=(M // tm,),
        in_specs=in_specs,
        out_specs=out_specs,
        out_shape=out_shape,
        input_output_aliases=aliases,
        compiler_params=_cparams(("arbitrary",)),
        name="norm_proj",
    )(*args)


def _mem_proj_kernel(x_ref, g_ref, w_ref, *refs, slab):
    k_ref, v_ref = refs[-2:]
    li, nl = slab
    xn = _rms(x_ref[...], g_ref[...]).astype(BF16)
    n = x_ref.shape[0]
    for j, o_ref in enumerate((k_ref, v_ref)):
        for k in range(nl):
            if k != li:
                o_ref[k] = jnp.zeros(o_ref.shape[1:], F32)
        for h in range(X_H):
            col = j * X_W + h * X_DH
            o_ref[li, 0, pl.ds(h, n, stride=X_H), :] = _dot(xn, w_ref[:, col:col + X_DH])


def _mem_proj(x, l, P, B, prev=None):
    M, K = x.shape
    n_mem = M // B
    depth = P["w_xkv"].shape[0]
    in_specs = [pl.BlockSpec((n_mem, K), lambda b: (b, 0)), _layer_spec(l, 1, K), _layer_spec(l, K, 2 * X_W)]
    args = [x, P["g_mem"], P["w_xkv"]]
    aliases = {}
    if prev is not None:
        for j in range(2):
            in_specs.append(pl.BlockSpec(memory_space=pl.ANY))
            args.append(prev[j])
            aliases[len(args) - 1] = j
    shape = jax.ShapeDtypeStruct((depth, B, n_mem * X_H, X_DH), F32)
    spec = _slab_spec(l, depth, prev is None, (1, n_mem * X_H, X_DH), lambda b: (b, 0, 0))
    return pl.pallas_call(
        functools.partial(_mem_proj_kernel, slab=(l, depth) if prev is None else (0, 1)),
        grid=(B,),
        in_specs=in_specs,
        out_specs=[spec, spec],
        out_shape=[shape, shape],
        input_output_aliases=aliases,
        compiler_params=_cparams(("arbitrary",)),
        name="mem_proj",
    )(*args)


def _tri_incl(n, upper):
    r = lax.broadcasted_iota(jnp.int32, (n, n), 0)
    c = lax.broadcasted_iota(jnp.int32, (n, n), 1)
    return ((r <= c) if upper else (c <= r)).astype(BF16)


BIAS_ROWS = 16


def _fox_cum_kernel(zs_ref, ft_ref, *, nb):
    n = nb * FOX_H
    lf = jnp.concatenate([zs_ref[u * LANES:(u + 1) * LANES, :].T[0:FOX_H, :] for u in range(nb)], axis=0)
    cum = _dot3_lhs(lf, _tri_incl(LANES, True))
    r = lax.broadcasted_iota(jnp.int32, (n, n), 0)
    c = lax.broadcasted_iota(jnp.int32, (n, n), 1)
    earlier = ((c % FOX_H == r % FOX_H) & (c // FOX_H < r // FOX_H)).astype(BF16)
    prefix = _dot3_rhs(earlier, jnp.broadcast_to(cum[:, LANES - 1:LANES], (n, LANES)))
    pieces = _split3((cum + prefix) * (-LOG2E))
    ro = lax.broadcasted_iota(jnp.int32, (nb * (FOX_H // 2) * BIAS_ROWS, n), 0)
    co = lax.broadcasted_iota(jnp.int32, (nb * (FOX_H // 2) * BIAS_ROWS, n), 1)
    blk = ro // ((FOX_H // 2) * BIAS_ROWS) == co // FOX_H
    pair = (ro // BIAS_ROWS) % (FOX_H // 2)
    slot = ro % BIAS_ROWS
    head = co % FOX_H
    out = None
    for idx, piece in enumerate(pieces):
        sel = blk & (((slot == idx) & (head == 2 * pair)) | ((slot == 3 + idx) & (head == 2 * pair + 1)))
        term = _dot(sel.astype(BF16), piece)
        out = term if out is None else out + term
    ft_ref[0] = out.reshape(nb, FOX_H // 2, BIAS_ROWS, LANES)


def _fox_cum(zs, B, S):
    nb = S // LANES
    return pl.pallas_call(
        functools.partial(_fox_cum_kernel, nb=nb),
        grid=(B,),
        in_specs=[pl.BlockSpec((S, LANES), lambda b: (b, 0))],
        out_specs=pl.BlockSpec((1, nb, FOX_H // 2, BIAS_ROWS, LANES), lambda b: (b, 0, 0, 0, 0)),
        out_shape=jax.ShapeDtypeStruct((B, nb, FOX_H // 2, BIAS_ROWS, LANES), F32),
        compiler_params=_cparams(("arbitrary",)),
        name="fox_cum",
    )(zs)


def _fox_prompt_kernel(ti_ref, tj_ref, q_ref, kt_ref, vt_ref, ft_ref, o_ref, m_s, l_s, acc_s, *, tq):
    t = pl.program_id(1)
    i = ti_ref[t]
    j = tj_ref[t]
    sub = tq // LANES
    npair = FOX_H // 2
    low = lax.broadcasted_iota(jnp.int32, (tq, LANES), 1) < FOX_DH

    @pl.when(j == 0)
    def _():
        m_s[...] = jnp.full(m_s.shape, NEG, F32)
        l_s[...] = jnp.zeros(l_s.shape, F32)
        acc_s[...] = jnp.zeros(acc_s.shape, F32)

    def update(masked):
        if masked:
            causal = (lax.broadcasted_iota(jnp.int32, (tq, tq), 1)
                      <= lax.broadcasted_iota(jnp.int32, (tq, tq), 0))
        for p in range(npair):
            rows = slice(p * LANES, (p + 1) * LANES)
            qp = q_ref[0, :, rows] * (FOX_DH ** -0.5 * LOG2E)
            bias = jnp.concatenate([ft_ref[0, u, p] for u in range(sub)], axis=1).astype(BF16)
            ka = jnp.concatenate(
                [kt_ref[rows, :].astype(BF16), bias, jnp.zeros((LANES - BIAS_ROWS, tq), BF16)], axis=0)
            vb = vt_ref[rows, :].astype(BF16)
            for e in range(2):
                h = 2 * p + e
                qh = jnp.where(low if e == 0 else jnp.logical_not(low), qp, 0.0)
                ones = ((bias_lane >= 3 * e) & (bias_lane < 3 * e + 3)).astype(F32)
                qa = jnp.concatenate([qh, ones], axis=1).astype(BF16)
                s = _dot(qa, ka)
                if masked:
                    s = jnp.where(causal, s, NEG)
                sc = [s[:, c * LANES:(c + 1) * LANES] for c in range(sub)]
                mx = functools.reduce(jnp.maximum, sc)
                m_prev = m_s[h]
                m_new = jnp.maximum(m_prev, jnp.max(mx, axis=1, keepdims=True))
                alpha = jnp.exp2(m_prev - m_new)
                ps = [jnp.exp2(c - m_new) for c in sc]
                l_s[h] = alpha * l_s[h] + jnp.sum(functools.reduce(jnp.add, ps), axis=1, keepdims=True)
                pr = jnp.concatenate([c.astype(BF16) for c in ps], axis=1)
                acc_s[h] = alpha * acc_s[h] + _dot_nt(pr, vb)
                m_s[h] = m_new

    bias_lane = lax.broadcasted_iota(jnp.int32, (tq, LANES), 1)

    @pl.when(j < i)
    def _():
        update(False)

    @pl.when(j == i)
    def _():
        update(True)
        for p in range(npair):
            oe = acc_s[2 * p] / l_s[2 * p]
            oo = acc_s[2 * p + 1] / l_s[2 * p + 1]
            o_ref[:, p * LANES:(p + 1) * LANES] = jnp.where(low, oe, oo).astype(o_ref.dtype)


def _fox_prompt(z3, kt, vt, ft, B, S, iq, layer, tq=512):
    tq = min(tq, S)
    nq = S // tq
    M = B * S
    sub = tq // LANES
    pairs = [(i, j) for i in range(nq) for j in range(i + 1)]
    ti = jnp.asarray([p[0] for p in pairs], jnp.int32)
    tj = jnp.asarray([p[1] for p in pairs], jnp.int32)
    grid_spec = pltpu.PrefetchScalarGridSpec(
        num_scalar_prefetch=2,
        grid=(B, len(pairs)),
        in_specs=[
            pl.BlockSpec((1, tq, FOX_W), lambda b, t, ti, tj: (iq, b * nq + ti[t], 0)),
            pl.BlockSpec((None, None, FOX_W, tq), lambda b, t, ti, tj: (layer, b, 0, tj[t])),
            pl.BlockSpec((None, None, FOX_W, tq), lambda b, t, ti, tj: (layer, b, 0, tj[t])),
            pl.BlockSpec((1, sub, FOX_H // 2, BIAS_ROWS, LANES), lambda b, t, ti, tj: (b, tj[t], 0, 0, 0)),
        ],
        out_specs=pl.BlockSpec((tq, FOX_W), lambda b, t, ti, tj: (b * nq + ti[t], 0)),
        scratch_shapes=[
            pltpu.VMEM((FOX_H, tq, LANES), F32),
            pltpu.VMEM((FOX_H, tq, LANES), F32),
            pltpu.VMEM((FOX_H, tq, LANES), F32),
        ],
    )
    return pl.pallas_call(
        functools.partial(_fox_prompt_kernel, tq=tq),
        grid_spec=grid_spec,
        out_shape=jax.ShapeDtypeStruct((M, FOX_W), BF16),
        compiler_params=_cparams(("arbitrary", "arbitrary")),
        name="fox_prompt",
    )(ti, tj, z3, kt, vt, ft)


SKIP_LOG = -110.0
KEY_SLOTS = 3


def _fox_decode_kernel(pt_ref, q_ref, kn_ref, vn_ref, zs_ref, lf_hbm, k_hbm, v_hbm, o_ref,
                       kbuf, vbuf, lfbuf, sbuf, m_ref, l_ref, acc_ref, need_ref, ksem, vsem, lfsem,
                       *, layer, npg, ng):
    b = pl.program_id(0)
    nreal = GROUP // 2
    rows = nreal * FOX_H
    n = npg * FOX_H

    def copies(hbm, buf, sem, g, slot, req=None):
        req = b if req is None else req
        return [pltpu.make_async_copy(hbm.at[layer, pt_ref[req, g * npg + i]], buf.at[slot, i], sem.at[slot])
                for i in range(npg)]

    def start_keys(g, slot, req=None):
        for c in copies(lf_hbm, lfbuf, lfsem, g, slot, req) + copies(k_hbm, kbuf, ksem, g, slot, req):
            c.start()

    order = [ng - 1 - it for it in range(ng)]
    ahead = min(KEY_SLOTS - 1, ng)

    @pl.when(b == 0)
    def _():
        for it in range(ahead):
            start_keys(order[it], it)

    hrow = lax.broadcasted_iota(jnp.int32, (FOX_H, FOX_W), 0)
    hcol = lax.broadcasted_iota(jnp.int32, (FOX_H, FOX_W), 1) // FOX_DH
    head_sel = jnp.concatenate([hrow == hcol] * nreal, axis=0)
    q = q_ref[...] * (FOX_DH ** -0.5)
    qbd = jnp.concatenate(
        [jnp.broadcast_to(q[nreal + t:nreal + t + 1, :], (FOX_H, FOX_W)) for t in range(nreal)], axis=0)
    qbd = jnp.where(head_sel, qbd, 0.0).astype(BF16)
    eye = (lax.broadcasted_iota(jnp.int32, (rows, rows), 0)
           == lax.broadcasted_iota(jnp.int32, (rows, rows), 1))

    def as_row(col):
        return jnp.sum(jnp.where(eye, jnp.broadcast_to(col, (rows, rows)), 0.0), axis=0, keepdims=True)

    zpad = jnp.zeros((LANES - GROUP, FOX_W), F32)
    kn = jnp.concatenate([kn_ref[0], zpad], axis=0).astype(BF16)
    vn_t = jnp.concatenate([vn_ref[0], zpad], axis=0).T.astype(BF16)
    zs = jnp.concatenate([zs_ref[...], jnp.zeros((LANES - GROUP, LANES), F32)], axis=0)
    lf_new = zs.T[0:FOX_H, :]
    kr = lax.broadcasted_iota(jnp.int32, (LANES, LANES), 0)
    kc = lax.broadcasted_iota(jnp.int32, (LANES, LANES), 1)
    inc = ((kr <= kc) & (kr >= nreal) & (kr < GROUP)).astype(BF16)
    f_new = jnp.concatenate([_dot3_lhs(lf_new, inc)] * nreal, axis=0)
    qi = lax.broadcasted_iota(jnp.int32, (rows, LANES), 0) // FOX_H
    kj = lax.broadcasted_iota(jnp.int32, (rows, LANES), 1) - nreal
    s = jnp.where((kj >= 0) & (kj <= qi), _dot_nt(qbd, kn) - f_new, NEG)
    m0 = jnp.max(s, axis=1, keepdims=True)
    p0 = jnp.exp(s - m0)
    m_ref[...] = m0
    l_ref[...] = jnp.sum(p0, axis=1, keepdims=True)
    acc_ref[...] = _dot_nt(vn_t, p0.astype(BF16))

    er = lax.broadcasted_iota(jnp.int32, (n, n), 0)
    ec = lax.broadcasted_iota(jnp.int32, (n, n), 1)
    later_pages = ((ec % FOX_H == er % FOX_H) & (ec // FOX_H > er // FOX_H)).astype(BF16)

    def update(g, slot):
        @pl.when(need_ref[slot] == 1)
        def _():
            for c in copies(v_hbm, vbuf, vsem, g, slot):
                c.wait()
            vcat = jnp.concatenate([vbuf[slot, i].astype(BF16) for i in range(npg)], axis=1)
            sg = sbuf[slot]
            m = m_ref[...]
            m_new = jnp.maximum(m, jnp.max(sg, axis=1, keepdims=True))
            alpha = jnp.exp(m - m_new)
            pr = jnp.exp(sg - m_new)
            l_ref[...] = alpha * l_ref[...] + jnp.sum(pr, axis=1, keepdims=True)
            acc_ref[...] = as_row(alpha) * acc_ref[...] + _dot_nt(vcat, pr.astype(BF16))
            m_ref[...] = m_new

    carry = jnp.zeros((FOX_H, 1), F32)
    for it, g in enumerate(order):
        slot = it % 2
        kslot = it % KEY_SLOTS
        if it + KEY_SLOTS - 1 < ng:
            start_keys(order[it + KEY_SLOTS - 1], (it + KEY_SLOTS - 1) % KEY_SLOTS)
        for c in copies(lf_hbm, lfbuf, lfsem, g, kslot):
            c.wait()
        lf = jnp.concatenate([lfbuf[kslot, i] for i in range(npg)], axis=0)
        incl = _dot3_lhs(lf, _tri_incl(LANES, False))
        later = _dot3_rhs(later_pages, jnp.broadcast_to(incl[:, 0:1], (n, LANES)))
        after = incl - lf + later + jnp.concatenate([jnp.broadcast_to(carry, (FOX_H, LANES))] * npg, axis=0)
        carry = after[0:FOX_H, 0:1] + lf[0:FOX_H, 0:1]
        gcat = jnp.concatenate([after[i * FOX_H:(i + 1) * FOX_H, :] for i in range(npg)], axis=1)
        for c in copies(k_hbm, kbuf, ksem, g, kslot):
            c.wait()
        kcat = jnp.concatenate([kbuf[kslot, i].astype(BF16) for i in range(npg)], axis=1)
        sg = _dot(qbd, kcat) + jnp.concatenate([gcat] * nreal, axis=0)
        sbuf[slot] = sg
        need = jnp.max(jnp.max(sg, axis=1, keepdims=True) - m_ref[...]) > SKIP_LOG
        need_ref[slot] = need.astype(jnp.int32)

        @pl.when(need)
        def _():
            for c in copies(v_hbm, vbuf, vsem, g, slot):
                c.start()

        if it >= 1:
            update(order[it - 1], 1 - slot)

    @pl.when(b + 1 < pl.num_programs(0))
    def _():
        for it in range(ahead):
            start_keys(order[it], it, req=b + 1)

    update(order[ng - 1], (ng - 1) % 2)

    out_t = acc_ref[...] / as_row(l_ref[...])
    own = (lax.broadcasted_iota(jnp.int32, (FOX_W, rows), 0) // FOX_DH
           == lax.broadcasted_iota(jnp.int32, (FOX_W, rows), 1) % FOX_H)
    out_t = jnp.where(own, out_t, 0.0).astype(BF16)
    gr = lax.broadcasted_iota(jnp.int32, (GROUP, rows), 0) - nreal
    gc = lax.broadcasted_iota(jnp.int32, (GROUP, rows), 1) // FOX_H
    o_ref[...] = _dot_nt((gr == gc).astype(BF16), out_t)


def _fox_decode(z3, zs, cache_k, cache_v, cache_lf, page_table, layer, iq, ik, iv, npg=16):
    Bd, n_pages = page_table.shape
    npg = min(npg, n_pages)
    ng = n_pages // npg
    rows = (GROUP // 2) * FOX_H
    any_spec = pl.BlockSpec(memory_space=pl.ANY)
    grid_spec = pltpu.PrefetchScalarGridSpec(
        num_scalar_prefetch=1,
        grid=(Bd,),
        in_specs=[
            pl.BlockSpec((None, GROUP, FOX_W), lambda b, pt: (iq, b, 0)),
            pl.BlockSpec((1, GROUP, FOX_W), lambda b, pt: (ik, b, 0)),
            pl.BlockSpec((1, GROUP, FOX_W), lambda b, pt: (iv, b, 0)),
            pl.BlockSpec((GROUP, LANES), lambda b, pt: (b, 0)),
            any_spec, any_spec, any_spec,
        ],
        out_specs=pl.BlockSpec((GROUP, FOX_W), lambda b, pt: (b, 0)),
        scratch_shapes=[
            pltpu.VMEM((KEY_SLOTS, npg, FOX_W, PAGE), F32),
            pltpu.VMEM((2, npg, FOX_W, PAGE), F32),
            pltpu.VMEM((KEY_SLOTS, npg, FOX_H, PAGE), F32),
            pltpu.VMEM((2, rows, npg * PAGE), F32),
            pltpu.VMEM((rows, 1), F32),
            pltpu.VMEM((rows, 1), F32),
            pltpu.VMEM((FOX_W, rows), F32),
            pltpu.SMEM((2,), jnp.int32),
            pltpu.SemaphoreType.DMA((KEY_SLOTS,)),
            pltpu.SemaphoreType.DMA((2,)),
            pltpu.SemaphoreType.DMA((KEY_SLOTS,)),
        ],
    )
    return pl.pallas_call(
        functools.partial(_fox_decode_kernel, layer=layer, npg=npg, ng=ng),
        grid_spec=grid_spec,
        out_shape=jax.ShapeDtypeStruct((Bd * GROUP, FOX_W), F32),
        compiler_params=_cparams(("arbitrary",)),
        name="fox_decode",
    )(page_table, z3, z3, z3, zs, cache_lf, cache_k, cache_v)


def _mlstm_kernel(*refs, rin, sample, slab):
    qk_ref, v_ref, og_ref, zs_ref, wc_ref, bc_ref, gh_ref = refs[0:7]
    if sample:
        inj_ref, c0_ref, nm0_ref = refs[7:10]
    h_ref, c_out_ref, nm_out_ref, xbuf, c_s, nm_s = refs[-6:]
    t = pl.program_id(1)
    rc = LANES

    @pl.when(t == 0)
    def _():
        xbuf[0:8, :] = jnp.zeros((8, 2 * ML_W), F32)
        if sample:
            c_s[...] = c0_ref[0]
            nm_s[...] = nm0_ref[0]
        else:
            c_s[...] = jnp.zeros(c_s.shape, F32)
            nm_s[...] = jnp.zeros(nm_s.shape, F32)

    x = jnp.concatenate([qk_ref[0], qk_ref[1]], axis=1)
    if sample:
        slot = lax.broadcasted_iota(jnp.int32, (rin, 1), 0)
        x = jnp.where((slot >= 1) & (slot < GROUP // 2), inj_ref[...], x)
    xbuf[8:8 + rin, :] = x
    y = bc_ref[...] + wc_ref[3:4, :] * x
    for jj in range(ML_CONV - 1):
        y = y + wc_ref[jj:jj + 1, :] * xbuf[pl.ds(8 - (ML_CONV - 1) + jj, rin), :]
    xbuf[0:8, :] = xbuf[rin:rin + 8, :]
    y = y * _sigmoid(y)
    q_all = y[:, 0:ML_W]
    k_all = y[:, ML_W:] * (ML_DH ** -0.5)
    v_all = v_ref[0]
    zs = zs_ref[...]
    if rin < rc:
        pad = jnp.zeros((rc - rin, ML_W), F32)
        q_all = jnp.concatenate([q_all, pad], axis=0)
        k_all = jnp.concatenate([k_all, pad], axis=0)
        v_all = jnp.concatenate([v_all, pad], axis=0)
        zs = jnp.concatenate([zs, jnp.zeros((rc - rin, LANES), F32)], axis=0)
    if sample:
        row = lax.broadcasted_iota(jnp.int32, (rc, LANES), 0)
        lane = lax.broadcasted_iota(jnp.int32, (rc, LANES), 1)
        padrow = (row < GROUP // 2) | (row >= GROUP)
        is_i = (lane >= FOX_H) & (lane < FOX_H + ML_H)
        zs = jnp.where(padrow, jnp.where(is_i, NEG, 0.0), zs)

    causal = (lax.broadcasted_iota(jnp.int32, (rc, rc), 1) <= lax.broadcasted_iota(jnp.int32, (rc, rc), 0))
    for u in range(max(rin // rc, 1)):
        rs = slice(u * rc, (u + 1) * rc)
        zs_u = zs[rs, :]
        colcum = _dot3_rhs(_tri_incl(rc, False), zs_u)
        rowraw = zs_u.T[FOX_H:FOX_H + 2 * ML_H, :]
        rowcum = _dot3_lhs(rowraw, _tri_incl(rc, True))

        heads = []
        for h in range(ML_H):
            cs = slice(h * ML_DH, (h + 1) * ML_DH)
            qf = q_all[rs, cs]
            kf = k_all[rs, cs]
            qh = qf.astype(BF16)
            kh = kf.astype(BF16)
            vh = v_all[rs, cs].astype(BF16)
            icol = jnp.broadcast_to(zs_u[:, FOX_H + h:FOX_H + h + 1], (rc, LANES))
            bcol = jnp.broadcast_to(colcum[:, FOX_H + ML_H + h:FOX_H + ML_H + h + 1], (rc, LANES))
            irow = rowraw[h:h + 1, :]
            brow = rowcum[ML_H + h:ML_H + h + 1, :]
            cmat = c_s[h]
            nrow = nm_s[h, 0:1, :]
            m = nm_s[h, 1:2, :]

            dlog = jnp.where(causal, bcol - brow + irow, NEG)
            g = bcol + m
            mt = jnp.maximum(g, jnp.max(dlog, axis=1, keepdims=True))
            w_intra = jnp.exp(dlog - mt)
            w_inter = jnp.exp(g - mt)
            sc = _dot_nt(qh, kh) * w_intra
            num = _dot(sc.astype(BF16), vh) + _dot(qh, cmat.astype(BF16)) * w_inter
            den = jnp.sum(sc, axis=1, keepdims=True) + w_inter * jnp.sum(qf * nrow, axis=1, keepdims=True)
            hh = num / jnp.maximum(jnp.abs(den), jnp.exp(-mt))

            b_last = bcol[rc - 1:rc, :]
            a = b_last - bcol + icol
            m_new = jnp.maximum(b_last + m, jnp.max(a, axis=0, keepdims=True))
            ws = jnp.exp(a - m_new)
            decay = jnp.exp(b_last + m - m_new)
            kw = kf * ws
            c_s[h] = decay * cmat + _dot_tn(kw.astype(BF16), vh)
            nm_s[h, 0:1, :] = decay * nrow + jnp.sum(kw, axis=0, keepdims=True)
            nm_s[h, 1:2, :] = m_new

            heads.append(_rms(hh, gh_ref[:, cs]))
        hfull = jnp.concatenate(heads, axis=1)
        if sample:
            h_ref[...] = (hfull[0:rin] * _sigmoid(og_ref[0])).astype(h_ref.dtype)
        else:
            h_ref[rs, :] = (hfull * _sigmoid(og_ref[0, rs, :])).astype(h_ref.dtype)

    @pl.when(t == pl.num_programs(1) - 1)
    def _():
        _store_slab(c_out_ref, slab, c_s[...][None])
        nm_out_ref[0] = nm_s[...]


def _mlstm(z3, zs, l, P, B, L, iqk, iv, io, state=None, c_prev=None):
    sample = state is not None
    rin = GROUP if sample else math.gcd(L, 2 * LANES)
    nt = L // rin
    M = B * L
    depth = P["w_ml_conv"].shape[0]
    in_specs = [
        pl.BlockSpec((2, rin, ML_W), lambda b, t: (iqk, b * nt + t, 0)),
        pl.BlockSpec((1, rin, ML_W), lambda b, t: (iv, b * nt + t, 0)),
        pl.BlockSpec((1, rin, ML_W), lambda b, t: (io, b * nt + t, 0)),
        pl.BlockSpec((rin, LANES), lambda b, t: (b * nt + t, 0)),
        _layer_spec(l, ML_CONV, 2 * ML_W),
        _layer_spec(l, 1, 2 * ML_W),
        _layer_spec(l, 1, ML_W),
    ]
    args = [z3, z3, z3, zs, P["w_ml_conv"], P["b_ml_conv"], P["g_ml_head"]]
    if sample:
        in_specs += [
            pl.BlockSpec((None, rin, 2 * ML_W), lambda b, t: (l, b * nt + t, 0)),
            pl.BlockSpec((None, 1, ML_H, ML_DH, ML_DH), lambda b, t: (l, b, 0, 0, 0)),
            pl.BlockSpec((None, 1, ML_H, 8, ML_DH), lambda b, t: (l, b, 0, 0, 0)),
        ]
        args += list(state)
    aliases = {}
    if c_prev is not None:
        in_specs.append(pl.BlockSpec(memory_space=pl.ANY))
        args.append(c_prev)
        aliases[len(args) - 1] = 1
    return pl.pallas_call(
        functools.partial(_mlstm_kernel, rin=rin, sample=sample,
                          slab=(l, depth) if c_prev is None else (0, 1)),
        grid=(B, nt),
        in_specs=in_specs,
        out_specs=[
            pl.BlockSpec((rin, ML_W), lambda b, t: (b * nt + t, 0)),
            _slab_spec(l, depth, c_prev is None, (1, ML_H, ML_DH, ML_DH), lambda b, t: (b, 0, 0, 0)),
            pl.BlockSpec((1, ML_H, 8, ML_DH), lambda b, t: (b, 0, 0, 0)),
        ],
        out_shape=[
            jax.ShapeDtypeStruct((M, ML_W), F32 if sample else BF16),
            jax.ShapeDtypeStruct((depth, B, ML_H, ML_DH, ML_DH), F32),
            jax.ShapeDtypeStruct((B, ML_H, 8, ML_DH), F32),
        ],
        input_output_aliases=aliases,
        scratch_shapes=[
            pltpu.VMEM((rin + 8, 2 * ML_W), F32),
            pltpu.VMEM((ML_H, ML_DH, ML_DH), F32),
            pltpu.VMEM((ML_H, 8, ML_DH), F32),
        ],
        compiler_params=_cparams(("arbitrary", "arbitrary")),
        name="mlstm_sample" if sample else "mlstm_prompt",
    )(*args)


def _mix_xattn_kernel(fo_ref, hm_ref, x_ref, wo_ref, gpm_ref, gpx_ref, wq_ref, mk_ref, mv_ref, wxo_ref, gox_ref, o_ref):
    mix = (_dot(fo_ref[...].astype(BF16), wo_ref[0:FOX_W, :])
           + _dot(hm_ref[...].astype(BF16), wo_ref[FOX_W:, :]))
    x1 = x_ref[...] + _rms(mix, gpm_ref[...])
    hq = _rms(x1, gpx_ref[...]).astype(BF16)
    xq = _dot(hq, wq_ref[...])
    nb = mk_ref.shape[0]
    rt = xq.shape[0] // nb
    n_mem = mk_ref.shape[1] // X_H
    parts = []
    for j in range(nb):
        heads = []
        for h in range(X_H):
            cs = slice(h * X_DH, (h + 1) * X_DH)
            mk = mk_ref[j, pl.ds(h, n_mem, stride=X_H), :].astype(BF16)
            mv = mv_ref[j, pl.ds(h, n_mem, stride=X_H), :].astype(BF16)
            s = _dot_nt(xq[j * rt:(j + 1) * rt, cs].astype(BF16), mk) * (X_DH ** -0.5)
            e = jnp.exp(s - jnp.max(s, axis=1, keepdims=True))
            heads.append(_dot(e.astype(BF16), mv) / jnp.sum(e, axis=1, keepdims=True))
        parts.append(jnp.concatenate(heads, axis=1))
    xo = _dot((parts[0] if nb == 1 else jnp.concatenate(parts, axis=0)).astype(BF16), wxo_ref[...])
    o_ref[...] = x1 + _rms(xo, gox_ref[...])


def _mix_xattn(fo, hm, x, l, P, mk, mv, rows_per_batch, tm):
    M, D = x.shape
    if tm <= rows_per_batch:
        tpb = rows_per_batch // tm
        mem_spec = pl.BlockSpec((None, 1) + mk.shape[2:], lambda i: (l, i // tpb, 0, 0))
    else:
        nb = tm // rows_per_batch
        mem_spec = pl.BlockSpec((None, nb) + mk.shape[2:], lambda i: (l, i, 0, 0))
    return pl.pallas_call(
        _mix_xattn_kernel,
        grid=(M // tm,),
        in_specs=[
            pl.BlockSpec((tm, FOX_W), lambda i: (i, 0)),
            pl.BlockSpec((tm, ML_W), lambda i: (i, 0)),
            pl.BlockSpec((tm, D), lambda i: (i, 0)),
            _layer_spec(l, FOX_W + ML_W, D),
            _layer_spec(l, 1, D), _layer_spec(l, 1, D),
            _layer_spec(l, D, X_W),
            mem_spec, mem_spec,
            _layer_spec(l, X_W, D),
            _layer_spec(l, 1, D),
        ],
        out_specs=pl.BlockSpec((tm, D), lambda i: (i, 0)),
        out_shape=jax.ShapeDtypeStruct((M, D), F32),
        compiler_params=_cparams(("arbitrary",)),
        name="mix_xattn",
    )(fo, hm, x, P["w_out"], P["g_post_mix"], P["g_pre_x"], P["w_xq"], mk, mv, P["w_xo"], P["g_post_x"])


def _convffn_kernel(*refs, tm, tiles_per_batch, sample):
    if sample:
        (x_ref, gpre_ref, wg_ref, wu_ref, wc_ref, bc_ref, wd_ref, gpost_ref, inj_ref,
         o_ref, gate_ref, xn_ref, acc_ref, gbuf, tails) = refs
    else:
        (x_ref, gpre_ref, wg_ref, wu_ref, wc_ref, bc_ref, wd_ref, gpost_ref,
         o_ref, gate_ref, xn_ref, acc_ref, gbuf, tails) = refs
    i = pl.program_id(0)
    j = pl.program_id(1)

    @pl.when(j == 0)
    def _():
        xn_ref[...] = _rms(x_ref[...], gpre_ref[...]).astype(BF16)
        acc_ref[...] = jnp.zeros_like(acc_ref)

    xn = xn_ref[...]
    g = _dot(xn, wg_ref[...])
    if sample:
        slot = lax.broadcasted_iota(jnp.int32, (tm, 1), 0) % GROUP
        g = jnp.where((slot >= GROUP // 2 - (FFN_CONV - 1)) & (slot < GROUP // 2), inj_ref[...], g)
        gate_ref[...] = g
        gbuf[0:8, :] = jnp.zeros((8, g.shape[1]), F32)
    else:
        first = (i % tiles_per_batch) == 0
        gbuf[0:8, :] = jnp.where(first, 0.0, tails[j])
        gate_ref[0] = g[tm - 8:tm, :]
        tails[j] = g[tm - 8:tm, :]
    gbuf[8:8 + tm, :] = g
    t = bc_ref[...] + wc_ref[2:3, :] * g
    for jj in range(FFN_CONV - 1):
        t = t + wc_ref[jj:jj + 1, :] * gbuf[pl.ds(8 - (FFN_CONV - 1) + jj, tm), :]
    a = _gelu_tanh(t) * _dot(xn, wu_ref[...])
    acc_ref[...] += _dot(a.astype(BF16), wd_ref[...])

    @pl.when(j == pl.num_programs(1) - 1)
    def _():
        o_ref[...] = x_ref[...] + _rms(acc_ref[...], gpost_ref[...])


def _convffn(x, l, P, rows_per_batch, tm, tn, inj=None):
    M, D = x.shape
    F = P["w_gate"].shape[2]
    sample = inj is not None
    nj = F // tn
    tiles_per_batch = max(rows_per_batch // tm, 1)
    in_specs = [
        pl.BlockSpec((tm, D), lambda i, j: (i, 0)),
        _layer_spec(l, 1, D),
        pl.BlockSpec((None, D, tn), lambda i, j: (l, 0, j)),
        pl.BlockSpec((None, D, tn), lambda i, j: (l, 0, j)),
        pl.BlockSpec((None, FFN_CONV, tn), lambda i, j: (l, 0, j)),
        pl.BlockSpec((None, 1, tn), lambda i, j: (l, 0, j)),
        pl.BlockSpec((None, tn, D), lambda i, j: (l, j, 0)),
        _layer_spec(l, 1, D),
    ]
    args = [x, P["g_pre_ffn"], P["w_gate"], P["w_up"], P["w_ffn_conv"], P["b_ffn_conv"], P["w_down"],
            P["g_post_ffn"]]
    if sample:
        in_specs.append(pl.BlockSpec((None, tm, tn), lambda i, j: (l, i, j)))
        args.append(inj)
        gate_shape = jax.ShapeDtypeStruct((M, F), F32)
        gate_spec = pl.BlockSpec((tm, tn), lambda i, j: (i, j))
    else:
        gate_shape = jax.ShapeDtypeStruct((M // tm, 8, F), F32)
        gate_spec = pl.BlockSpec((1, 8, tn), lambda i, j: (i, 0, j))
    return pl.pallas_call(
        functools.partial(_convffn_kernel, tm=tm, tiles_per_batch=tiles_per_batch, sample=sample),
        grid=(M // tm, nj),
        in_specs=in_specs,
        out_specs=[pl.BlockSpec((tm, D), lambda i, j: (i, 0)), gate_spec],
        out_shape=[jax.ShapeDtypeStruct((M, D), F32), gate_shape],
        scratch_shapes=[
            pltpu.VMEM((tm, D), BF16),
            pltpu.VMEM((tm, D), F32),
            pltpu.VMEM((tm + 8, tn), F32),
            pltpu.VMEM((nj, 8, tn), F32),
        ],
        compiler_params=_cparams(("arbitrary", "arbitrary")),
        name="convffn_sample" if sample else "convffn_prompt",
    )(*args)


_IQK, _IMV, _IMO, _IFQ, _IFK, _IFV = 0, 2, 3, 4, 5, 6


def _prep_params(w_in, b_fox_f, b_ml_i, b_ml_f, w_out, w_xq, w_xk, w_xv, w_xo, w_gate, w_up, w_down, vectors):
    depth, K, _ = w_in.shape
    o_ff = 3 * FOX_W
    o_mq = o_ff + FOX_H
    o_mi = o_mq + 3 * ML_W
    o_mo = o_mi + 2 * ML_H
    n_gate = FOX_H + 2 * ML_H
    P = dict(
        wbig=jnp.concatenate([w_in[:, :, o_mq:o_mi], w_in[:, :, o_mo:], w_in[:, :, :o_ff]], axis=2).astype(BF16),
        wsmall=jnp.concatenate([w_in[:, :, o_ff:o_mq], w_in[:, :, o_mi:o_mo],
                                jnp.zeros((depth, K, LANES - n_gate), F32)], axis=2).astype(BF16),
        bsmall=jnp.concatenate([b_fox_f, b_ml_i, b_ml_f, jnp.zeros((depth, LANES - n_gate), F32)],
                               axis=1).reshape(depth, 1, LANES),
        wkv_t=jnp.transpose(w_in[:, :, FOX_W:o_ff].reshape(depth, K, 2, FOX_W), (0, 2, 3, 1)).astype(BF16),
        w_out=w_out.astype(BF16), w_xq=w_xq.astype(BF16),
        w_xkv=jnp.concatenate([w_xk, w_xv], axis=2).astype(BF16),
        w_xo=w_xo.astype(BF16), w_gate=w_gate.astype(BF16), w_up=w_up.astype(BF16), w_down=w_down.astype(BF16))
    for name, v in vectors.items():
        P[name] = v.reshape(depth, 1, v.shape[-1]) if v.ndim == 2 else v
    return P


def kernel(x_prompt, x_sample, mem_prompt, cache_fox_k, cache_fox_v, cache_fox_logf, state_ml_C, state_ml_n, state_ml_m, state_ml_conv, state_ffn_conv, cache_mem_k, cache_mem_v, page_table, g_pre_mix, w_in, b_fox_f, w_ml_conv, b_ml_conv, b_ml_i, b_ml_f, g_ml_head, w_out, g_post_mix, g_pre_x, g_mem, w_xq, w_xk, w_xv, w_xo, g_post_x, g_pre_ffn, w_gate, w_up, w_ffn_conv, b_ffn_conv, w_down, g_post_ffn):
    B, S, D = x_prompt.shape
    Bd, Ld, _ = x_sample.shape
    depth = w_in.shape[0]
    n_mem = mem_prompt.shape[1]
    d_ff = w_gate.shape[2]
    pool = cache_fox_k.shape[1]
    pad = GROUP - Ld
    assert Ld == GROUP // 2 and S % 256 == 0

    xp = x_prompt.reshape(B * S, D)
    xs = jnp.pad(x_sample, ((0, 0), (pad, 0), (0, 0))).reshape(Bd * GROUP, D)
    memx = mem_prompt.reshape(B * n_mem, D)
    ck = jnp.transpose(cache_fox_k, (0, 1, 3, 4, 2)).reshape(depth, pool, FOX_W, PAGE)
    cv = jnp.transpose(cache_fox_v, (0, 1, 3, 4, 2)).reshape(depth, pool, FOX_W, PAGE)
    clf = jnp.transpose(cache_fox_logf, (0, 1, 3, 2))
    ffn_tn = d_ff // 2 if (d_ff // 2) % LANES == 0 else d_ff

    P = _prep_params(
        w_in, b_fox_f, b_ml_i, b_ml_f, w_out, w_xq, w_xk, w_xv, w_xo, w_gate, w_up, w_down,
        dict(g_pre_mix=g_pre_mix, g_mem=g_mem, w_ml_conv=w_ml_conv, b_ml_conv=b_ml_conv, g_ml_head=g_ml_head,
             g_post_mix=g_post_mix, g_pre_x=g_pre_x, g_post_x=g_post_x, g_pre_ffn=g_pre_ffn,
             w_ffn_conv=w_ffn_conv, b_ffn_conv=b_ffn_conv, g_post_ffn=g_post_ffn))
    inj_ml = jnp.pad(state_ml_conv, ((0, 0), (0, 0), (1, GROUP - ML_CONV), (0, 0))).reshape(depth, Bd * GROUP, 2 * ML_W)
    inj_ffn = jnp.pad(state_ffn_conv, ((0, 0), (0, 0), (GROUP // 2 - (FFN_CONV - 1), GROUP // 2), (0, 0)))
    inj_ffn = inj_ffn.reshape(depth, Bd * GROUP, d_ff)
    nm0 = jnp.concatenate(
        [state_ml_n[:, :, :, None, :],
         jnp.broadcast_to(state_ml_m[:, :, :, None, None], (depth, Bd, ML_H, 1, ML_DH)),
         jnp.zeros((depth, Bd, ML_H, 6, ML_DH), F32)], axis=3)
    smk = cache_mem_k.reshape(depth, Bd, n_mem * X_H, X_DH)
    smv = cache_mem_v.reshape(depth, Bd, n_mem * X_H, X_DH)

    outs = {k: [] for k in ("lf_p", "fk_s", "fv_s", "lf_s", "nm_p", "cv_p", "nm_s", "cv_s", "fc_p", "fc_s")}
    kv_p = mem_p = c_p = c_s = None
    for l in range(depth):
        mem_p = _mem_proj(memx, l, P, B, prev=mem_p)
        z3, zs, *kv_p = _norm_proj(xp, l, P, _IFK, FOX_W, rows_per_batch=S, t_prev=kv_p)
        ft = _fox_cum(zs, B, S)
        fo = _fox_prompt(z3, kv_p[0], kv_p[1], ft, B, S, _IFQ, l)
        hm, c_p, nm_new = _mlstm(z3, zs, l, P, B, S, _IQK, _IMV, _IMO, c_prev=c_p)
        xp = _mix_xattn(fo, hm, xp, l, P, mem_p[0], mem_p[1], S, min(512, S))
        xp, gate_tail = _convffn(xp, l, P, S, min(512, S), ffn_tn)
        outs["lf_p"].append(zs[:, :FOX_H].reshape(B, S, FOX_H))
        outs["nm_p"].append(nm_new)
        tails = z3.reshape(z3.shape[0], B, S, ML_W)[0:2, :, S - (ML_CONV - 1):, :]
        outs["cv_p"].append(jnp.transpose(tails, (1, 2, 0, 3)).reshape(B, ML_CONV - 1, 2 * ML_W))
        outs["fc_p"].append(gate_tail.reshape(B, -1, 8, d_ff)[:, -1, 8 - (FFN_CONV - 1):, :])

        z3, zs = _norm_proj(xs, l, P, _IFV + 1, FOX_W)
        fo = _fox_decode(z3, zs, ck, cv, clf, page_table, l, _IFQ, _IFK, _IFV)
        hm, c_s, nm_new = _mlstm(z3, zs, l, P, Bd, GROUP, _IQK, _IMV, _IMO,
                                 state=(inj_ml, state_ml_C, nm0), c_prev=c_s)
        xs = _mix_xattn(fo, hm, xs, l, P, smk, smv, GROUP, GROUP * math.gcd(Bd, 8))
        xs, gate_full = _convffn(xs, l, P, Bd * GROUP, Bd * GROUP, ffn_tn, inj=inj_ffn)
        real = lambda a: a.reshape((Bd, GROUP) + a.shape[1:])[:, pad:]
        outs["fk_s"].append(real(z3[_IFK]).reshape(Bd, Ld, FOX_H, FOX_DH))
        outs["fv_s"].append(real(z3[_IFV]).reshape(Bd, Ld, FOX_H, FOX_DH))
        outs["lf_s"].append(real(zs)[:, :, :FOX_H])
        outs["nm_s"].append(nm_new)
        tails = z3.reshape(z3.shape[0], Bd, GROUP, ML_W)[0:2, :, GROUP - (ML_CONV - 1):, :]
        outs["cv_s"].append(jnp.transpose(tails, (1, 2, 0, 3)).reshape(Bd, ML_CONV - 1, 2 * ML_W))
        outs["fc_s"].append(gate_full.reshape(Bd, GROUP, d_ff)[:, GROUP - (FFN_CONV - 1):, :])

    st = lambda k: jnp.stack(outs[k], axis=0)
    y_p = xp.reshape(B, S, D)
    y_s = xs.reshape(Bd, GROUP, D)[:, pad:]
    to_tokens = lambda a: jnp.transpose(a.reshape(depth, B, FOX_H, FOX_DH, S), (0, 1, 4, 2, 3))
    nm_p, nm_s = st("nm_p"), st("nm_s")
    mem_leaf = lambda a: a.reshape(depth, B, n_mem, X_H, X_DH)
    return (y_p, y_s, to_tokens(kv_p[0]), to_tokens(kv_p[1]), st("lf_p"), st("fk_s"), st("fv_s"), st("lf_s"),
            c_p, nm_p[:, :, :, 0, :], nm_p[:, :, :, 1, 0], st("cv_p"),
            c_s, nm_s[:, :, :, 0, :], nm_s[:, :, :, 1, 0], st("cv_s"),
            st("fc_p"), st("fc_s"), mem_leaf(mem_p[0]), mem_leaf(mem_p[1]))
```

```python
import functools
import math

import jax
import jax.numpy as jnp
from jax import lax
from jax.experimental import pallas as pl
from jax.experimental.pallas import tpu as pltpu

FOX_H, FOX_DH = 8, 64
FOX_W = FOX_H * FOX_DH
ML_H, ML_DH = 4, 128
ML_W = ML_H * ML_DH
ML_CONV = 4
X_H, X_DH = 4, 128
X_W = X_H * X_DH
FFN_CONV = 3
EPS = 1e-6
PAGE = 128
NEG = -1e30
LOG2E = 1.4426950408889634
GROUP = 8
LANES = 128
VMEM_LIMIT = 56 * 1024 * 1024
ROW_TILE = 512
ATTN_TILE = 512
DECODE_PAGES = 16

F32 = jnp.float32
BF16 = jnp.bfloat16


def _cparams(sem):
    return pltpu.CompilerParams(dimension_semantics=sem, vmem_limit_bytes=VMEM_LIMIT)


def _split3(x):
    hi = x.astype(BF16)
    r1 = x - hi.astype(F32)
    mid = r1.astype(BF16)
    lo = (r1 - mid.astype(F32)).astype(BF16)
    return hi, mid, lo


def _dot(a, b):
    return jnp.dot(a, b, preferred_element_type=F32)


def _dot_nt(a, b):
    return lax.dot_general(a, b, (((1,), (1,)), ((), ())), preferred_element_type=F32)


def _dot_tn(a, b):
    return lax.dot_general(a, b, (((0,), (0,)), ((), ())), preferred_element_type=F32)


def _dot3_lhs(x, w):
    hi, mid, lo = _split3(x)
    return _dot(hi, w) + _dot(mid, w) + _dot(lo, w)


def _dot3_rhs(w, x):
    hi, mid, lo = _split3(x)
    return _dot(w, hi) + _dot(w, mid) + _dot(w, lo)


def _rms(x, g):
    return x * lax.rsqrt(jnp.mean(x * x, axis=-1, keepdims=True) + EPS) * g


def _log_sigmoid(x):
    return jnp.minimum(x, 0.0) - jnp.log1p(jnp.exp(-jnp.abs(x)))


def _sigmoid(x):
    return 1.0 / (1.0 + jnp.exp(-x))


def _gelu_tanh(x):
    c = 0.7978845608028654
    return x * (0.5 * (1.0 + jnp.tanh(c * (x + 0.044715 * (x * x * x)))))


def _slab_spec(l, depth, first, block, index_map):
    if first:
        return pl.BlockSpec((depth,) + block, lambda *a: (0,) + tuple(index_map(*a)))
    return pl.BlockSpec((1,) + block, lambda *a: (l,) + tuple(index_map(*a)))


def _store_slab(ref, slab, value):
    li, nl = slab
    for k in range(nl):
        ref[k] = value if k == li else jnp.zeros_like(value)


def _norm_proj_kernel(*refs, tn, nz, with_gates, nt, n_alias, slab):
    refs = list(refs)
    x_ref, g_ref, w_ref = refs[0:3]
    pos = 3
    if with_gates:
        ws_ref, bs_ref = refs[pos:pos + 2]
        pos += 2
    if nt:
        wt_ref = refs[pos]
        pos += 1
    pos += n_alias
    z_ref = refs[pos]
    pos += 1
    if with_gates:
        zs_ref = refs[pos]
        pos += 1
    t_refs = refs[pos:pos + nt]

    xn = _rms(x_ref[...], g_ref[...]).astype(BF16)
    if with_gates:
        zs = _dot(xn, ws_ref[...]) + bs_ref[...]
        lane = lax.broadcasted_iota(jnp.int32, zs.shape, 1)
        forget = (lane < FOX_H) | ((lane >= FOX_H + ML_H) & (lane < FOX_H + 2 * ML_H))
        zs_ref[...] = jnp.where(forget, _log_sigmoid(zs), zs)
    for j in range(nz):
        z_ref[j] = _dot(xn, w_ref[:, j * tn:(j + 1) * tn])
    for j in range(nt):
        _store_slab(t_refs[j], slab, _dot_nt(wt_ref[j], xn)[None])


def _layer_spec(l, *shape):
    return pl.BlockSpec((None,) + shape, lambda *_: (l,) + (0,) * len(shape))


def _norm_proj(x, l, P, nz, tn, rows_per_batch=None, t_prev=None, tm=ROW_TILE):
    M, K = x.shape
    tm = min(tm, M)
    depth = P["wbig"].shape[0]
    nt = 0 if rows_per_batch is None else P["wkv_t"].shape[1]
    n_alias = 0 if t_prev is None else nt
    aliases = {}
    in_specs = [
        pl.BlockSpec((tm, K), lambda i: (i, 0)),
        _layer_spec(l, 1, K),
        _layer_spec(l, K, nz * tn),
        _layer_spec(l, K, LANES),
        _layer_spec(l, 1, LANES),
    ]
    args = [x, P["g_pre_mix"], P["wbig"], P["wsmall"], P["bsmall"]]
    if nt:
        in_specs.append(_layer_spec(l, nt, tn, K))
        args.append(P["wkv_t"])
    out_shape = [jax.ShapeDtypeStruct((nz, M, tn), F32), jax.ShapeDtypeStruct((M, LANES), F32)]
    out_specs = [pl.BlockSpec((nz, tm, tn), lambda i: (0, i, 0)), pl.BlockSpec((tm, LANES), lambda i: (i, 0))]
    if nt:
        tpb = rows_per_batch // tm
        for j in range(nt):
            if t_prev is not None:
                in_specs.append(pl.BlockSpec(memory_space=pl.ANY))
                args.append(t_prev[j])
                aliases[len(args) - 1] = len(out_shape)
            out_shape.append(jax.ShapeDtypeStruct((depth, M // rows_per_batch, tn, rows_per_batch), F32))
            out_specs.append(_slab_spec(l, depth, t_prev is None, (1, tn, tm), lambda i: (i // tpb, 0, i % tpb)))
    slab = (l, depth) if t_prev is None else (0, 1)
    return pl.pallas_call(
        functools.partial(_norm_proj_kernel, tn=tn, nz=nz, with_gates=True, nt=nt, n_alias=n_alias, slab=slab),
        grid=(M // tm,),
        in_specs=in_specs,
        out_specs=out_specs,
        out_shape=out_shape,
        input_output_aliases=aliases,
        compiler_params=_cparams(("arbitrary",)),
        name="norm_proj",
    )(*args)


def _mem_proj_kernel(x_ref, g_ref, w_ref, *refs, slab):
    k_ref, v_ref = refs[-2:]
    li, nl = slab
    xn = _rms(x_ref[...], g_ref[...]).astype(BF16)
    n = x_ref.shape[0]
    for j, o_ref in enumerate((k_ref, v_ref)):
        for k in range(nl):
            if k != li:
                o_ref[k] = jnp.zeros(o_ref.shape[1:], F32)
        for h in range(X_H):
            col = j * X_W + h * X_DH
            o_ref[li, 0, pl.ds(h, n, stride=X_H), :] = _dot(xn, w_ref[:, col:col + X_DH])


def _mem_proj(x, l, P, B, prev=None):
    M, K = x.shape
    n_mem = M // B
    depth = P["w_xkv"].shape[0]
    in_specs = [pl.BlockSpec((n_mem, K), lambda b: (b, 0)), _layer_spec(l, 1, K), _layer_spec(l, K, 2 * X_W)]
    args = [x, P["g_mem"], P["w_xkv"]]
    aliases = {}
    if prev is not None:
        for j in range(2):
            in_specs.append(pl.BlockSpec(memory_space=pl.ANY))
            args.append(prev[j])
            aliases[len(args) - 1] = j
    shape = jax.ShapeDtypeStruct((depth, B, n_mem * X_H, X_DH), F32)
    spec = _slab_spec(l, depth, prev is None, (1, n_mem * X_H, X_DH), lambda b: (b, 0, 0))
    return pl.pallas_call(
        functools.partial(_mem_proj_kernel, slab=(l, depth) if prev is None else (0, 1)),
        grid=(B,),
        in_specs=in_specs,
        out_specs=[spec, spec],
        out_shape=[shape, shape],
        input_output_aliases=aliases,
        compiler_params=_cparams(("arbitrary",)),
        name="mem_proj",
    )(*args)


def _tri_incl(n, upper):
    r = lax.broadcasted_iota(jnp.int32, (n, n), 0)
    c = lax.broadcasted_iota(jnp.int32, (n, n), 1)
    return ((r <= c) if upper else (c <= r)).astype(BF16)


BIAS_ROWS = 16


def _fox_cum_kernel(zs_ref, ft_ref, *, nb):
    n = nb * FOX_H
    lf = jnp.concatenate([zs_ref[u * LANES:(u + 1) * LANES, :].T[0:FOX_H, :] for u in range(nb)], axis=0)
    cum = _dot3_lhs(lf, _tri_incl(LANES, True))
    r = lax.broadcasted_iota(jnp.int32, (n, n), 0)
    c = lax.broadcasted_iota(jnp.int32, (n, n), 1)
    earlier = ((c % FOX_H == r % FOX_H) & (c // FOX_H < r // FOX_H)).astype(BF16)
    prefix = _dot3_rhs(earlier, jnp.broadcast_to(cum[:, LANES - 1:LANES], (n, LANES)))
    pieces = _split3((cum + prefix) * (-LOG2E))
    ro = lax.broadcasted_iota(jnp.int32, (nb * (FOX_H // 2) * BIAS_ROWS, n), 0)
    co = lax.broadcasted_iota(jnp.int32, (nb * (FOX_H // 2) * BIAS_ROWS, n), 1)
    blk = ro // ((FOX_H // 2) * BIAS_ROWS) == co // FOX_H
    pair = (ro // BIAS_ROWS) % (FOX_H // 2)
    slot = ro % BIAS_ROWS
    head = co % FOX_H
    out = None
    for idx, piece in enumerate(pieces):
        sel = blk & (((slot == idx) & (head == 2 * pair)) | ((slot == 3 + idx) & (head == 2 * pair + 1)))
        term = _dot(sel.astype(BF16), piece)
        out = term if out is None else out + term
    ft_ref[0] = out.reshape(nb, FOX_H // 2, BIAS_ROWS, LANES)


def _fox_cum(zs, B, S):
    nb = S // LANES
    return pl.pallas_call(
        functools.partial(_fox_cum_kernel, nb=nb),
        grid=(B,),
        in_specs=[pl.BlockSpec((S, LANES), lambda b: (b, 0))],
        out_specs=pl.BlockSpec((1, nb, FOX_H // 2, BIAS_ROWS, LANES), lambda b: (b, 0, 0, 0, 0)),
        out_shape=jax.ShapeDtypeStruct((B, nb, FOX_H // 2, BIAS_ROWS, LANES), F32),
        compiler_params=_cparams(("arbitrary",)),
        name="fox_cum",
    )(zs)


def _fox_prompt_kernel(ti_ref, tj_ref, q_ref, kt_ref, vt_ref, ft_ref, o_ref, m_s, l_s, acc_s, *, tq):
    t = pl.program_id(1)
    i = ti_ref[t]
    j = tj_ref[t]
    sub = tq // LANES
    npair = FOX_H // 2
    low = lax.broadcasted_iota(jnp.int32, (tq, LANES), 1) < FOX_DH

    @pl.when(j == 0)
    def _():
        m_s[...] = jnp.full(m_s.shape, NEG, F32)
        l_s[...] = jnp.zeros(l_s.shape, F32)
        acc_s[...] = jnp.zeros(acc_s.shape, F32)

    def update(masked):
        if masked:
            causal = (lax.broadcasted_iota(jnp.int32, (tq, tq), 1)
                      <= lax.broadcasted_iota(jnp.int32, (tq, tq), 0))
        for p in range(npair):
            rows = slice(p * LANES, (p + 1) * LANES)
            qp = q_ref[0, :, rows] * (FOX_DH ** -0.5 * LOG2E)
            bias = jnp.concatenate([ft_ref[0, u, p] for u in range(sub)], axis=1).astype(BF16)
            ka = jnp.concatenate(
                [kt_ref[rows, :].astype(BF16), bias, jnp.zeros((LANES - BIAS_ROWS, tq), BF16)], axis=0)
            vb = vt_ref[rows, :].astype(BF16)
            for e in range(2):
                h = 2 * p + e
                qh = jnp.where(low if e == 0 else jnp.logical_not(low), qp, 0.0)
                ones = ((bias_lane >= 3 * e) & (bias_lane < 3 * e + 3)).astype(F32)
                qa = jnp.concatenate([qh, ones], axis=1).astype(BF16)
                s = _dot(qa, ka)
                if masked:
                    s = jnp.where(causal, s, NEG)
                sc = [s[:, c * LANES:(c + 1) * LANES] for c in range(sub)]
                mx = functools.reduce(jnp.maximum, sc)
                m_prev = m_s[h]
                m_new = jnp.maximum(m_prev, jnp.max(mx, axis=1, keepdims=True))
                alpha = jnp.exp2(m_prev - m_new)
                ps = [jnp.exp2(c - m_new) for c in sc]
                l_s[h] = alpha * l_s[h] + jnp.sum(functools.reduce(jnp.add, ps), axis=1, keepdims=True)
                pr = jnp.concatenate([c.astype(BF16) for c in ps], axis=1)
                acc_s[h] = alpha * acc_s[h] + _dot_nt(pr, vb)
                m_s[h] = m_new

    bias_lane = lax.broadcasted_iota(jnp.int32, (tq, LANES), 1)

    @pl.when(j < i)
    def _():
        update(False)

    @pl.when(j == i)
    def _():
        update(True)
        for p in range(npair):
            oe = acc_s[2 * p] / l_s[2 * p]
            oo = acc_s[2 * p + 1] / l_s[2 * p + 1]
            o_ref[:, p * LANES:(p + 1) * LANES] = jnp.where(low, oe, oo).astype(o_ref.dtype)


def _fox_prompt(z3, kt, vt, ft, B, S, iq, layer, tq=ATTN_TILE):
    tq = min(tq, S)
    nq = S // tq
    M = B * S
    sub = tq // LANES
    pairs = [(i, j) for i in range(nq) for j in range(i + 1)]
    ti = jnp.asarray([p[0] for p in pairs], jnp.int32)
    tj = jnp.asarray([p[1] for p in pairs], jnp.int32)
    grid_spec = pltpu.PrefetchScalarGridSpec(
        num_scalar_prefetch=2,
        grid=(B, len(pairs)),
        in_specs=[
            pl.BlockSpec((1, tq, FOX_W), lambda b, t, ti, tj: (iq, b * nq + ti[t], 0)),
            pl.BlockSpec((None, None, FOX_W, tq), lambda b, t, ti, tj: (layer, b, 0, tj[t])),
            pl.BlockSpec((None, None, FOX_W, tq), lambda b, t, ti, tj: (layer, b, 0, tj[t])),
            pl.BlockSpec((1, sub, FOX_H // 2, BIAS_ROWS, LANES), lambda b, t, ti, tj: (b, tj[t], 0, 0, 0)),
        ],
        out_specs=pl.BlockSpec((tq, FOX_W), lambda b, t, ti, tj: (b * nq + ti[t], 0)),
        scratch_shapes=[
            pltpu.VMEM((FOX_H, tq, LANES), F32),
            pltpu.VMEM((FOX_H, tq, LANES), F32),
            pltpu.VMEM((FOX_H, tq, LANES), F32),
        ],
    )
    return pl.pallas_call(
        functools.partial(_fox_prompt_kernel, tq=tq),
        grid_spec=grid_spec,
        out_shape=jax.ShapeDtypeStruct((M, FOX_W), BF16),
        compiler_params=_cparams(("arbitrary", "arbitrary")),
        name="fox_prompt",
    )(ti, tj, z3, kt, vt, ft)


SKIP_LOG = -110.0
KEY_SLOTS = 3


def _fox_decode_kernel(pt_ref, q_ref, kn_ref, vn_ref, zs_ref, lf_hbm, k_hbm, v_hbm, o_ref,
                       kbuf, vbuf, lfbuf, sbuf, m_ref, l_ref, acc_ref, need_ref, ksem, vsem, lfsem,
                       *, layer, npg, ng):
    b = pl.program_id(0)
    nreal = GROUP // 2
    rows = nreal * FOX_H
    n = npg * FOX_H

    def copies(hbm, buf, sem, g, slot, req=None):
        req = b if req is None else req
        return [pltpu.make_async_copy(hbm.at[layer, pt_ref[req, g * npg + i]], buf.at[slot, i], sem.at[slot])
                for i in range(npg)]

    def start_keys(g, slot, req=None):
        for c in copies(lf_hbm, lfbuf, lfsem, g, slot, req) + copies(k_hbm, kbuf, ksem, g, slot, req):
            c.start()

    order = [ng - 1 - it for it in range(ng)]
    ahead = min(KEY_SLOTS - 1, ng)

    @pl.when(b == 0)
    def _():
        for it in range(ahead):
            start_keys(order[it], it)

    hrow = lax.broadcasted_iota(jnp.int32, (FOX_H, FOX_W), 0)
    hcol = lax.broadcasted_iota(jnp.int32, (FOX_H, FOX_W), 1) // FOX_DH
    head_sel = jnp.concatenate([hrow == hcol] * nreal, axis=0)
    q = q_ref[...] * (FOX_DH ** -0.5)
    qbd = jnp.concatenate(
        [jnp.broadcast_to(q[nreal + t:nreal + t + 1, :], (FOX_H, FOX_W)) for t in range(nreal)], axis=0)
    qbd = jnp.where(head_sel, qbd, 0.0).astype(BF16)
    eye = (lax.broadcasted_iota(jnp.int32, (rows, rows), 0)
           == lax.broadcasted_iota(jnp.int32, (rows, rows), 1))

    def as_row(col):
        return jnp.sum(jnp.where(eye, jnp.broadcast_to(col, (rows, rows)), 0.0), axis=0, keepdims=True)

    zpad = jnp.zeros((LANES - GROUP, FOX_W), F32)
    kn = jnp.concatenate([kn_ref[0], zpad], axis=0).astype(BF16)
    vn_t = jnp.concatenate([vn_ref[0], zpad], axis=0).T.astype(BF16)
    zs = jnp.concatenate([zs_ref[...], jnp.zeros((LANES - GROUP, LANES), F32)], axis=0)
    lf_new = zs.T[0:FOX_H, :]
    kr = lax.broadcasted_iota(jnp.int32, (LANES, LANES), 0)
    kc = lax.broadcasted_iota(jnp.int32, (LANES, LANES), 1)
    inc = ((kr <= kc) & (kr >= nreal) & (kr < GROUP)).astype(BF16)
    f_new = jnp.concatenate([_dot3_lhs(lf_new, inc)] * nreal, axis=0)
    qi = lax.broadcasted_iota(jnp.int32, (rows, LANES), 0) // FOX_H
    kj = lax.broadcasted_iota(jnp.int32, (rows, LANES), 1) - nreal
    s = jnp.where((kj >= 0) & (kj <= qi), _dot_nt(qbd, kn) - f_new, NEG)
    m0 = jnp.max(s, axis=1, keepdims=True)
    p0 = jnp.exp(s - m0)
    m_ref[...] = m0
    l_ref[...] = jnp.sum(p0, axis=1, keepdims=True)
    acc_ref[...] = _dot_nt(vn_t, p0.astype(BF16))

    er = lax.broadcasted_iota(jnp.int32, (n, n), 0)
    ec = lax.broadcasted_iota(jnp.int32, (n, n), 1)
    later_pages = ((ec % FOX_H == er % FOX_H) & (ec // FOX_H > er // FOX_H)).astype(BF16)

    def update(g, slot):
        @pl.when(need_ref[slot] == 1)
        def _():
            for c in copies(v_hbm, vbuf, vsem, g, slot):
                c.wait()
            vcat = jnp.concatenate([vbuf[slot, i].astype(BF16) for i in range(npg)], axis=1)
            sg = sbuf[slot]
            m = m_ref[...]
            m_new = jnp.maximum(m, jnp.max(sg, axis=1, keepdims=True))
            alpha = jnp.exp(m - m_new)
            pr = jnp.exp(sg - m_new)
            l_ref[...] = alpha * l_ref[...] + jnp.sum(pr, axis=1, keepdims=True)
            acc_ref[...] = as_row(alpha) * acc_ref[...] + _dot_nt(vcat, pr.astype(BF16))
            m_ref[...] = m_new

    carry = jnp.zeros((FOX_H, 1), F32)
    for it, g in enumerate(order):
        slot = it % 2
        kslot = it % KEY_SLOTS
        if it + KEY_SLOTS - 1 < ng:
            start_keys(order[it + KEY_SLOTS - 1], (it + KEY_SLOTS - 1) % KEY_SLOTS)
        for c in copies(lf_hbm, lfbuf, lfsem, g, kslot):
            c.wait()
        lf = jnp.concatenate([lfbuf[kslot, i] for i in range(npg)], axis=0)
        incl = _dot3_lhs(lf, _tri_incl(LANES, False))
        later = _dot3_rhs(later_pages, jnp.broadcast_to(incl[:, 0:1], (n, LANES)))
        after = incl - lf + later + jnp.concatenate([jnp.broadcast_to(carry, (FOX_H, LANES))] * npg, axis=0)
        carry = after[0:FOX_H, 0:1] + lf[0:FOX_H, 0:1]
        gcat = jnp.concatenate([after[i * FOX_H:(i + 1) * FOX_H, :] for i in range(npg)], axis=1)
        for c in copies(k_hbm, kbuf, ksem, g, kslot):
            c.wait()
        kcat = jnp.concatenate([kbuf[kslot, i].astype(BF16) for i in range(npg)], axis=1)
        sg = _dot(qbd, kcat) + jnp.concatenate([gcat] * nreal, axis=0)
        sbuf[slot] = sg
        need = jnp.max(jnp.max(sg, axis=1, keepdims=True) - m_ref[...]) > SKIP_LOG
        need_ref[slot] = need.astype(jnp.int32)

        @pl.when(need)
        def _():
            for c in copies(v_hbm, vbuf, vsem, g, slot):
                c.start()

        if it >= 1:
            update(order[it - 1], 1 - slot)

    @pl.when(b + 1 < pl.num_programs(0))
    def _():
        for it in range(ahead):
            start_keys(order[it], it, req=b + 1)

    update(order[ng - 1], (ng - 1) % 2)

    out_t = acc_ref[...] / as_row(l_ref[...])
    own = (lax.broadcasted_iota(jnp.int32, (FOX_W, rows), 0) // FOX_DH
           == lax.broadcasted_iota(jnp.int32, (FOX_W, rows), 1) % FOX_H)
    out_t = jnp.where(own, out_t, 0.0).astype(BF16)
    gr = lax.broadcasted_iota(jnp.int32, (GROUP, rows), 0) - nreal
    gc = lax.broadcasted_iota(jnp.int32, (GROUP, rows), 1) // FOX_H
    o_ref[...] = _dot_nt((gr == gc).astype(BF16), out_t)


def _fox_decode(z3, zs, cache_k, cache_v, cache_lf, page_table, layer, iq, ik, iv, npg=DECODE_PAGES):
    Bd, n_pages = page_table.shape
    npg = min(npg, n_pages)
    ng = n_pages // npg
    rows = (GROUP // 2) * FOX_H
    any_spec = pl.BlockSpec(memory_space=pl.ANY)
    grid_spec = pltpu.PrefetchScalarGridSpec(
        num_scalar_prefetch=1,
        grid=(Bd,),
        in_specs=[
            pl.BlockSpec((None, GROUP, FOX_W), lambda b, pt: (iq, b, 0)),
            pl.BlockSpec((1, GROUP, FOX_W), lambda b, pt: (ik, b, 0)),
            pl.BlockSpec((1, GROUP, FOX_W), lambda b, pt: (iv, b, 0)),
            pl.BlockSpec((GROUP, LANES), lambda b, pt: (b, 0)),
            any_spec, any_spec, any_spec,
        ],
        out_specs=pl.BlockSpec((GROUP, FOX_W), lambda b, pt: (b, 0)),
        scratch_shapes=[
            pltpu.VMEM((KEY_SLOTS, npg, FOX_W, PAGE), F32),
            pltpu.VMEM((2, npg, FOX_W, PAGE), F32),
            pltpu.VMEM((KEY_SLOTS, npg, FOX_H, PAGE), F32),
            pltpu.VMEM((2, rows, npg * PAGE), F32),
            pltpu.VMEM((rows, 1), F32),
            pltpu.VMEM((rows, 1), F32),
            pltpu.VMEM((FOX_W, rows), F32),
            pltpu.SMEM((2,), jnp.int32),
            pltpu.SemaphoreType.DMA((KEY_SLOTS,)),
            pltpu.SemaphoreType.DMA((2,)),
            pltpu.SemaphoreType.DMA((KEY_SLOTS,)),
        ],
    )
    return pl.pallas_call(
        functools.partial(_fox_decode_kernel, layer=layer, npg=npg, ng=ng),
        grid_spec=grid_spec,
        out_shape=jax.ShapeDtypeStruct((Bd * GROUP, FOX_W), F32),
        compiler_params=_cparams(("arbitrary",)),
        name="fox_decode",
    )(page_table, z3, z3, z3, zs, cache_lf, cache_k, cache_v)


def _mlstm_kernel(*refs, rin, sample, slab):
    qk_ref, v_ref, og_ref, zs_ref, wc_ref, bc_ref, gh_ref = refs[0:7]
    if sample:
        inj_ref, c0_ref, nm0_ref = refs[7:10]
    h_ref, c_out_ref, nm_out_ref, xbuf, c_s, nm_s = refs[-6:]
    t = pl.program_id(1)
    rc = LANES

    @pl.when(t == 0)
    def _():
        xbuf[0:8, :] = jnp.zeros((8, 2 * ML_W), F32)
        if sample:
            c_s[...] = c0_ref[0]
            nm_s[...] = nm0_ref[0]
        else:
            c_s[...] = jnp.zeros(c_s.shape, F32)
            nm_s[...] = jnp.zeros(nm_s.shape, F32)

    x = jnp.concatenate([qk_ref[0], qk_ref[1]], axis=1)
    if sample:
        slot = lax.broadcasted_iota(jnp.int32, (rin, 1), 0)
        x = jnp.where((slot >= 1) & (slot < GROUP // 2), inj_ref[...], x)
    xbuf[8:8 + rin, :] = x
    y = bc_ref[...] + wc_ref[3:4, :] * x
    for jj in range(ML_CONV - 1):
        y = y + wc_ref[jj:jj + 1, :] * xbuf[pl.ds(8 - (ML_CONV - 1) + jj, rin), :]
    xbuf[0:8, :] = xbuf[rin:rin + 8, :]
    y = y * _sigmoid(y)
    q_all = y[:, 0:ML_W]
    k_all = y[:, ML_W:] * (ML_DH ** -0.5)
    v_all = v_ref[0]
    zs = zs_ref[...]
    if rin < rc:
        pad = jnp.zeros((rc - rin, ML_W), F32)
        q_all = jnp.concatenate([q_all, pad], axis=0)
        k_all = jnp.concatenate([k_all, pad], axis=0)
        v_all = jnp.concatenate([v_all, pad], axis=0)
        zs = jnp.concatenate([zs, jnp.zeros((rc - rin, LANES), F32)], axis=0)
    if sample:
        row = lax.broadcasted_iota(jnp.int32, (rc, LANES), 0)
        lane = lax.broadcasted_iota(jnp.int32, (rc, LANES), 1)
        padrow = (row < GROUP // 2) | (row >= GROUP)
        is_i = (lane >= FOX_H) & (lane < FOX_H + ML_H)
        zs = jnp.where(padrow, jnp.where(is_i, NEG, 0.0), zs)

    causal = (lax.broadcasted_iota(jnp.int32, (rc, rc), 1) <= lax.broadcasted_iota(jnp.int32, (rc, rc), 0))
    for u in range(max(rin // rc, 1)):
        rs = slice(u * rc, (u + 1) * rc)
        zs_u = zs[rs, :]
        colcum = _dot3_rhs(_tri_incl(rc, False), zs_u)
        rowraw = zs_u.T[FOX_H:FOX_H + 2 * ML_H, :]
        rowcum = _dot3_lhs(rowraw, _tri_incl(rc, True))

        heads = []
        for h in range(ML_H):
            cs = slice(h * ML_DH, (h + 1) * ML_DH)
            qf = q_all[rs, cs]
            kf = k_all[rs, cs]
            qh = qf.astype(BF16)
            kh = kf.astype(BF16)
            vh = v_all[rs, cs].astype(BF16)
            icol = jnp.broadcast_to(zs_u[:, FOX_H + h:FOX_H + h + 1], (rc, LANES))
            bcol = jnp.broadcast_to(colcum[:, FOX_H + ML_H + h:FOX_H + ML_H + h + 1], (rc, LANES))
            irow = rowraw[h:h + 1, :]
            brow = rowcum[ML_H + h:ML_H + h + 1, :]
            cmat = c_s[h]
            nrow = nm_s[h, 0:1, :]
            m = nm_s[h, 1:2, :]

            dlog = jnp.where(causal, bcol - brow + irow, NEG)
            g = bcol + m
            mt = jnp.maximum(g, jnp.max(dlog, axis=1, keepdims=True))
            w_intra = jnp.exp(dlog - mt)
            w_inter = jnp.exp(g - mt)
            sc = _dot_nt(qh, kh) * w_intra
            num = _dot(sc.astype(BF16), vh) + _dot(qh, cmat.astype(BF16)) * w_inter
            den = jnp.sum(sc, axis=1, keepdims=True) + w_inter * jnp.sum(qf * nrow, axis=1, keepdims=True)
            hh = num / jnp.maximum(jnp.abs(den), jnp.exp(-mt))

            b_last = bcol[rc - 1:rc, :]
            a = b_last - bcol + icol
            m_new = jnp.maximum(b_last + m, jnp.max(a, axis=0, keepdims=True))
            ws = jnp.exp(a - m_new)
            decay = jnp.exp(b_last + m - m_new)
            kw = kf * ws
            c_s[h] = decay * cmat + _dot_tn(kw.astype(BF16), vh)
            nm_s[h, 0:1, :] = decay * nrow + jnp.sum(kw, axis=0, keepdims=True)
            nm_s[h, 1:2, :] = m_new

            heads.append(_rms(hh, gh_ref[:, cs]))
        hfull = jnp.concatenate(heads, axis=1)
        if sample:
            h_ref[...] = (hfull[0:rin] * _sigmoid(og_ref[0])).astype(h_ref.dtype)
        else:
            h_ref[rs, :] = (hfull * _sigmoid(og_ref[0, rs, :])).astype(h_ref.dtype)

    @pl.when(t == pl.num_programs(1) - 1)
    def _():
        _store_slab(c_out_ref, slab, c_s[...][None])
        nm_out_ref[0] = nm_s[...]


def _mlstm(z3, zs, l, P, B, L, iqk, iv, io, state=None, c_prev=None):
    sample = state is not None
    rin = GROUP if sample else math.gcd(L, 2 * LANES)
    nt = L // rin
    M = B * L
    depth = P["w_ml_conv"].shape[0]
    in_specs = [
        pl.BlockSpec((2, rin, ML_W), lambda b, t: (iqk, b * nt + t, 0)),
        pl.BlockSpec((1, rin, ML_W), lambda b, t: (iv, b * nt + t, 0)),
        pl.BlockSpec((1, rin, ML_W), lambda b, t: (io, b * nt + t, 0)),
        pl.BlockSpec((rin, LANES), lambda b, t: (b * nt + t, 0)),
        _layer_spec(l, ML_CONV, 2 * ML_W),
        _layer_spec(l, 1, 2 * ML_W),
        _layer_spec(l, 1, ML_W),
    ]
    args = [z3, z3, z3, zs, P["w_ml_conv"], P["b_ml_conv"], P["g_ml_head"]]
    if sample:
        in_specs += [
            pl.BlockSpec((None, rin, 2 * ML_W), lambda b, t: (l, b * nt + t, 0)),
            pl.BlockSpec((None, 1, ML_H, ML_DH, ML_DH), lambda b, t: (l, b, 0, 0, 0)),
            pl.BlockSpec((None, 1, ML_H, 8, ML_DH), lambda b, t: (l, b, 0, 0, 0)),
        ]
        args += list(state)
    aliases = {}
    if c_prev is not None:
        in_specs.append(pl.BlockSpec(memory_space=pl.ANY))
        args.append(c_prev)
        aliases[len(args) - 1] = 1
    return pl.pallas_call(
        functools.partial(_mlstm_kernel, rin=rin, sample=sample,
                          slab=(l, depth) if c_prev is None else (0, 1)),
        grid=(B, nt),
        in_specs=in_specs,
        out_specs=[
            pl.BlockSpec((rin, ML_W), lambda b, t: (b * nt + t, 0)),
            _slab_spec(l, depth, c_prev is None, (1, ML_H, ML_DH, ML_DH), lambda b, t: (b, 0, 0, 0)),
            pl.BlockSpec((1, ML_H, 8, ML_DH), lambda b, t: (b, 0, 0, 0)),
        ],
        out_shape=[
            jax.ShapeDtypeStruct((M, ML_W), F32 if sample else BF16),
            jax.ShapeDtypeStruct((depth, B, ML_H, ML_DH, ML_DH), F32),
            jax.ShapeDtypeStruct((B, ML_H, 8, ML_DH), F32),
        ],
        input_output_aliases=aliases,
        scratch_shapes=[
            pltpu.VMEM((rin + 8, 2 * ML_W), F32),
            pltpu.VMEM((ML_H, ML_DH, ML_DH), F32),
            pltpu.VMEM((ML_H, 8, ML_DH), F32),
        ],
        compiler_params=_cparams(("arbitrary", "arbitrary")),
        name="mlstm_sample" if sample else "mlstm_prompt",
    )(*args)


def _mix_xattn_kernel(fo_ref, hm_ref, x_ref, wo_ref, gpm_ref, gpx_ref, wq_ref, mk_ref, mv_ref, wxo_ref, gox_ref, o_ref):
    mix = (_dot(fo_ref[...].astype(BF16), wo_ref[0:FOX_W, :])
           + _dot(hm_ref[...].astype(BF16), wo_ref[FOX_W:, :]))
    x1 = x_ref[...] + _rms(mix, gpm_ref[...])
    hq = _rms(x1, gpx_ref[...]).astype(BF16)
    xq = _dot(hq, wq_ref[...])
    nb = mk_ref.shape[0]
    rt = xq.shape[0] // nb
    n_mem = mk_ref.shape[1] // X_H
    parts = []
    for j in range(nb):
        heads = []
        for h in range(X_H):
            cs = slice(h * X_DH, (h + 1) * X_DH)
            mk = mk_ref[j, pl.ds(h, n_mem, stride=X_H), :].astype(BF16)
            mv = mv_ref[j, pl.ds(h, n_mem, stride=X_H), :].astype(BF16)
            s = _dot_nt(xq[j * rt:(j + 1) * rt, cs].astype(BF16), mk) * (X_DH ** -0.5)
            e = jnp.exp(s - jnp.max(s, axis=1, keepdims=True))
            heads.append(_dot(e.astype(BF16), mv) / jnp.sum(e, axis=1, keepdims=True))
        parts.append(jnp.concatenate(heads, axis=1))
    xo = _dot((parts[0] if nb == 1 else jnp.concatenate(parts, axis=0)).astype(BF16), wxo_ref[...])
    o_ref[...] = x1 + _rms(xo, gox_ref[...])


def _mix_xattn(fo, hm, x, l, P, mk, mv, rows_per_batch, tm):
    M, D = x.shape
    if tm <= rows_per_batch:
        tpb = rows_per_batch // tm
        mem_spec = pl.BlockSpec((None, 1) + mk.shape[2:], lambda i: (l, i // tpb, 0, 0))
    else:
        nb = tm // rows_per_batch
        mem_spec = pl.BlockSpec((None, nb) + mk.shape[2:], lambda i: (l, i, 0, 0))
    return pl.pallas_call(
        _mix_xattn_kernel,
        grid=(M // tm,),
        in_specs=[
            pl.BlockSpec((tm, FOX_W), lambda i: (i, 0)),
            pl.BlockSpec((tm, ML_W), lambda i: (i, 0)),
            pl.BlockSpec((tm, D), lambda i: (i, 0)),
            _layer_spec(l, FOX_W + ML_W, D),
            _layer_spec(l, 1, D), _layer_spec(l, 1, D),
            _layer_spec(l, D, X_W),
            mem_spec, mem_spec,
            _layer_spec(l, X_W, D),
            _layer_spec(l, 1, D),
        ],
        out_specs=pl.BlockSpec((tm, D), lambda i: (i, 0)),
        out_shape=jax.ShapeDtypeStruct((M, D), F32),
        compiler_params=_cparams(("arbitrary",)),
        name="mix_xattn",
    )(fo, hm, x, P["w_out"], P["g_post_mix"], P["g_pre_x"], P["w_xq"], mk, mv, P["w_xo"], P["g_post_x"])


def _convffn_kernel(*refs, tm, tiles_per_batch, sample):
    if sample:
        (x_ref, gpre_ref, wg_ref, wu_ref, wc_ref, bc_ref, wd_ref, gpost_ref, inj_ref,
         o_ref, gate_ref, xn_ref, acc_ref, gbuf, tails) = refs
    else:
        (x_ref, gpre_ref, wg_ref, wu_ref, wc_ref, bc_ref, wd_ref, gpost_ref,
         o_ref, gate_ref, xn_ref, acc_ref, gbuf, tails) = refs
    i = pl.program_id(0)
    j = pl.program_id(1)

    @pl.when(j == 0)
    def _():
        xn_ref[...] = _rms(x_ref[...], gpre_ref[...]).astype(BF16)
        acc_ref[...] = jnp.zeros_like(acc_ref)

    xn = xn_ref[...]
    g = _dot(xn, wg_ref[...])
    if sample:
        slot = lax.broadcasted_iota(jnp.int32, (tm, 1), 0) % GROUP
        g = jnp.where((slot >= GROUP // 2 - (FFN_CONV - 1)) & (slot < GROUP // 2), inj_ref[...], g)
        gate_ref[...] = g
        gbuf[0:8, :] = jnp.zeros((8, g.shape[1]), F32)
    else:
        first = (i % tiles_per_batch) == 0
        gbuf[0:8, :] = jnp.where(first, 0.0, tails[j])
        gate_ref[0] = g[tm - 8:tm, :]
        tails[j] = g[tm - 8:tm, :]
    gbuf[8:8 + tm, :] = g
    t = bc_ref[...] + wc_ref[2:3, :] * g
    for jj in range(FFN_CONV - 1):
        t = t + wc_ref[jj:jj + 1, :] * gbuf[pl.ds(8 - (FFN_CONV - 1) + jj, tm), :]
    a = _gelu_tanh(t) * _dot(xn, wu_ref[...])
    acc_ref[...] += _dot(a.astype(BF16), wd_ref[...])

    @pl.when(j == pl.num_programs(1) - 1)
    def _():
        o_ref[...] = x_ref[...] + _rms(acc_ref[...], gpost_ref[...])


def _convffn(x, l, P, rows_per_batch, tm, tn, inj=None):
    M, D = x.shape
    F = P["w_gate"].shape[2]
    sample = inj is not None
    nj = F // tn
    tiles_per_batch = max(rows_per_batch // tm, 1)
    in_specs = [
        pl.BlockSpec((tm, D), lambda i, j: (i, 0)),
        _layer_spec(l, 1, D),
        pl.BlockSpec((None, D, tn), lambda i, j: (l, 0, j)),
        pl.BlockSpec((None, D, tn), lambda i, j: (l, 0, j)),
        pl.BlockSpec((None, FFN_CONV, tn), lambda i, j: (l, 0, j)),
        pl.BlockSpec((None, 1, tn), lambda i, j: (l, 0, j)),
        pl.BlockSpec((None, tn, D), lambda i, j: (l, j, 0)),
        _layer_spec(l, 1, D),
    ]
    args = [x, P["g_pre_ffn"], P["w_gate"], P["w_up"], P["w_ffn_conv"], P["b_ffn_conv"], P["w_down"],
            P["g_post_ffn"]]
    if sample:
        in_specs.append(pl.BlockSpec((None, tm, tn), lambda i, j: (l, i, j)))
        args.append(inj)
        gate_shape = jax.ShapeDtypeStruct((M, F), F32)
        gate_spec = pl.BlockSpec((tm, tn), lambda i, j: (i, j))
    else:
        gate_shape = jax.ShapeDtypeStruct((M // tm, 8, F), F32)
        gate_spec = pl.BlockSpec((1, 8, tn), lambda i, j: (i, 0, j))
    return pl.pallas_call(
        functools.partial(_convffn_kernel, tm=tm, tiles_per_batch=tiles_per_batch, sample=sample),
        grid=(M // tm, nj),
        in_specs=in_specs,
        out_specs=[pl.BlockSpec((tm, D), lambda i, j: (i, 0)), gate_spec],
        out_shape=[jax.ShapeDtypeStruct((M, D), F32), gate_shape],
        scratch_shapes=[
            pltpu.VMEM((tm, D), BF16),
            pltpu.VMEM((tm, D), F32),
            pltpu.VMEM((tm + 8, tn), F32),
            pltpu.VMEM((nj, 8, tn), F32),
        ],
        compiler_params=_cparams(("arbitrary", "arbitrary")),
        name="convffn_sample" if sample else "convffn_prompt",
    )(*args)


_IQK, _IMV, _IMO, _IFQ, _IFK, _IFV = 0, 2, 3, 4, 5, 6


def _prep_params(w_in, b_fox_f, b_ml_i, b_ml_f, w_out, w_xq, w_xk, w_xv, w_xo, w_gate, w_up, w_down, vectors):
    depth, K, _ = w_in.shape
    o_ff = 3 * FOX_W
    o_mq = o_ff + FOX_H
    o_mi = o_mq + 3 * ML_W
    o_mo = o_mi + 2 * ML_H
    n_gate = FOX_H + 2 * ML_H
    P = dict(
        wbig=jnp.concatenate([w_in[:, :, o_mq:o_mi], w_in[:, :, o_mo:], w_in[:, :, :o_ff]], axis=2).astype(BF16),
        wsmall=jnp.concatenate([w_in[:, :, o_ff:o_mq], w_in[:, :, o_mi:o_mo],
                                jnp.zeros((depth, K, LANES - n_gate), F32)], axis=2).astype(BF16),
        bsmall=jnp.concatenate([b_fox_f, b_ml_i, b_ml_f, jnp.zeros((depth, LANES - n_gate), F32)],
                               axis=1).reshape(depth, 1, LANES),
        wkv_t=jnp.transpose(w_in[:, :, FOX_W:o_ff].reshape(depth, K, 2, FOX_W), (0, 2, 3, 1)).astype(BF16),
        w_out=w_out.astype(BF16), w_xq=w_xq.astype(BF16),
        w_xkv=jnp.concatenate([w_xk, w_xv], axis=2).astype(BF16),
        w_xo=w_xo.astype(BF16), w_gate=w_gate.astype(BF16), w_up=w_up.astype(BF16), w_down=w_down.astype(BF16))
    for name, v in vectors.items():
        P[name] = v.reshape(depth, 1, v.shape[-1]) if v.ndim == 2 else v
    return P


def kernel(x_prompt, x_sample, mem_prompt, cache_fox_k, cache_fox_v, cache_fox_logf, state_ml_C, state_ml_n, state_ml_m, state_ml_conv, state_ffn_conv, cache_mem_k, cache_mem_v, page_table, g_pre_mix, w_in, b_fox_f, w_ml_conv, b_ml_conv, b_ml_i, b_ml_f, g_ml_head, w_out, g_post_mix, g_pre_x, g_mem, w_xq, w_xk, w_xv, w_xo, g_post_x, g_pre_ffn, w_gate, w_up, w_ffn_conv, b_ffn_conv, w_down, g_post_ffn):
    B, S, D = x_prompt.shape
    Bd, Ld, _ = x_sample.shape
    depth = w_in.shape[0]
    n_mem = mem_prompt.shape[1]
    d_ff = w_gate.shape[2]
    pool = cache_fox_k.shape[1]
    pad = GROUP - Ld
    assert Ld == GROUP // 2 and S % 256 == 0

    xp = x_prompt.reshape(B * S, D)
    xs = jnp.pad(x_sample, ((0, 0), (pad, 0), (0, 0))).reshape(Bd * GROUP, D)
    memx = mem_prompt.reshape(B * n_mem, D)
    ck = jnp.transpose(cache_fox_k, (0, 1, 3, 4, 2)).reshape(depth, pool, FOX_W, PAGE)
    cv = jnp.transpose(cache_fox_v, (0, 1, 3, 4, 2)).reshape(depth, pool, FOX_W, PAGE)
    clf = jnp.transpose(cache_fox_logf, (0, 1, 3, 2))
    ffn_tn = d_ff // 2 if (d_ff // 2) % LANES == 0 else d_ff

    P = _prep_params(
        w_in, b_fox_f, b_ml_i, b_ml_f, w_out, w_xq, w_xk, w_xv, w_xo, w_gate, w_up, w_down,
        dict(g_pre_mix=g_pre_mix, g_mem=g_mem, w_ml_conv=w_ml_conv, b_ml_conv=b_ml_conv, g_ml_head=g_ml_head,
             g_post_mix=g_post_mix, g_pre_x=g_pre_x, g_post_x=g_post_x, g_pre_ffn=g_pre_ffn,
             w_ffn_conv=w_ffn_conv, b_ffn_conv=b_ffn_conv, g_post_ffn=g_post_ffn))
    inj_ml = jnp.pad(state_ml_conv, ((0, 0), (0, 0), (1, GROUP - ML_CONV), (0, 0))).reshape(depth, Bd * GROUP, 2 * ML_W)
    inj_ffn = jnp.pad(state_ffn_conv, ((0, 0), (0, 0), (GROUP // 2 - (FFN_CONV - 1), GROUP // 2), (0, 0)))
    inj_ffn = inj_ffn.reshape(depth, Bd * GROUP, d_ff)
    nm0 = jnp.concatenate(
        [state_ml_n[:, :, :, None, :],
         jnp.broadcast_to(state_ml_m[:, :, :, None, None], (depth, Bd, ML_H, 1, ML_DH)),
         jnp.zeros((depth, Bd, ML_H, 6, ML_DH), F32)], axis=3)
    smk = cache_mem_k.reshape(depth, Bd, n_mem * X_H, X_DH)
    smv = cache_mem_v.reshape(depth, Bd, n_mem * X_H, X_DH)

    outs = {k: [] for k in ("lf_p", "fk_s", "fv_s", "lf_s", "nm_p", "cv_p", "nm_s", "cv_s", "fc_p", "fc_s")}
    kv_p = mem_p = c_p = c_s = None
    for l in range(depth):
        mem_p = _mem_proj(memx, l, P, B, prev=mem_p)
        z3, zs, *kv_p = _norm_proj(xp, l, P, _IFK, FOX_W, rows_per_batch=S, t_prev=kv_p)
        ft = _fox_cum(zs, B, S)
        fo = _fox_prompt(z3, kv_p[0], kv_p[1], ft, B, S, _IFQ, l)
        hm, c_p, nm_new = _mlstm(z3, zs, l, P, B, S, _IQK, _IMV, _IMO, c_prev=c_p)
        xp = _mix_xattn(fo, hm, xp, l, P, mem_p[0], mem_p[1], S, min(2 * ROW_TILE, S))
        xp, gate_tail = _convffn(xp, l, P, S, min(ROW_TILE, S), ffn_tn)
        outs["lf_p"].append(zs[:, :FOX_H].reshape(B, S, FOX_H))
        outs["nm_p"].append(nm_new)
        tails = z3.reshape(z3.shape[0], B, S, ML_W)[0:2, :, S - (ML_CONV - 1):, :]
        outs["cv_p"].append(jnp.transpose(tails, (1, 2, 0, 3)).reshape(B, ML_CONV - 1, 2 * ML_W))
        outs["fc_p"].append(gate_tail.reshape(B, -1, 8, d_ff)[:, -1, 8 - (FFN_CONV - 1):, :])

        z3, zs = _norm_proj(xs, l, P, _IFV + 1, FOX_W)
        fo = _fox_decode(z3, zs, ck, cv, clf, page_table, l, _IFQ, _IFK, _IFV)
        hm, c_s, nm_new = _mlstm(z3, zs, l, P, Bd, GROUP, _IQK, _IMV, _IMO,
                                 state=(inj_ml, state_ml_C, nm0), c_prev=c_s)
        xs = _mix_xattn(fo, hm, xs, l, P, smk, smv, GROUP, GROUP * math.gcd(Bd, 8))
        xs, gate_full = _convffn(xs, l, P, Bd * GROUP, Bd * GROUP, ffn_tn, inj=inj_ffn)
        real = lambda a: a.reshape((Bd, GROUP) + a.shape[1:])[:, pad:]
        outs["fk_s"].append(real(z3[_IFK]).reshape(Bd, Ld, FOX_H, FOX_DH))
        outs["fv_s"].append(real(z3[_IFV]).reshape(Bd, Ld, FOX_H, FOX_DH))
        outs["lf_s"].append(real(zs)[:, :, :FOX_H])
        outs["nm_s"].append(nm_new)
        tails = z3.reshape(z3.shape[0], Bd, GROUP, ML_W)[0:2, :, GROUP - (ML_CONV - 1):, :]
        outs["cv_s"].append(jnp.transpose(tails, (1, 2, 0, 3)).reshape(Bd, ML_CONV - 1, 2 * ML_W))
        outs["fc_s"].append(gate_full.reshape(Bd, GROUP, d_ff)[:, GROUP - (FFN_CONV - 1):, :])

    st = lambda k: jnp.stack(outs[k], axis=0)
    y_p = xp.reshape(B, S, D)
    y_s = xs.reshape(Bd, GROUP, D)[:, pad:]
    to_tokens = lambda a: jnp.transpose(a.reshape(depth, B, FOX_H, FOX_DH, S), (0, 1, 4, 2, 3))
    nm_p, nm_s = st("nm_p"), st("nm_s")
    mem_leaf = lambda a: a.reshape(depth, B, n_mem, X_H, X_DH)
    return (y_p, y_s, to_tokens(kv_p[0]), to_tokens(kv_p[1]), st("lf_p"), st("fk_s"), st("fv_s"), st("lf_s"),
            c_p, nm_p[:, :, :, 0, :], nm_p[:, :, :, 1, 0], st("cv_p"),
            c_s, nm_s[:, :, :, 0, :], nm_s[:, :, :, 1, 0], st("cv_s"),
            st("fc_p"), st("fc_s"), mem_leaf(mem_p[0]), mem_leaf(mem_p[1]))
```

```python
import functools
import math

import jax
import jax.numpy as jnp
from jax import lax
from jax.experimental import pallas as pl
from jax.experimental.pallas import tpu as pltpu

FOX_H, FOX_DH = 8, 64
FOX_W = FOX_H * FOX_DH
ML_H, ML_DH = 4, 128
ML_W = ML_H * ML_DH
ML_CONV = 4
X_H, X_DH = 4, 128
X_W = X_H * X_DH
FFN_CONV = 3
EPS = 1e-6
PAGE = 128
NEG = -1e30
LOG2E = 1.4426950408889634
GROUP = 8
LANES = 128
VMEM_LIMIT = 56 * 1024 * 1024
ROW_TILE = 512
ATTN_TILE = 512
DECODE_PAGES = 16

F32 = jnp.float32
BF16 = jnp.bfloat16


def _cparams(sem):
    return pltpu.CompilerParams(dimension_semantics=sem, vmem_limit_bytes=VMEM_LIMIT)


def _split3(x):
    hi = x.astype(BF16)
    r1 = x - hi.astype(F32)
    mid = r1.astype(BF16)
    lo = (r1 - mid.astype(F32)).astype(BF16)
    return hi, mid, lo


def _dot(a, b):
    return jnp.dot(a, b, preferred_element_type=F32)


def _dot_nt(a, b):
    return lax.dot_general(a, b, (((1,), (1,)), ((), ())), preferred_element_type=F32)


def _dot_tn(a, b):
    return lax.dot_general(a, b, (((0,), (0,)), ((), ())), preferred_element_type=F32)


def _dot3_lhs(x, w):
    hi, mid, lo = _split3(x)
    return _dot(hi, w) + _dot(mid, w) + _dot(lo, w)


def _dot3_rhs(w, x):
    hi, mid, lo = _split3(x)
    return _dot(w, hi) + _dot(w, mid) + _dot(w, lo)


def _rms(x, g):
    return x * lax.rsqrt(jnp.mean(x * x, axis=-1, keepdims=True) + EPS) * g


def _log_sigmoid(x):
    return jnp.minimum(x, 0.0) - jnp.log1p(jnp.exp(-jnp.abs(x)))


def _sigmoid(x):
    return 1.0 / (1.0 + jnp.exp(-x))


def _gelu_tanh(x):
    c = 0.7978845608028654
    return x * (0.5 * (1.0 + jnp.tanh(c * (x + 0.044715 * (x * x * x)))))


def _slab_spec(l, depth, first, block, index_map):
    if first:
        return pl.BlockSpec((depth,) + block, lambda *a: (0,) + tuple(index_map(*a)))
    return pl.BlockSpec((1,) + block, lambda *a: (l,) + tuple(index_map(*a)))


def _store_slab(ref, slab, value):
    li, nl = slab
    for k in range(nl):
        ref[k] = value if k == li else jnp.zeros_like(value)


def _norm_proj_kernel(*refs, tn, nz, with_gates, nt, n_alias, slab):
    refs = list(refs)
    x_ref, g_ref, w_ref = refs[0:3]
    pos = 3
    if with_gates:
        ws_ref, bs_ref = refs[pos:pos + 2]
        pos += 2
    if nt:
        wt_ref = refs[pos]
        pos += 1
    pos += n_alias
    z_ref = refs[pos]
    pos += 1
    if with_gates:
        zs_ref = refs[pos]
        pos += 1
    t_refs = refs[pos:pos + nt]

    xn = _rms(x_ref[...], g_ref[...]).astype(BF16)
    if with_gates:
        zs = _dot(xn, ws_ref[...]) + bs_ref[...]
        lane = lax.broadcasted_iota(jnp.int32, zs.shape, 1)
        forget = (lane < FOX_H) | ((lane >= FOX_H + ML_H) & (lane < FOX_H + 2 * ML_H))
        zs_ref[...] = jnp.where(forget, _log_sigmoid(zs), zs)
    for j in range(nz):
        z_ref[j] = _dot(xn, w_ref[:, j * tn:(j + 1) * tn])
    for j in range(nt):
        _store_slab(t_refs[j], slab, _dot_nt(wt_ref[j], xn)[None])


def _layer_spec(l, *shape):
    return pl.BlockSpec((None,) + shape, lambda *_: (l,) + (0,) * len(shape))


def _norm_proj(x, l, P, nz, tn, rows_per_batch=None, t_prev=None, tm=ROW_TILE):
    M, K = x.shape
    tm = min(tm, M)
    depth = P["wbig"].shape[0]
    nt = 0 if rows_per_batch is None else P["wkv_t"].shape[1]
    n_alias = 0 if t_prev is None else nt
    aliases = {}
    in_specs = [
        pl.BlockSpec((tm, K), lambda i: (i, 0)),
        _layer_spec(l, 1, K),
        _layer_spec(l, K, nz * tn),
        _layer_spec(l, K, LANES),
        _layer_spec(l, 1, LANES),
    ]
    args = [x, P["g_pre_mix"], P["wbig"], P["wsmall"], P["bsmall"]]
    if nt:
        in_specs.append(_layer_spec(l, nt, tn, K))
        args.append(P["wkv_t"])
    out_shape = [jax.ShapeDtypeStruct((nz, M, tn), F32), jax.ShapeDtypeStruct((M, LANES), F32)]
    out_specs = [pl.BlockSpec((nz, tm, tn), lambda i: (0, i, 0)), pl.BlockSpec((tm, LANES), lambda i: (i, 0))]
    if nt:
        tpb = rows_per_batch // tm
        for j in range(nt):
            if t_prev is not None:
                in_specs.append(pl.BlockSpec(memory_space=pl.ANY))
                args.append(t_prev[j])
                aliases[len(args) - 1] = len(out_shape)
            out_shape.append(jax.ShapeDtypeStruct((depth, M // rows_per_batch, tn, rows_per_batch), F32))
            out_specs.append(_slab_spec(l, depth, t_prev is None, (1, tn, tm), lambda i: (i // tpb, 0, i % tpb)))
    slab = (l, depth) if t_prev is None else (0, 1)
    return pl.pallas_call(
        functools.partial(_norm_proj_kernel, tn=tn, nz=nz, with_gates=True, nt=nt, n_alias=n_alias, slab=slab),
        grid=(M // tm,),
        in_specs=in_specs,
        out_specs=out_specs,
        out_shape=out_shape,
        input_output_aliases=aliases,
        compiler_params=_cparams(("arbitrary",)),
        name="norm_proj",
    )(*args)


def _mem_proj_kernel(x_ref, g_ref, w_ref, *refs, slab):
    k_ref, v_ref = refs[-2:]
    li, nl = slab
    xn = _rms(x_ref[...], g_ref[...]).astype(BF16)
    n = x_ref.shape[0]
    for j, o_ref in enumerate((k_ref, v_ref)):
        for k in range(nl):
            if k != li:
                o_ref[k] = jnp.zeros(o_ref.shape[1:], F32)
        for h in range(X_H):
            col = j * X_W + h * X_DH
            o_ref[li, 0, pl.ds(h, n, stride=X_H), :] = _dot(xn, w_ref[:, col:col + X_DH])


def _mem_proj(x, l, P, B, prev=None):
    M, K = x.shape
    n_mem = M // B
    depth = P["w_xkv"].shape[0]
    in_specs = [pl.BlockSpec((n_mem, K), lambda b: (b, 0)), _layer_spec(l, 1, K), _layer_spec(l, K, 2 * X_W)]
    args = [x, P["g_mem"], P["w_xkv"]]
    aliases = {}
    if prev is not None:
        for j in range(2):
            in_specs.append(pl.BlockSpec(memory_space=pl.ANY))
            args.append(prev[j])
            aliases[len(args) - 1] = j
    shape = jax.ShapeDtypeStruct((depth, B, n_mem * X_H, X_DH), F32)
    spec = _slab_spec(l, depth, prev is None, (1, n_mem * X_H, X_DH), lambda b: (b, 0, 0))
    return pl.pallas_call(
        functools.partial(_mem_proj_kernel, slab=(l, depth) if prev is None else (0, 1)),
        grid=(B,),
        in_specs=in_specs,
        out_specs=[spec, spec],
        out_shape=[shape, shape],
        input_output_aliases=aliases,
        compiler_params=_cparams(("arbitrary",)),
        name="mem_proj",
    )(*args)


def _tri_incl(n, upper):
    r = lax.broadcasted_iota(jnp.int32, (n, n), 0)
    c = lax.broadcasted_iota(jnp.int32, (n, n), 1)
    return ((r <= c) if upper else (c <= r)).astype(BF16)


BIAS_ROWS = 16


def _fox_cum_kernel(zs_ref, ft_ref, *, nb):
    n = nb * FOX_H
    lf = jnp.concatenate([zs_ref[u * LANES:(u + 1) * LANES, :].T[0:FOX_H, :] for u in range(nb)], axis=0)
    cum = _dot3_lhs(lf, _tri_incl(LANES, True))
    r = lax.broadcasted_iota(jnp.int32, (n, n), 0)
    c = lax.broadcasted_iota(jnp.int32, (n, n), 1)
    earlier = ((c % FOX_H == r % FOX_H) & (c // FOX_H < r // FOX_H)).astype(BF16)
    prefix = _dot3_rhs(earlier, jnp.broadcast_to(cum[:, LANES - 1:LANES], (n, LANES)))
    pieces = _split3((cum + prefix) * (-LOG2E))
    ro = lax.broadcasted_iota(jnp.int32, (nb * (FOX_H // 2) * BIAS_ROWS, n), 0)
    co = lax.broadcasted_iota(jnp.int32, (nb * (FOX_H // 2) * BIAS_ROWS, n), 1)
    blk = ro // ((FOX_H // 2) * BIAS_ROWS) == co // FOX_H
    pair = (ro // BIAS_ROWS) % (FOX_H // 2)
    slot = ro % BIAS_ROWS
    head = co % FOX_H
    out = None
    for idx, piece in enumerate(pieces):
        sel = blk & (((slot == idx) & (head == 2 * pair)) | ((slot == 3 + idx) & (head == 2 * pair + 1)))
        term = _dot(sel.astype(BF16), piece)
        out = term if out is None else out + term
    ft_ref[0] = out.reshape(nb, FOX_H // 2, BIAS_ROWS, LANES)


def _fox_cum(zs, B, S):
    nb = S // LANES
    return pl.pallas_call(
        functools.partial(_fox_cum_kernel, nb=nb),
        grid=(B,),
        in_specs=[pl.BlockSpec((S, LANES), lambda b: (b, 0))],
        out_specs=pl.BlockSpec((1, nb, FOX_H // 2, BIAS_ROWS, LANES), lambda b: (b, 0, 0, 0, 0)),
        out_shape=jax.ShapeDtypeStruct((B, nb, FOX_H // 2, BIAS_ROWS, LANES), F32),
        compiler_params=_cparams(("arbitrary",)),
        name="fox_cum",
    )(zs)


def _fox_prompt_kernel(ti_ref, tj_ref, q_ref, kt_ref, vt_ref, ft_ref, o_ref, m_s, l_s, acc_s, *, tq):
    t = pl.program_id(1)
    i = ti_ref[t]
    j = tj_ref[t]
    sub = tq // LANES
    npair = FOX_H // 2
    low = lax.broadcasted_iota(jnp.int32, (tq, LANES), 1) < FOX_DH

    @pl.when(j == 0)
    def _():
        m_s[...] = jnp.full(m_s.shape, NEG, F32)
        l_s[...] = jnp.zeros(l_s.shape, F32)
        acc_s[...] = jnp.zeros(acc_s.shape, F32)

    def update(masked):
        if masked:
            causal = (lax.broadcasted_iota(jnp.int32, (tq, tq), 1)
                      <= lax.broadcasted_iota(jnp.int32, (tq, tq), 0))
        for p in range(npair):
            rows = slice(p * LANES, (p + 1) * LANES)
            qp = q_ref[0, :, rows] * (FOX_DH ** -0.5 * LOG2E)
            bias = jnp.concatenate([ft_ref[0, u, p] for u in range(sub)], axis=1).astype(BF16)
            ka = jnp.concatenate(
                [kt_ref[rows, :].astype(BF16), bias, jnp.zeros((LANES - BIAS_ROWS, tq), BF16)], axis=0)
            vb = vt_ref[rows, :].astype(BF16)
            for e in range(2):
                h = 2 * p + e
                qh = jnp.where(low if e == 0 else jnp.logical_not(low), qp, 0.0)
                ones = ((bias_lane >= 3 * e) & (bias_lane < 3 * e + 3)).astype(F32)
                qa = jnp.concatenate([qh, ones], axis=1).astype(BF16)
                s = _dot(qa, ka)
                if masked:
                    s = jnp.where(causal, s, NEG)
                sc = [s[:, c * LANES:(c + 1) * LANES] for c in range(sub)]
                mx = functools.reduce(jnp.maximum, sc)
                m_prev = m_s[h]
                m_new = jnp.maximum(m_prev, jnp.max(mx, axis=1, keepdims=True))
                alpha = jnp.exp2(m_prev - m_new)
                ps = [jnp.exp2(c - m_new) for c in sc]
                l_s[h] = alpha * l_s[h] + jnp.sum(functools.reduce(jnp.add, ps), axis=1, keepdims=True)
                pr = jnp.concatenate([c.astype(BF16) for c in ps], axis=1)
                acc_s[h] = alpha * acc_s[h] + _dot_nt(pr, vb)
                m_s[h] = m_new

    bias_lane = lax.broadcasted_iota(jnp.int32, (tq, LANES), 1)

    @pl.when(j < i)
    def _():
        update(False)

    @pl.when(j == i)
    def _():
        update(True)
        for p in range(npair):
            oe = acc_s[2 * p] / l_s[2 * p]
            oo = acc_s[2 * p + 1] / l_s[2 * p + 1]
            o_ref[:, p * LANES:(p + 1) * LANES] = jnp.where(low, oe, oo).astype(o_ref.dtype)


def _fox_prompt(z3, kt, vt, ft, B, S, iq, layer, tq=ATTN_TILE):
    tq = min(tq, S)
    nq = S // tq
    M = B * S
    sub = tq // LANES
    pairs = [(i, j) for i in range(nq) for j in range(i + 1)]
    ti = jnp.asarray([p[0] for p in pairs], jnp.int32)
    tj = jnp.asarray([p[1] for p in pairs], jnp.int32)
    grid_spec = pltpu.PrefetchScalarGridSpec(
        num_scalar_prefetch=2,
        grid=(B, len(pairs)),
        in_specs=[
            pl.BlockSpec((1, tq, FOX_W), lambda b, t, ti, tj: (iq, b * nq + ti[t], 0)),
            pl.BlockSpec((None, None, FOX_W, tq), lambda b, t, ti, tj: (layer, b, 0, tj[t])),
            pl.BlockSpec((None, None, FOX_W, tq), lambda b, t, ti, tj: (layer, b, 0, tj[t])),
            pl.BlockSpec((1, sub, FOX_H // 2, BIAS_ROWS, LANES), lambda b, t, ti, tj: (b, tj[t], 0, 0, 0)),
        ],
        out_specs=pl.BlockSpec((tq, FOX_W), lambda b, t, ti, tj: (b * nq + ti[t], 0)),
        scratch_shapes=[
            pltpu.VMEM((FOX_H, tq, LANES), F32),
            pltpu.VMEM((FOX_H, tq, LANES), F32),
            pltpu.VMEM((FOX_H, tq, LANES), F32),
        ],
    )
    return pl.pallas_call(
        functools.partial(_fox_prompt_kernel, tq=tq),
        grid_spec=grid_spec,
        out_shape=jax.ShapeDtypeStruct((M, FOX_W), BF16),
        compiler_params=_cparams(("arbitrary", "arbitrary")),
        name="fox_prompt",
    )(ti, tj, z3, kt, vt, ft)


SKIP_LOG = -110.0
KEY_SLOTS = 3


def _fox_decode_kernel(pt_ref, q_ref, kn_ref, vn_ref, zs_ref, lf_hbm, k_hbm, v_hbm, o_ref,
                       kbuf, vbuf, lfbuf, sbuf, m_ref, l_ref, acc_ref, need_ref, ksem, vsem, lfsem,
                       *, layer, npg, ng):
    b = pl.program_id(0)
    nreal = GROUP // 2
    rows = nreal * FOX_H
    n = npg * FOX_H

    def copies(hbm, buf, sem, g, slot, req=None):
        req = b if req is None else req
        return [pltpu.make_async_copy(hbm.at[layer, pt_ref[req, g * npg + i]], buf.at[slot, i], sem.at[slot])
                for i in range(npg)]

    def start_keys(g, slot, req=None):
        for c in copies(lf_hbm, lfbuf, lfsem, g, slot, req) + copies(k_hbm, kbuf, ksem, g, slot, req):
            c.start()

    order = [ng - 1 - it for it in range(ng)]
    ahead = min(KEY_SLOTS - 1, ng)

    @pl.when(b == 0)
    def _():
        for it in range(ahead):
            start_keys(order[it], it)

    hrow = lax.broadcasted_iota(jnp.int32, (FOX_H, FOX_W), 0)
    hcol = lax.broadcasted_iota(jnp.int32, (FOX_H, FOX_W), 1) // FOX_DH
    head_sel = jnp.concatenate([hrow == hcol] * nreal, axis=0)
    q = q_ref[...] * (FOX_DH ** -0.5)
    qbd = jnp.concatenate(
        [jnp.broadcast_to(q[nreal + t:nreal + t + 1, :], (FOX_H, FOX_W)) for t in range(nreal)], axis=0)
    qbd = jnp.where(head_sel, qbd, 0.0).astype(BF16)
    eye = (lax.broadcasted_iota(jnp.int32, (rows, rows), 0)
           == lax.broadcasted_iota(jnp.int32, (rows, rows), 1))

    def as_row(col):
        return jnp.sum(jnp.where(eye, jnp.broadcast_to(col, (rows, rows)), 0.0), axis=0, keepdims=True)

    zpad = jnp.zeros((LANES - GROUP, FOX_W), F32)
    kn = jnp.concatenate([kn_ref[0], zpad], axis=0).astype(BF16)
    vn_t = jnp.concatenate([vn_ref[0], zpad], axis=0).T.astype(BF16)
    zs = jnp.concatenate([zs_ref[...], jnp.zeros((LANES - GROUP, LANES), F32)], axis=0)
    lf_new = zs.T[0:FOX_H, :]
    kr = lax.broadcasted_iota(jnp.int32, (LANES, LANES), 0)
    kc = lax.broadcasted_iota(jnp.int32, (LANES, LANES), 1)
    inc = ((kr <= kc) & (kr >= nreal) & (kr < GROUP)).astype(BF16)
    f_new = jnp.concatenate([_dot3_lhs(lf_new, inc)] * nreal, axis=0)
    qi = lax.broadcasted_iota(jnp.int32, (rows, LANES), 0) // FOX_H
    kj = lax.broadcasted_iota(jnp.int32, (rows, LANES), 1) - nreal
    s = jnp.where((kj >= 0) & (kj <= qi), _dot_nt(qbd, kn) - f_new, NEG)
    m0 = jnp.max(s, axis=1, keepdims=True)
    p0 = jnp.exp(s - m0)
    m_ref[...] = m0
    l_ref[...] = jnp.sum(p0, axis=1, keepdims=True)
    acc_ref[...] = _dot_nt(vn_t, p0.astype(BF16))

    er = lax.broadcasted_iota(jnp.int32, (n, n), 0)
    ec = lax.broadcasted_iota(jnp.int32, (n, n), 1)
    later_pages = ((ec % FOX_H == er % FOX_H) & (ec // FOX_H > er // FOX_H)).astype(BF16)

    def update(g, slot):
        @pl.when(need_ref[slot] == 1)
        def _():
            for c in copies(v_hbm, vbuf, vsem, g, slot):
                c.wait()
            vcat = jnp.concatenate([vbuf[slot, i].astype(BF16) for i in range(npg)], axis=1)
            sg = sbuf[slot]
            m = m_ref[...]
            m_new = jnp.maximum(m, jnp.max(sg, axis=1, keepdims=True))
            alpha = jnp.exp(m - m_new)
            pr = jnp.exp(sg - m_new)
            l_ref[...] = alpha * l_ref[...] + jnp.sum(pr, axis=1, keepdims=True)
            acc_ref[...] = as_row(alpha) * acc_ref[...] + _dot_nt(vcat, pr.astype(BF16))
            m_ref[...] = m_new

    carry = jnp.zeros((FOX_H, 1), F32)
    for it, g in enumerate(order):
        slot = it % 2
        kslot = it % KEY_SLOTS
        if it + KEY_SLOTS - 1 < ng:
            start_keys(order[it + KEY_SLOTS - 1], (it + KEY_SLOTS - 1) % KEY_SLOTS)
        for c in copies(lf_hbm, lfbuf, lfsem, g, kslot):
            c.wait()
        lf = jnp.concatenate([lfbuf[kslot, i] for i in range(npg)], axis=0)
        incl = _dot3_lhs(lf, _tri_incl(LANES, False))
        later = _dot3_rhs(later_pages, jnp.broadcast_to(incl[:, 0:1], (n, LANES)))
        after = incl - lf + later + jnp.concatenate([jnp.broadcast_to(carry, (FOX_H, LANES))] * npg, axis=0)
        carry = after[0:FOX_H, 0:1] + lf[0:FOX_H, 0:1]
        gcat = jnp.concatenate([after[i * FOX_H:(i + 1) * FOX_H, :] for i in range(npg)], axis=1)
        for c in copies(k_hbm, kbuf, ksem, g, kslot):
            c.wait()
        kcat = jnp.concatenate([kbuf[kslot, i].astype(BF16) for i in range(npg)], axis=1)
        sg = _dot(qbd, kcat) + jnp.concatenate([gcat] * nreal, axis=0)
        sbuf[slot] = sg
        need = jnp.max(jnp.max(sg, axis=1, keepdims=True) - m_ref[...]) > SKIP_LOG
        need_ref[slot] = need.astype(jnp.int32)

        @pl.when(need)
        def _():
            for c in copies(v_hbm, vbuf, vsem, g, slot):
                c.start()

        if it >= 1:
            update(order[it - 1], 1 - slot)

    @pl.when(b + 1 < pl.num_programs(0))
    def _():
        for it in range(ahead):
            start_keys(order[it], it, req=b + 1)

    update(order[ng - 1], (ng - 1) % 2)

    out_t = acc_ref[...] / as_row(l_ref[...])
    own = (lax.broadcasted_iota(jnp.int32, (FOX_W, rows), 0) // FOX_DH
           == lax.broadcasted_iota(jnp.int32, (FOX_W, rows), 1) % FOX_H)
    out_t = jnp.where(own, out_t, 0.0).astype(BF16)
    gr = lax.broadcasted_iota(jnp.int32, (GROUP, rows), 0) - nreal
    gc = lax.broadcasted_iota(jnp.int32, (GROUP, rows), 1) // FOX_H
    o_ref[...] = _dot_nt((gr == gc).astype(BF16), out_t)


def _fox_decode(z3, zs, cache_k, cache_v, cache_lf, page_table, layer, iq, ik, iv, npg=DECODE_PAGES):
    Bd, n_pages = page_table.shape
    npg = min(npg, n_pages)
    ng = n_pages // npg
    rows = (GROUP // 2) * FOX_H
    any_spec = pl.BlockSpec(memory_space=pl.ANY)
    grid_spec = pltpu.PrefetchScalarGridSpec(
        num_scalar_prefetch=1,
        grid=(Bd,),
        in_specs=[
            pl.BlockSpec((None, GROUP, FOX_W), lambda b, pt: (iq, b, 0)),
            pl.BlockSpec((1, GROUP, FOX_W), lambda b, pt: (ik, b, 0)),
            pl.BlockSpec((1, GROUP, FOX_W), lambda b, pt: (iv, b, 0)),
            pl.BlockSpec((GROUP, LANES), lambda b, pt: (b, 0)),
            any_spec, any_spec, any_spec,
        ],
        out_specs=pl.BlockSpec((GROUP, FOX_W), lambda b, pt: (b, 0)),
        scratch_shapes=[
            pltpu.VMEM((KEY_SLOTS, npg, FOX_W, PAGE), F32),
            pltpu.VMEM((2, npg, FOX_W, PAGE), F32),
            pltpu.VMEM((KEY_SLOTS, npg, FOX_H, PAGE), F32),
            pltpu.VMEM((2, rows, npg * PAGE), F32),
            pltpu.VMEM((rows, 1), F32),
            pltpu.VMEM((rows, 1), F32),
            pltpu.VMEM((FOX_W, rows), F32),
            pltpu.SMEM((2,), jnp.int32),
            pltpu.SemaphoreType.DMA((KEY_SLOTS,)),
            pltpu.SemaphoreType.DMA((2,)),
            pltpu.SemaphoreType.DMA((KEY_SLOTS,)),
        ],
    )
    return pl.pallas_call(
        functools.partial(_fox_decode_kernel, layer=layer, npg=npg, ng=ng),
        grid_spec=grid_spec,
        out_shape=jax.ShapeDtypeStruct((Bd * GROUP, FOX_W), F32),
        compiler_params=_cparams(("arbitrary",)),
        name="fox_decode",
    )(page_table, z3, z3, z3, zs, cache_lf, cache_k, cache_v)


def _mlstm_kernel(*refs, rin, sample, slab):
    qk_ref, v_ref, og_ref, zs_ref, wc_ref, bc_ref, gh_ref = refs[0:7]
    if sample:
        inj_ref, c0_ref, nm0_ref = refs[7:10]
    h_ref, c_out_ref, nm_out_ref, xbuf, c_s, nm_s = refs[-6:]
    t = pl.program_id(1)
    rc = LANES

    @pl.when(t == 0)
    def _():
        xbuf[0:8, :] = jnp.zeros((8, 2 * ML_W), F32)
        if sample:
            c_s[...] = c0_ref[0]
            nm_s[...] = nm0_ref[0]
        else:
            c_s[...] = jnp.zeros(c_s.shape, F32)
            nm_s[...] = jnp.zeros(nm_s.shape, F32)

    x = jnp.concatenate([qk_ref[0], qk_ref[1]], axis=1)
    if sample:
        slot = lax.broadcasted_iota(jnp.int32, (rin, 1), 0)
        x = jnp.where((slot >= 1) & (slot < GROUP // 2), inj_ref[...], x)
    xbuf[8:8 + rin, :] = x
    y = bc_ref[...] + wc_ref[3:4, :] * x
    for jj in range(ML_CONV - 1):
        y = y + wc_ref[jj:jj + 1, :] * xbuf[pl.ds(8 - (ML_CONV - 1) + jj, rin), :]
    xbuf[0:8, :] = xbuf[rin:rin + 8, :]
    y = y * _sigmoid(y)
    q_all = y[:, 0:ML_W]
    k_all = y[:, ML_W:] * (ML_DH ** -0.5)
    v_all = v_ref[0]
    zs = zs_ref[...]
    if rin < rc:
        pad = jnp.zeros((rc - rin, ML_W), F32)
        q_all = jnp.concatenate([q_all, pad], axis=0)
        k_all = jnp.concatenate([k_all, pad], axis=0)
        v_all = jnp.concatenate([v_all, pad], axis=0)
        zs = jnp.concatenate([zs, jnp.zeros((rc - rin, LANES), F32)], axis=0)
    if sample:
        row = lax.broadcasted_iota(jnp.int32, (rc, LANES), 0)
        lane = lax.broadcasted_iota(jnp.int32, (rc, LANES), 1)
        padrow = (row < GROUP // 2) | (row >= GROUP)
        is_i = (lane >= FOX_H) & (lane < FOX_H + ML_H)
        zs = jnp.where(padrow, jnp.where(is_i, NEG, 0.0), zs)

    causal = (lax.broadcasted_iota(jnp.int32, (rc, rc), 1) <= lax.broadcasted_iota(jnp.int32, (rc, rc), 0))
    for u in range(max(rin // rc, 1)):
        rs = slice(u * rc, (u + 1) * rc)
        zs_u = zs[rs, :]
        colcum = _dot3_rhs(_tri_incl(rc, False), zs_u)
        rowraw = zs_u.T[FOX_H:FOX_H + 2 * ML_H, :]
        rowcum = _dot3_lhs(rowraw, _tri_incl(rc, True))

        heads = []
        for h in range(ML_H):
            cs = slice(h * ML_DH, (h + 1) * ML_DH)
            qf = q_all[rs, cs]
            kf = k_all[rs, cs]
            qh = qf.astype(BF16)
            kh = kf.astype(BF16)
            vh = v_all[rs, cs].astype(BF16)
            icol = jnp.broadcast_to(zs_u[:, FOX_H + h:FOX_H + h + 1], (rc, LANES))
            bcol = jnp.broadcast_to(colcum[:, FOX_H + ML_H + h:FOX_H + ML_H + h + 1], (rc, LANES))
            irow = rowraw[h:h + 1, :]
            brow = rowcum[ML_H + h:ML_H + h + 1, :]
            cmat = c_s[h]
            nrow = nm_s[h, 0:1, :]
            m = nm_s[h, 1:2, :]

            dlog = jnp.where(causal, bcol - brow + irow, NEG)
            g = bcol + m
            mt = jnp.maximum(g, jnp.max(dlog, axis=1, keepdims=True))
            w_intra = jnp.exp(dlog - mt)
            w_inter = jnp.exp(g - mt)
            sc = _dot_nt(qh, kh) * w_intra
            num = _dot(sc.astype(BF16), vh) + _dot(qh, cmat.astype(BF16)) * w_inter
            den = jnp.sum(sc, axis=1, keepdims=True) + w_inter * jnp.sum(qf * nrow, axis=1, keepdims=True)
            hh = num / jnp.maximum(jnp.abs(den), jnp.exp(-mt))

            b_last = bcol[rc - 1:rc, :]
            a = b_last - bcol + icol
            m_new = jnp.maximum(b_last + m, jnp.max(a, axis=0, keepdims=True))
            ws = jnp.exp(a - m_new)
            decay = jnp.exp(b_last + m - m_new)
            kw = kf * ws
            c_s[h] = decay * cmat + _dot_tn(kw.astype(BF16), vh)
            nm_s[h, 0:1, :] = decay * nrow + jnp.sum(kw, axis=0, keepdims=True)
            nm_s[h, 1:2, :] = m_new

            heads.append(_rms(hh, gh_ref[:, cs]))
        hfull = jnp.concatenate(heads, axis=1)
        if sample:
            h_ref[...] = (hfull[0:rin] * _sigmoid(og_ref[0])).astype(h_ref.dtype)
        else:
            h_ref[rs, :] = (hfull * _sigmoid(og_ref[0, rs, :])).astype(h_ref.dtype)

    @pl.when(t == pl.num_programs(1) - 1)
    def _():
        _store_slab(c_out_ref, slab, c_s[...][None])
        nm_out_ref[0] = nm_s[...]


def _mlstm(z3, zs, l, P, B, L, iqk, iv, io, state=None, c_prev=None):
    sample = state is not None
    rin = GROUP if sample else math.gcd(L, 4 * LANES)
    nt = L // rin
    M = B * L
    depth = P["w_ml_conv"].shape[0]
    in_specs = [
        pl.BlockSpec((2, rin, ML_W), lambda b, t: (iqk, b * nt + t, 0)),
        pl.BlockSpec((1, rin, ML_W), lambda b, t: (iv, b * nt + t, 0)),
        pl.BlockSpec((1, rin, ML_W), lambda b, t: (io, b * nt + t, 0)),
        pl.BlockSpec((rin, LANES), lambda b, t: (b * nt + t, 0)),
        _layer_spec(l, ML_CONV, 2 * ML_W),
        _layer_spec(l, 1, 2 * ML_W),
        _layer_spec(l, 1, ML_W),
    ]
    args = [z3, z3, z3, zs, P["w_ml_conv"], P["b_ml_conv"], P["g_ml_head"]]
    if sample:
        in_specs += [
            pl.BlockSpec((None, rin, 2 * ML_W), lambda b, t: (l, b * nt + t, 0)),
            pl.BlockSpec((None, 1, ML_H, ML_DH, ML_DH), lambda b, t: (l, b, 0, 0, 0)),
            pl.BlockSpec((None, 1, ML_H, 8, ML_DH), lambda b, t: (l, b, 0, 0, 0)),
        ]
        args += list(state)
    aliases = {}
    if c_prev is not None:
        in_specs.append(pl.BlockSpec(memory_space=pl.ANY))
        args.append(c_prev)
        aliases[len(args) - 1] = 1
    return pl.pallas_call(
        functools.partial(_mlstm_kernel, rin=rin, sample=sample,
                          slab=(l, depth) if c_prev is None else (0, 1)),
        grid=(B, nt),
        in_specs=in_specs,
        out_specs=[
            pl.BlockSpec((rin, ML_W), lambda b, t: (b * nt + t, 0)),
            _slab_spec(l, depth, c_prev is None, (1, ML_H, ML_DH, ML_DH), lambda b, t: (b, 0, 0, 0)),
            pl.BlockSpec((1, ML_H, 8, ML_DH), lambda b, t: (b, 0, 0, 0)),
        ],
        out_shape=[
            jax.ShapeDtypeStruct((M, ML_W), F32 if sample else BF16),
            jax.ShapeDtypeStruct((depth, B, ML_H, ML_DH, ML_DH), F32),
            jax.ShapeDtypeStruct((B, ML_H, 8, ML_DH), F32),
        ],
        input_output_aliases=aliases,
        scratch_shapes=[
            pltpu.VMEM((rin + 8, 2 * ML_W), F32),
            pltpu.VMEM((ML_H, ML_DH, ML_DH), F32),
            pltpu.VMEM((ML_H, 8, ML_DH), F32),
        ],
        compiler_params=_cparams(("arbitrary", "arbitrary")),
        name="mlstm_sample" if sample else "mlstm_prompt",
    )(*args)


def _mix_xattn_kernel(fo_ref, hm_ref, x_ref, wo_ref, gpm_ref, gpx_ref, wq_ref, mk_ref, mv_ref, wxo_ref, gox_ref, o_ref):
    mix = (_dot(fo_ref[...].astype(BF16), wo_ref[0:FOX_W, :])
           + _dot(hm_ref[...].astype(BF16), wo_ref[FOX_W:, :]))
    x1 = x_ref[...] + _rms(mix, gpm_ref[...])
    hq = _rms(x1, gpx_ref[...]).astype(BF16)
    xq = _dot(hq, wq_ref[...])
    nb = mk_ref.shape[0]
    rt = xq.shape[0] // nb
    n_mem = mk_ref.shape[1] // X_H
    parts = []
    for j in range(nb):
        heads = []
        for h in range(X_H):
            cs = slice(h * X_DH, (h + 1) * X_DH)
            mk = mk_ref[j, pl.ds(h, n_mem, stride=X_H), :].astype(BF16)
            mv = mv_ref[j, pl.ds(h, n_mem, stride=X_H), :].astype(BF16)
            s = _dot_nt(xq[j * rt:(j + 1) * rt, cs].astype(BF16), mk) * (X_DH ** -0.5)
            e = jnp.exp(s - jnp.max(s, axis=1, keepdims=True))
            heads.append(_dot(e.astype(BF16), mv) / jnp.sum(e, axis=1, keepdims=True))
        parts.append(jnp.concatenate(heads, axis=1))
    xo = _dot((parts[0] if nb == 1 else jnp.concatenate(parts, axis=0)).astype(BF16), wxo_ref[...])
    o_ref[...] = x1 + _rms(xo, gox_ref[...])


def _mix_xattn(fo, hm, x, l, P, mk, mv, rows_per_batch, tm):
    M, D = x.shape
    if tm <= rows_per_batch:
        tpb = rows_per_batch // tm
        mem_spec = pl.BlockSpec((None, 1) + mk.shape[2:], lambda i: (l, i // tpb, 0, 0))
    else:
        nb = tm // rows_per_batch
        mem_spec = pl.BlockSpec((None, nb) + mk.shape[2:], lambda i: (l, i, 0, 0))
    return pl.pallas_call(
        _mix_xattn_kernel,
        grid=(M // tm,),
        in_specs=[
            pl.BlockSpec((tm, FOX_W), lambda i: (i, 0)),
            pl.BlockSpec((tm, ML_W), lambda i: (i, 0)),
            pl.BlockSpec((tm, D), lambda i: (i, 0)),
            _layer_spec(l, FOX_W + ML_W, D),
            _layer_spec(l, 1, D), _layer_spec(l, 1, D),
            _layer_spec(l, D, X_W),
            mem_spec, mem_spec,
            _layer_spec(l, X_W, D),
            _layer_spec(l, 1, D),
        ],
        out_specs=pl.BlockSpec((tm, D), lambda i: (i, 0)),
        out_shape=jax.ShapeDtypeStruct((M, D), F32),
        compiler_params=_cparams(("arbitrary",)),
        name="mix_xattn",
    )(fo, hm, x, P["w_out"], P["g_post_mix"], P["g_pre_x"], P["w_xq"], mk, mv, P["w_xo"], P["g_post_x"])


def _convffn_kernel(*refs, tm, tiles_per_batch, sample):
    if sample:
        (x_ref, gpre_ref, wg_ref, wu_ref, wc_ref, bc_ref, wd_ref, gpost_ref, inj_ref,
         o_ref, gate_ref, xn_ref, acc_ref, gbuf, tails) = refs
    else:
        (x_ref, gpre_ref, wg_ref, wu_ref, wc_ref, bc_ref, wd_ref, gpost_ref,
         o_ref, gate_ref, xn_ref, acc_ref, gbuf, tails) = refs
    i = pl.program_id(0)
    j = pl.program_id(1)

    @pl.when(j == 0)
    def _():
        xn_ref[...] = _rms(x_ref[...], gpre_ref[...]).astype(BF16)
        acc_ref[...] = jnp.zeros_like(acc_ref)

    xn = xn_ref[...]
    g = _dot(xn, wg_ref[...])
    if sample:
        slot = lax.broadcasted_iota(jnp.int32, (tm, 1), 0) % GROUP
        g = jnp.where((slot >= GROUP // 2 - (FFN_CONV - 1)) & (slot < GROUP // 2), inj_ref[...], g)
        gate_ref[...] = g
        gbuf[0:8, :] = jnp.zeros((8, g.shape[1]), F32)
    else:
        first = (i % tiles_per_batch) == 0
        gbuf[0:8, :] = jnp.where(first, 0.0, tails[j])
        gate_ref[0] = g[tm - 8:tm, :]
        tails[j] = g[tm - 8:tm, :]
    gbuf[8:8 + tm, :] = g
    t = bc_ref[...] + wc_ref[2:3, :] * g
    for jj in range(FFN_CONV - 1):
        t = t + wc_ref[jj:jj + 1, :] * gbuf[pl.ds(8 - (FFN_CONV - 1) + jj, tm), :]
    a = _gelu_tanh(t) * _dot(xn, wu_ref[...])
    acc_ref[...] += _dot(a.astype(BF16), wd_ref[...])

    @pl.when(j == pl.num_programs(1) - 1)
    def _():
        o_ref[...] = x_ref[...] + _rms(acc_ref[...], gpost_ref[...])


def _convffn(x, l, P, rows_per_batch, tm, tn, inj=None):
    M, D = x.shape
    F = P["w_gate"].shape[2]
    sample = inj is not None
    nj = F // tn
    tiles_per_batch = max(rows_per_batch // tm, 1)
    in_specs = [
        pl.BlockSpec((tm, D), lambda i, j: (i, 0)),
        _layer_spec(l, 1, D),
        pl.BlockSpec((None, D, tn), lambda i, j: (l, 0, j)),
        pl.BlockSpec((None, D, tn), lambda i, j: (l, 0, j)),
        pl.BlockSpec((None, FFN_CONV, tn), lambda i, j: (l, 0, j)),
        pl.BlockSpec((None, 1, tn), lambda i, j: (l, 0, j)),
        pl.BlockSpec((None, tn, D), lambda i, j: (l, j, 0)),
        _layer_spec(l, 1, D),
    ]
    args = [x, P["g_pre_ffn"], P["w_gate"], P["w_up"], P["w_ffn_conv"], P["b_ffn_conv"], P["w_down"],
            P["g_post_ffn"]]
    if sample:
        in_specs.append(pl.BlockSpec((None, tm, tn), lambda i, j: (l, i, j)))
        args.append(inj)
        gate_shape = jax.ShapeDtypeStruct((M, F), F32)
        gate_spec = pl.BlockSpec((tm, tn), lambda i, j: (i, j))
    else:
        gate_shape = jax.ShapeDtypeStruct((M // tm, 8, F), F32)
        gate_spec = pl.BlockSpec((1, 8, tn), lambda i, j: (i, 0, j))
    return pl.pallas_call(
        functools.partial(_convffn_kernel, tm=tm, tiles_per_batch=tiles_per_batch, sample=sample),
        grid=(M // tm, nj),
        in_specs=in_specs,
        out_specs=[pl.BlockSpec((tm, D), lambda i, j: (i, 0)), gate_spec],
        out_shape=[jax.ShapeDtypeStruct((M, D), F32), gate_shape],
        scratch_shapes=[
            pltpu.VMEM((tm, D), BF16),
            pltpu.VMEM((tm, D), F32),
            pltpu.VMEM((tm + 8, tn), F32),
            pltpu.VMEM((nj, 8, tn), F32),
        ],
        compiler_params=_cparams(("arbitrary", "arbitrary")),
        name="convffn_sample" if sample else "convffn_prompt",
    )(*args)


_IQK, _IMV, _IMO, _IFQ, _IFK, _IFV = 0, 2, 3, 4, 5, 6


def _prep_params(w_in, b_fox_f, b_ml_i, b_ml_f, w_out, w_xq, w_xk, w_xv, w_xo, w_gate, w_up, w_down, vectors):
    depth, K, _ = w_in.shape
    o_ff = 3 * FOX_W
    o_mq = o_ff + FOX_H
    o_mi = o_mq + 3 * ML_W
    o_mo = o_mi + 2 * ML_H
    n_gate = FOX_H + 2 * ML_H
    P = dict(
        wbig=jnp.concatenate([w_in[:, :, o_mq:o_mi], w_in[:, :, o_mo:], w_in[:, :, :o_ff]], axis=2).astype(BF16),
        wsmall=jnp.concatenate([w_in[:, :, o_ff:o_mq], w_in[:, :, o_mi:o_mo],
                                jnp.zeros((depth, K, LANES - n_gate), F32)], axis=2).astype(BF16),
        bsmall=jnp.concatenate([b_fox_f, b_ml_i, b_ml_f, jnp.zeros((depth, LANES - n_gate), F32)],
                               axis=1).reshape(depth, 1, LANES),
        wkv_t=jnp.transpose(w_in[:, :, FOX_W:o_ff].reshape(depth, K, 2, FOX_W), (0, 2, 3, 1)).astype(BF16),
        w_out=w_out.astype(BF16), w_xq=w_xq.astype(BF16),
        w_xkv=jnp.concatenate([w_xk, w_xv], axis=2).astype(BF16),
        w_xo=w_xo.astype(BF16), w_gate=w_gate.astype(BF16), w_up=w_up.astype(BF16), w_down=w_down.astype(BF16))
    for name, v in vectors.items():
        P[name] = v.reshape(depth, 1, v.shape[-1]) if v.ndim == 2 else v
    return P


def kernel(x_prompt, x_sample, mem_prompt, cache_fox_k, cache_fox_v, cache_fox_logf, state_ml_C, state_ml_n, state_ml_m, state_ml_conv, state_ffn_conv, cache_mem_k, cache_mem_v, page_table, g_pre_mix, w_in, b_fox_f, w_ml_conv, b_ml_conv, b_ml_i, b_ml_f, g_ml_head, w_out, g_post_mix, g_pre_x, g_mem, w_xq, w_xk, w_xv, w_xo, g_post_x, g_pre_ffn, w_gate, w_up, w_ffn_conv, b_ffn_conv, w_down, g_post_ffn):
    B, S, D = x_prompt.shape
    Bd, Ld, _ = x_sample.shape
    depth = w_in.shape[0]
    n_mem = mem_prompt.shape[1]
    d_ff = w_gate.shape[2]
    pool = cache_fox_k.shape[1]
    pad = GROUP - Ld
    assert Ld == GROUP // 2 and S % 256 == 0

    xp = x_prompt.reshape(B * S, D)
    xs = jnp.pad(x_sample, ((0, 0), (pad, 0), (0, 0))).reshape(Bd * GROUP, D)
    memx = mem_prompt.reshape(B * n_mem, D)
    ck = jnp.transpose(cache_fox_k, (0, 1, 3, 4, 2)).reshape(depth, pool, FOX_W, PAGE)
    cv = jnp.transpose(cache_fox_v, (0, 1, 3, 4, 2)).reshape(depth, pool, FOX_W, PAGE)
    clf = jnp.transpose(cache_fox_logf, (0, 1, 3, 2))
    ffn_tn = d_ff // 2 if (d_ff // 2) % LANES == 0 else d_ff

    P = _prep_params(
        w_in, b_fox_f, b_ml_i, b_ml_f, w_out, w_xq, w_xk, w_xv, w_xo, w_gate, w_up, w_down,
        dict(g_pre_mix=g_pre_mix, g_mem=g_mem, w_ml_conv=w_ml_conv, b_ml_conv=b_ml_conv, g_ml_head=g_ml_head,
             g_post_mix=g_post_mix, g_pre_x=g_pre_x, g_post_x=g_post_x, g_pre_ffn=g_pre_ffn,
             w_ffn_conv=w_ffn_conv, b_ffn_conv=b_ffn_conv, g_post_ffn=g_post_ffn))
    inj_ml = jnp.pad(state_ml_conv, ((0, 0), (0, 0), (1, GROUP - ML_CONV), (0, 0))).reshape(depth, Bd * GROUP, 2 * ML_W)
    inj_ffn = jnp.pad(state_ffn_conv, ((0, 0), (0, 0), (GROUP // 2 - (FFN_CONV - 1), GROUP // 2), (0, 0)))
    inj_ffn = inj_ffn.reshape(depth, Bd * GROUP, d_ff)
    nm0 = jnp.concatenate(
        [state_ml_n[:, :, :, None, :],
         jnp.broadcast_to(state_ml_m[:, :, :, None, None], (depth, Bd, ML_H, 1, ML_DH)),
         jnp.zeros((depth, Bd, ML_H, 6, ML_DH), F32)], axis=3)
    smk = cache_mem_k.reshape(depth, Bd, n_mem * X_H, X_DH)
    smv = cache_mem_v.reshape(depth, Bd, n_mem * X_H, X_DH)

    outs = {k: [] for k in ("lf_p", "fk_s", "fv_s", "lf_s", "nm_p", "cv_p", "nm_s", "cv_s", "fc_p", "fc_s")}
    kv_p = mem_p = c_p = c_s = None
    for l in range(depth):
        mem_p = _mem_proj(memx, l, P, B, prev=mem_p)
        z3, zs, *kv_p = _norm_proj(xp, l, P, _IFK, FOX_W, rows_per_batch=S, t_prev=kv_p)
        ft = _fox_cum(zs, B, S)
        fo = _fox_prompt(z3, kv_p[0], kv_p[1], ft, B, S, _IFQ, l)
        hm, c_p, nm_new = _mlstm(z3, zs, l, P, B, S, _IQK, _IMV, _IMO, c_prev=c_p)
        xp = _mix_xattn(fo, hm, xp, l, P, mem_p[0], mem_p[1], S, min(2 * ROW_TILE, S))
        xp, gate_tail = _convffn(xp, l, P, S, min(ROW_TILE, S), ffn_tn)
        outs["lf_p"].append(zs[:, :FOX_H].reshape(B, S, FOX_H))
        outs["nm_p"].append(nm_new)
        tails = z3.reshape(z3.shape[0], B, S, ML_W)[0:2, :, S - (ML_CONV - 1):, :]
        outs["cv_p"].append(jnp.transpose(tails, (1, 2, 0, 3)).reshape(B, ML_CONV - 1, 2 * ML_W))
        outs["fc_p"].append(gate_tail.reshape(B, -1, 8, d_ff)[:, -1, 8 - (FFN_CONV - 1):, :])

        z3, zs = _norm_proj(xs, l, P, _IFV + 1, FOX_W)
        fo = _fox_decode(z3, zs, ck, cv, clf, page_table, l, _IFQ, _IFK, _IFV)
        hm, c_s, nm_new = _mlstm(z3, zs, l, P, Bd, GROUP, _IQK, _IMV, _IMO,
                                 state=(inj_ml, state_ml_C, nm0), c_prev=c_s)
        xs = _mix_xattn(fo, hm, xs, l, P, smk, smv, GROUP, GROUP * math.gcd(Bd, 8))
        xs, gate_full = _convffn(xs, l, P, Bd * GROUP, Bd * GROUP, ffn_tn, inj=inj_ffn)
        real = lambda a: a.reshape((Bd, GROUP) + a.shape[1:])[:, pad:]
        outs["fk_s"].append(real(z3[_IFK]).reshape(Bd, Ld, FOX_H, FOX_DH))
        outs["fv_s"].append(real(z3[_IFV]).reshape(Bd, Ld, FOX_H, FOX_DH))
        outs["lf_s"].append(real(zs)[:, :, :FOX_H])
        outs["nm_s"].append(nm_new)
        tails = z3.reshape(z3.shape[0], Bd, GROUP, ML_W)[0:2, :, GROUP - (ML_CONV - 1):, :]
        outs["cv_s"].append(jnp.transpose(tails, (1, 2, 0, 3)).reshape(Bd, ML_CONV - 1, 2 * ML_W))
        outs["fc_s"].append(gate_full.reshape(Bd, GROUP, d_ff)[:, GROUP - (FFN_CONV - 1):, :])

    st = lambda k: jnp.stack(outs[k], axis=0)
    y_p = xp.reshape(B, S, D)
    y_s = xs.reshape(Bd, GROUP, D)[:, pad:]
    to_tokens = lambda a: jnp.transpose(a.reshape(depth, B, FOX_H, FOX_DH, S), (0, 1, 4, 2, 3))
    nm_p, nm_s = st("nm_p"), st("nm_s")
    mem_leaf = lambda a: a.reshape(depth, B, n_mem, X_H, X_DH)
    return (y_p, y_s, to_tokens(kv_p[0]), to_tokens(kv_p[1]), st("lf_p"), st("fk_s"), st("fv_s"), st("lf_s"),
            c_p, nm_p[:, :, :, 0, :], nm_p[:, :, :, 1, 0], st("cv_p"),
            c_s, nm_s[:, :, :, 0, :], nm_s[:, :, :, 1, 0], st("cv_s"),
            st("fc_p"), st("fc_s"), mem_leaf(mem_p[0]), mem_leaf(mem_p[1]))
```
